```python
import math
import jax, jax.numpy as jnp
from jax import lax
import numpy as np

D_MODEL = 4096
BATCH = 4
SEQ = 4096
DEPTH = 1

MEM_LEN = 256
EPS = 1e-6

DA_HEADS = 16
DA_HALF = 64
DA_VDIM = 2 * DA_HALF
DA_WIDTH = DA_HEADS * DA_VDIM
HG_HEADS = 16
HG_DK = 128
HG_DV = 128
HG_WIDTH = HG_HEADS * HG_DV
HG_CHUNK = 64
MIX_WIDTH = DA_WIDTH + HG_WIDTH
IN_SIZES = (DA_HEADS * 2 * DA_HALF, DA_HEADS * 2 * DA_HALF, DA_WIDTH,
            HG_HEADS * HG_DK, HG_HEADS * HG_DK, HG_WIDTH, HG_WIDTH)
IN_WIDTH = sum(IN_SIZES)
IN_OFFSETS = tuple(int(o) for o in np.cumsum(IN_SIZES)[:-1])
Q_BLOCK = 128
XA_HEADS = 4
XA_HDIM = D_MODEL // XA_HEADS
PEER_HEADS = 8
PEER_NKEYS = 128
PEER_N = PEER_NKEYS * PEER_NKEYS
PEER_QDIM = 256
PEER_HALF = PEER_QDIM // 2
PEER_TOPK = 16
PEER_TOK_BLOCK = 128

kernel_name = "hybrid_diffattn_hgrn2_peer_block"


def rmsnorm(x, g):
    xf = x.astype(jnp.float32)
    y = xf * lax.rsqrt(jnp.mean(xf * xf, axis=-1, keepdims=True) + EPS)
    return (y * g.astype(jnp.float32)).astype(x.dtype)


def alibi_slopes(n_heads):
    return jnp.asarray(np.array([2.0 ** (-8.0 * (h + 1) / n_heads) for h in range(n_heads)], dtype=np.float32))


def diff_attention(q, k, v, lq1, lk1, lq2, lk2, subln_g, layer_idx):
    B, S = q.shape[0], q.shape[1]
    lam_init = 0.8 - 0.6 * math.exp(-0.3 * layer_idx)
    f32 = jnp.float32
    lam = (jnp.exp(jnp.sum(lq1.astype(f32) * lk1.astype(f32)))
           - jnp.exp(jnp.sum(lq2.astype(f32) * lk2.astype(f32))) + lam_init)
    scale = DA_HALF ** -0.5
    slopes = alibi_slopes(DA_HEADS)
    kpos = jnp.arange(S)
    n_blocks = S // Q_BLOCK
    qb = q.reshape(B, n_blocks, Q_BLOCK, DA_HEADS, 2, DA_HALF).transpose(1, 0, 2, 3, 4, 5)

    def block(args):
        qblk, blk = args
        qpos = blk * Q_BLOCK + jnp.arange(Q_BLOCK)
        dist = (qpos[:, None] - kpos[None, :]).astype(f32)
        bias = jnp.where(dist[None] >= 0, -slopes[:, None, None] * dist[None], -jnp.inf)
        s = jnp.einsum('bqhcd,bkhcd->bchqk', qblk, k).astype(f32) * scale + bias[None, None]
        p = jax.nn.softmax(s, axis=-1)
        attn = p[:, 0] - lam * p[:, 1]
        return jnp.einsum('bhqk,bkhd->bqhd', attn.astype(v.dtype), v)

    out = lax.map(block, (qb, jnp.arange(n_blocks)))
    out = out.transpose(1, 0, 2, 3, 4).reshape(B, S, DA_HEADS, DA_VDIM)
    out = rmsnorm(out, subln_g) * (1.0 - lam_init)
    return out.reshape(B, S, DA_WIDTH)


def hgrn2(q, f_logit, i, g, lb, norm_g):
    B, S = q.shape[0], q.shape[1]
    f32 = jnp.float32
    qf = jax.nn.silu(q.astype(f32))
    f = lb + (1.0 - lb) * jax.nn.sigmoid(f_logit.astype(f32))
    logf = jnp.log(f)
    kf = 1.0 - f
    nc = S // HG_CHUNK

    def to_chunks(t):
        return t.reshape(B, nc, HG_CHUNK, HG_HEADS, t.shape[-1]).transpose(1, 0, 3, 2, 4)

    causal = jnp.tril(jnp.ones((HG_CHUNK, HG_CHUNK), dtype=bool))

    def step(state, inp):
        qc, kc, lfc, ic = inp
        b = jnp.cumsum(lfc, axis=2)
        dec = b[:, :, :, None, :] - b[:, :, None, :, :]
        dec = jnp.where(causal[None, None, :, :, None], dec, -jnp.inf)
        a = jnp.einsum('bhtd,bhsd,bhtsd->bhts', qc, kc, jnp.exp(dec))
        o = (jnp.einsum('bhts,bhsv->bhtv', a, ic)
             + jnp.einsum('bhtd,bhdv->bhtv', qc * jnp.exp(b), state))
        b_end = b[:, :, -1:, :]
        new_state = (jnp.exp(b_end[:, :, 0, :])[..., None] * state
                     + jnp.einsum('bhsd,bhsv->bhdv', kc * jnp.exp(b_end - b), ic))
        return new_state, o

    s0 = jnp.zeros((B, HG_HEADS, HG_DK, HG_DV), f32)
    _, o = lax.scan(step, s0, (to_chunks(qf), to_chunks(kf), to_chunks(logf), to_chunks(i.astype(f32))))
    o = o.transpose(1, 0, 3, 2, 4).reshape(B, S, HG_HEADS, HG_DV)
    o = rmsnorm(o, norm_g) * jax.nn.silu(g.astype(f32))
    return o.reshape(B, S, HG_WIDTH).astype(q.dtype)


def cross_attention(hn, mem_n, w_q, w_kv, w_o):
    B, S, _ = hn.shape
    M = mem_n.shape[1]
    q = (hn @ w_q).reshape(B, S, XA_HEADS, XA_HDIM)
    kv = (mem_n @ w_kv).reshape(B, M, 2, XA_HEADS, XA_HDIM)
    k, v = kv[:, :, 0], kv[:, :, 1]
    s = jnp.einsum('bqhd,bkhd->bhqk', q, k).astype(jnp.float32) * (XA_HDIM ** -0.5)
    p = jax.nn.softmax(s, axis=-1)
    o = jnp.einsum('bhqk,bkhd->bqhd', p.astype(v.dtype), v).reshape(B, S, D_MODEL)
    return o @ w_o


def peer(hn, w_q, sub_keys, down, up):
    B, S, D = hn.shape
    tok = hn.reshape(-1, PEER_TOK_BLOCK, D)

    def block(xb):
        t = xb.shape[0]
        q = (xb @ w_q).reshape(t, PEER_HEADS, 2, PEER_HALF)
        s = jnp.einsum('thcd,hckd->thck', q, sub_keys).astype(jnp.float32)
        sv, si = lax.top_k(s, PEER_TOPK)
        cand = (sv[:, :, 0, :, None] + sv[:, :, 1, None, :]).reshape(t, PEER_HEADS, PEER_TOPK * PEER_TOPK)
        cidx = (si[:, :, 0, :, None] * PEER_NKEYS + si[:, :, 1, None, :]).reshape(t, PEER_HEADS, PEER_TOPK * PEER_TOPK)
        best, pos = lax.top_k(cand, PEER_TOPK)
        eidx = jnp.take_along_axis(cidx, pos, axis=-1)
        gate = jax.nn.softmax(best, axis=-1)
        u = down[eidx]
        act = jax.nn.gelu(jnp.einsum('td,thkd->thk', xb, u).astype(jnp.float32), approximate=False)
        w = (gate * act).astype(xb.dtype)
        return jnp.einsum('thk,thkd->td', w, up[eidx])

    return lax.map(block, tok).reshape(B, S, D)


def setup_inputs(seed: int = 0) -> dict:
    key = jax.random.key(seed)
    ks = jax.random.split(key, 24)
    f32 = jnp.float32
    D = D_MODEL

    def nrm(k, shape, scale):
        return jax.random.normal(k, shape, f32) * scale

    def gain(k, shape):
        return 1.0 + 0.02 * jax.random.normal(k, shape, f32)

    return {
        "x": nrm(ks[0], (BATCH, SEQ, D), 1.0),
        "mem": nrm(ks[1], (BATCH, MEM_LEN, D), 1.0),
        "mix_norm_g": gain(ks[2], (DEPTH, D)),
        "w_in": nrm(ks[3], (DEPTH, D, IN_WIDTH), D ** -0.5),
        "da_lambda_q1": nrm(ks[4], (DEPTH, DA_HALF), 0.1),
        "da_lambda_k1": nrm(ks[5], (DEPTH, DA_HALF), 0.1),
        "da_lambda_q2": nrm(ks[6], (DEPTH, DA_HALF), 0.1),
        "da_lambda_k2": nrm(ks[7], (DEPTH, DA_HALF), 0.1),
        "da_subln_g": gain(ks[8], (DEPTH, DA_VDIM)),
        "hgrn_gamma": nrm(ks[9], (DEPTH + 1, HG_HEADS * HG_DK), 0.5),
        "hgrn_norm_g": gain(ks[10], (DEPTH, HG_DV)),
        "w_out": nrm(ks[11], (DEPTH, MIX_WIDTH, D), MIX_WIDTH ** -0.5),
        "cross_norm_g": gain(ks[12], (DEPTH, D)),
        "mem_norm_g": gain(ks[13], (DEPTH, D)),
        "w_cq": nrm(ks[14], (DEPTH, D, D), D ** -0.5),
        "w_ckv": nrm(ks[15], (DEPTH, D, 2 * D), D ** -0.5),
        "w_co": nrm(ks[16], (DEPTH, D, D), D ** -0.5),
        "ffn_norm_g": gain(ks[17], (DEPTH, D)),
        "peer_wq": nrm(ks[18], (DEPTH, D, PEER_HEADS * PEER_QDIM), D ** -0.5),
        "peer_subkeys": nrm(ks[19], (DEPTH, PEER_HEADS, 2, PEER_NKEYS, PEER_HALF), PEER_HALF ** -0.5),
        "peer_down": nrm(ks[20], (DEPTH, PEER_N, D), D ** -0.5),
        "peer_up": nrm(ks[21], (DEPTH, PEER_N, D), 0.5),
        "final_norm_g": gain(ks[22], (D,)),
    }


def reference(x, mem, mix_norm_g, w_in, da_lambda_q1, da_lambda_k1, da_lambda_q2, da_lambda_k2,
              da_subln_g, hgrn_gamma, hgrn_norm_g, w_out, cross_norm_g, mem_norm_g, w_cq, w_ckv, w_co,
              ffn_norm_g, peer_wq, peer_subkeys, peer_down, peer_up, final_norm_g):
    B, S, _ = x.shape
    lb_all = jnp.cumsum(jax.nn.softmax(hgrn_gamma.astype(jnp.float32), axis=0), axis=0)
    h = x
    for l in range(DEPTH):
        xn = rmsnorm(h, mix_norm_g[l])
        proj = xn @ w_in[l]
        dq, dk, dv, hq, hf, hi, hg = jnp.split(proj, IN_OFFSETS, axis=-1)
        a_out = diff_attention(
            dq.reshape(B, S, DA_HEADS, 2, DA_HALF), dk.reshape(B, S, DA_HEADS, 2, DA_HALF),
            dv.reshape(B, S, DA_HEADS, DA_VDIM),
            da_lambda_q1[l], da_lambda_k1[l], da_lambda_q2[l], da_lambda_k2[l], da_subln_g[l], l)
        b_out = hgrn2(
            hq.reshape(B, S, HG_HEADS, HG_DK), hf.reshape(B, S, HG_HEADS, HG_DK),
            hi.reshape(B, S, HG_HEADS, HG_DV), hg.reshape(B, S, HG_HEADS, HG_DV),
            lb_all[l].reshape(HG_HEADS, HG_DK), hgrn_norm_g[l])
        h = h + jnp.concatenate([a_out, b_out], axis=-1) @ w_out[l]
        h = h + cross_attention(rmsnorm(h, cross_norm_g[l]), rmsnorm(mem, mem_norm_g[l]),
                                w_cq[l], w_ckv[l], w_co[l])
        h = h + peer(rmsnorm(h, ffn_norm_g[l]), peer_wq[l], peer_subkeys[l], peer_down[l], peer_up[l])
    return rmsnorm(h, final_norm_g)
```

```python
import functools
import math

import numpy as np
import jax
import jax.numpy as jnp
from jax import lax
from jax.experimental import pallas as pl
from jax.experimental.pallas import tpu as pltpu

F32 = jnp.float32
BF16 = jnp.bfloat16
EPS = 1e-6
NEG_BIG = -1e30

DA_HEADS = 16
DA_HALF = 64
HEAD_W = 128
HG_HEADS = 16
HG_CHUNK = 64
HG_SUB = 16
XA_HEADS = 4
PEER_HEADS = 8
PEER_NKEYS = 128
PEER_TOPK = 16
NOT_SELECTED = 99.0

VMEM_LIMIT = 56 * 1024 * 1024


def _cparams(sem):
    return pltpu.CompilerParams(dimension_semantics=sem, vmem_limit_bytes=VMEM_LIMIT)


def _nt_dot(a, b):
    return lax.dot_general(a, b, (((1,), (1,)), ((), ())), preferred_element_type=F32)


def _tn_dot(a, b):
    return lax.dot_general(a, b, (((0,), (0,)), ((), ())), preferred_element_type=F32)


def _sigmoid(x):
    return 1.0 / (1.0 + jnp.exp(-x))


def _norm_matmul_kernel(x_ref, g_ref, w_ref, o_ref, xn_ref, *, rows):
    @pl.when(pl.program_id(1) == 0)
    def _():
        g = g_ref[...]

        def chunk(ci, _):
            r0 = pl.multiple_of(ci * rows, rows)
            x = x_ref[pl.ds(r0, rows), :]
            ms = jnp.mean(x * x, axis=-1, keepdims=True)
            xn_ref[pl.ds(r0, rows), :] = (x * lax.rsqrt(ms + EPS) * g).astype(BF16)
            return 0

        lax.fori_loop(0, x_ref.shape[0] // rows, chunk, 0)

    o_ref[...] = jnp.dot(xn_ref[...], w_ref[...], preferred_element_type=F32).astype(o_ref.dtype)


def norm_matmul(x, g, w, *, tm=512, tn=1024, emit_xn=False):
    m, k = x.shape
    n = w.shape[1]
    tm, tn = min(tm, m), min(tn, n)
    assert m % tm == 0 and n % tn == 0
    rows = min(128, tm)
    out_shape = [jax.ShapeDtypeStruct((m, n), BF16)]
    out_specs = [pl.BlockSpec((tm, tn), lambda i, j: (i, j))]
    scratch = [pltpu.VMEM((tm, k), BF16)]
    if emit_xn:
        out_shape.append(jax.ShapeDtypeStruct((m, k), BF16))
        out_specs.append(pl.BlockSpec((tm, k), lambda i, j: (i, 0)))
        scratch = []
    res = pl.pallas_call(
        functools.partial(_norm_matmul_kernel, rows=rows),
        grid=(m // tm, n // tn),
        in_specs=[pl.BlockSpec((tm, k), lambda i, j: (i, 0)),
                  pl.BlockSpec((1, k), lambda i, j: (0, 0)),
                  pl.BlockSpec((k, tn), lambda i, j: (0, j))],
        out_specs=out_specs,
        out_shape=out_shape,
        scratch_shapes=scratch,
        compiler_params=_cparams(("parallel", "arbitrary")),
        name="norm_matmul",
    )(x, g.reshape(1, k).astype(F32), w)
    return res if emit_xn else res[0]


def _matmul_residual_kernel(*refs, n_lhs):
    lhs = refs[:n_lhs]
    ws = refs[n_lhs:2 * n_lhs]
    res_ref, o_ref = refs[2 * n_lhs], refs[2 * n_lhs + 1]
    acc = res_ref[...]
    for a_ref, w_ref in zip(lhs, ws):
        acc = acc + jnp.dot(a_ref[...], w_ref[...], preferred_element_type=F32)
    o_ref[...] = acc


def matmul_residual(lhs_list, w_list, res, *, tm=512, tn=1024):
    m, n = res.shape
    tm, tn = min(tm, m), min(tn, n)
    assert m % tm == 0 and n % tn == 0
    n_lhs = len(lhs_list)
    in_specs = [pl.BlockSpec((tm, a.shape[1]), lambda i, j: (i, 0)) for a in lhs_list]
    in_specs += [pl.BlockSpec((w.shape[0], tn), lambda i, j: (0, j)) for w in w_list]
    in_specs += [pl.BlockSpec((tm, tn), lambda i, j: (i, j))]
    return pl.pallas_call(
        functools.partial(_matmul_residual_kernel, n_lhs=n_lhs),
        grid=(m // tm, n // tn),
        in_specs=in_specs,
        out_specs=pl.BlockSpec((tm, tn), lambda i, j: (i, j)),
        out_shape=jax.ShapeDtypeStruct((m, n), F32),
        compiler_params=_cparams(("parallel", "parallel")),
        name="matmul_residual",
    )(*lhs_list, *w_list, res)


def _diff_attn_kernel(lam_ref, subg_ref, slope_ref, q_ref, k_ref, v_ref, o_ref, *, tq, lam_init):
    qi = pl.program_id(2)
    scale = DA_HALF ** -0.5
    q = q_ref[...] * jnp.asarray(scale, BF16)
    lane = lax.broadcasted_iota(jnp.int32, q.shape, 1)
    zero = jnp.zeros_like(q)
    qc = (jnp.where(lane < DA_HALF, q, zero), jnp.where(lane >= DA_HALF, q, zero))
    slope = slope_ref[0:1, 0:1]
    kiota = lax.broadcasted_iota(jnp.int32, (1, tq), 1)

    def step(j, carry, masked):
        r0 = pl.multiple_of(j * tq, tq)
        kj = k_ref[pl.ds(r0, tq), :]
        vj = v_ref[pl.ds(r0, tq), :]
        rel = (kiota + (j - qi) * tq).astype(F32)
        bias = slope * rel
        if masked:
            row = lax.broadcasted_iota(jnp.int32, (tq, tq), 0)
            col = lax.broadcasted_iota(jnp.int32, (tq, tq), 1)
            causal = col <= row
        new = []
        for c in range(2):
            m, l, acc = carry[c]
            s = _nt_dot(qc[c], kj) + bias
            if masked:
                s = jnp.where(causal, s, NEG_BIG)
            m_new = jnp.maximum(m, jnp.max(s, axis=-1, keepdims=True))
            p = jnp.exp(s - m_new)
            alpha = jnp.exp(m - m_new)
            l = alpha * l + jnp.sum(p, axis=-1, keepdims=True)
            acc = alpha * acc + jnp.dot(p.astype(BF16), vj, preferred_element_type=F32)
            new.append((m_new, l, acc))
        return tuple(new)

    init_c = (jnp.full((tq, 1), NEG_BIG, F32), jnp.zeros((tq, 1), F32), jnp.zeros((tq, HEAD_W), F32))
    carry = lax.fori_loop(0, qi, functools.partial(step, masked=False), (init_c, init_c))
    (_, l1, a1), (_, l2, a2) = step(qi, carry, True)

    lv = lam_ref[...]
    lam = (jnp.exp(jnp.sum(lv[0:1] * lv[1:2], axis=-1, keepdims=True))
           - jnp.exp(jnp.sum(lv[2:3] * lv[3:4], axis=-1, keepdims=True)) + lam_init)
    o = a1 / l1 - lam * (a2 / l2)
    ms = jnp.mean(o * o, axis=-1, keepdims=True)
    o = o * lax.rsqrt(ms + EPS) * subg_ref[...] * (1.0 - lam_init)
    o_ref[...] = o.astype(o_ref.dtype)


def diff_attention(proj, lam4, subln_g, batch, seq, *, tq=512, layer_idx=0):
    n = proj.shape[0]
    tq = min(tq, seq)
    nq = seq // tq
    lam_init = 0.8 - 0.6 * math.exp(-0.3 * layer_idx)
    slopes = np.array([2.0 ** (-8.0 * (h + 1) / DA_HEADS) for h in range(DA_HEADS)], dtype=np.float32)
    slope_tab = jnp.asarray(np.broadcast_to(slopes[:, None, None], (DA_HEADS, 8, 128)).copy())
    return pl.pallas_call(
        functools.partial(_diff_attn_kernel, tq=tq, lam_init=lam_init),
        grid=(batch, DA_HEADS, nq),
        in_specs=[pl.BlockSpec((4, DA_HALF), lambda b, h, i: (0, 0)),
                  pl.BlockSpec((1, HEAD_W), lambda b, h, i: (0, 0)),
                  pl.BlockSpec((None, 8, 128), lambda b, h, i: (h, 0, 0)),
                  pl.BlockSpec((tq, HEAD_W), lambda b, h, i: (b * nq + i, h)),
                  pl.BlockSpec((seq, HEAD_W), lambda b, h, i: (b, DA_HEADS + h)),
                  pl.BlockSpec((seq, HEAD_W), lambda b, h, i: (b, 2 * DA_HEADS + h))],
        out_specs=pl.BlockSpec((tq, HEAD_W), lambda b, h, i: (b * nq + i, h)),
        out_shape=jax.ShapeDtypeStruct((n, DA_HEADS * HEAD_W), BF16),
        compiler_params=_cparams(("parallel", "parallel", "arbitrary")),
        name="diff_attention",
    )(lam4, subln_g.reshape(1, HEAD_W).astype(F32), slope_tab, proj, proj, proj)


def _hgrn_chunk(q, k, b, v):
    c = HG_CHUNK
    row = lax.broadcasted_iota(jnp.int32, (c, 1), 0)
    arow = lax.broadcasted_iota(jnp.int32, (c, c), 0)
    acol = lax.broadcasted_iota(jnp.int32, (c, c), 1)

    def block_level(half):
        groups = c // (2 * half)
        ref = b[half - 1:half]
        for gidx in range(1, groups):
            lo = gidx * 2 * half
            ref = jnp.where(row >= lo, b[lo + half - 1:lo + half], ref)
        upper = (row % (2 * half)) >= half
        qs = jnp.where(upper, q * jnp.exp(jnp.minimum(b - ref, 0.0)), 0.0)
        ks = jnp.where(upper, 0.0, k * jnp.exp(jnp.minimum(ref - b, 0.0)))
        a = _nt_dot(qs.astype(BF16), ks.astype(BF16))
        keep = ((arow // (2 * half)) == (acol // (2 * half))) & ((arow % (2 * half)) >= half) & (
            (acol % (2 * half)) < half)
        return jnp.where(keep, a, 0.0)

    a_tot = None
    half = c // 2
    while half >= HG_SUB:
        lvl = block_level(half)
        a_tot = lvl if a_tot is None else a_tot + lvl
        half //= 2

    nsub = c // HG_SUB
    q3 = q.reshape(nsub, HG_SUB, HEAD_W)
    k3 = k.reshape(nsub, HG_SUB, HEAD_W)
    b3 = b.reshape(nsub, HG_SUB, HEAD_W)
    tgt_base = (arow // HG_SUB) * HG_SUB
    a_diag = jnp.zeros((c, c), F32)
    for s in range(HG_SUB):
        e = jnp.exp(jnp.minimum(b3 - b3[:, s:s + 1, :], 0.0))
        col = jnp.sum(q3 * (k3[:, s:s + 1, :] * e), axis=-1, keepdims=True)
        col = col.reshape(c, 1)
        a_diag = jnp.where(acol == tgt_base + s, col, a_diag)
    same_blk = (arow // HG_SUB) == (acol // HG_SUB)
    a_tot = a_tot + jnp.where(same_blk & (acol <= arow), a_diag, 0.0)
    return jnp.dot(a_tot.astype(BF16), v, preferred_element_type=F32)


def _hgrn_kernel(gam_ref, ng_ref, q_ref, f_ref, i_ref, g_ref, o_ref, st_ref, *, n_chunks, layer_idx):
    @pl.when(pl.program_id(2) == 0)
    def _():
        st_ref[...] = jnp.zeros_like(st_ref)

    gam = gam_ref[...]
    ge = jnp.exp(gam - jnp.max(gam, axis=0, keepdims=True))
    lb = jnp.sum(ge[0:layer_idx + 1], axis=0, keepdims=True) / jnp.sum(ge, axis=0, keepdims=True)
    ng = ng_ref[...]
    c = HG_CHUNK
    tri = (lax.broadcasted_iota(jnp.int32, (c, c), 0) >= lax.broadcasted_iota(jnp.int32, (c, c), 1))
    tri = jnp.where(tri, 1.0, 0.0).astype(BF16)

    def chunk(ci, _):
        r0 = pl.multiple_of(ci * c, c)
        qr = q_ref[pl.ds(r0, c), :].astype(F32)
        fr = f_ref[pl.ds(r0, c), :].astype(F32)
        v = i_ref[pl.ds(r0, c), :]
        gr = g_ref[pl.ds(r0, c), :].astype(F32)
        q = qr * _sigmoid(qr)
        f = lb + (1.0 - lb) * _sigmoid(fr)
        logf = jnp.log(f)
        k = 1.0 - f
        hi = logf.astype(BF16)
        r1 = logf - hi.astype(F32)
        mid = r1.astype(BF16)
        lo = (r1 - mid.astype(F32)).astype(BF16)
        b = (jnp.dot(tri, hi, preferred_element_type=F32) + jnp.dot(tri, mid, preferred_element_type=F32)
             + jnp.dot(tri, lo, preferred_element_type=F32))
        st = st_ref[...]
        o = _nt_dot((q * jnp.exp(b)).astype(BF16), st.astype(BF16))
        o = o + _hgrn_chunk(q, k, b, v)
        bend = b[c - 1:c]
        kd = (k * jnp.exp(bend - b)).astype(BF16)
        st_ref[...] = st * jnp.exp(bend) + _tn_dot(v, kd)
        ms = jnp.mean(o * o, axis=-1, keepdims=True)
        on = o * lax.rsqrt(ms + EPS) * ng * (gr * _sigmoid(gr))
        o_ref[pl.ds(r0, c), :] = on.astype(o_ref.dtype)
        return 0

    lax.fori_loop(0, n_chunks, chunk, 0)


def hgrn2(proj, gamma, norm_g, batch, seq, *, tc=512, layer_idx=0, col0=3 * DA_HEADS):
    n = proj.shape[0]
    tc = min(tc, seq)
    ng = seq // tc
    depth1 = gamma.shape[0]

    def spec(off):
        return pl.BlockSpec((tc, HEAD_W), lambda b, h, g: (b * ng + g, col0 + off * HG_HEADS + h))

    return pl.pallas_call(
        functools.partial(_hgrn_kernel, n_chunks=tc // HG_CHUNK, layer_idx=layer_idx),
        grid=(batch, HG_HEADS, ng),
        in_specs=[pl.BlockSpec((depth1, HEAD_W), lambda b, h, g: (0, h)),
                  pl.BlockSpec((1, HEAD_W), lambda b, h, g: (0, 0)),
                  spec(0), spec(1), spec(2), spec(3)],
        out_specs=pl.BlockSpec((tc, HEAD_W), lambda b, h, g: (b * ng + g, h)),
        out_shape=jax.ShapeDtypeStruct((n, HG_HEADS * HEAD_W), BF16),
        scratch_shapes=[pltpu.VMEM((HEAD_W, HEAD_W), F32)],
        compiler_params=_cparams(("parallel", "parallel", "arbitrary")),
        name="hgrn2",
    )(gamma.astype(F32), norm_g.reshape(1, HEAD_W).astype(F32), proj, proj, proj, proj)


def _cross_attn_kernel(q_ref, k_ref, v_ref, o_ref, *, scale):
    q = q_ref[...] * jnp.asarray(scale, BF16)
    s = _nt_dot(q, k_ref[...])
    m = jnp.max(s, axis=-1, keepdims=True)
    p = jnp.exp(s - m)
    l = jnp.sum(p, axis=-1, keepdims=True)
    o = jnp.dot(p.astype(BF16), v_ref[...], preferred_element_type=F32) / l
    o_ref[...] = o.astype(o_ref.dtype)


def cross_attention(q, kv, batch, seq, mem_len, *, tq=512):
    n, d = q.shape
    hd = d // XA_HEADS
    tq = min(tq, seq)
    nq = seq // tq
    return pl.pallas_call(
        functools.partial(_cross_attn_kernel, scale=hd ** -0.5),
        grid=(batch, nq, XA_HEADS),
        in_specs=[pl.BlockSpec((tq, hd), lambda b, i, h: (b * nq + i, h)),
                  pl.BlockSpec((mem_len, hd), lambda b, i, h: (b, h)),
                  pl.BlockSpec((mem_len, hd), lambda b, i, h: (b, XA_HEADS + h))],
        out_specs=pl.BlockSpec((tq, hd), lambda b, i, h: (b * nq + i, h)),
        out_shape=jax.ShapeDtypeStruct((n, d), BF16),
        compiler_params=_cparams(("parallel", "parallel", "parallel")),
        name="cross_attention",
    )(q, kv, kv)


def _topk_desc(s, k):
    rows = s.shape[0]
    iota = lax.broadcasted_iota(jnp.int32, s.shape, 0)
    rank = jnp.full(s.shape, NOT_SELECTED, F32)
    vals = []
    cur = s
    for r in range(k):
        m = jnp.max(cur, axis=0, keepdims=True)
        first = jnp.min(jnp.where(cur == m, iota, rows), axis=0, keepdims=True)
        sel = iota == first
        rank = jnp.where(sel, float(r), rank)
        cur = jnp.where(sel, -jnp.inf, cur)
        vals.append(m)
    return jnp.concatenate(vals, axis=0), rank


def _peer_topk_kernel(q_ref, keys_ref, l1_ref, c_ref, r2_ref, e2_ref):
    kk = PEER_TOPK
    for h in range(PEER_HEADS):
        q1 = q_ref[:, (2 * h) * PEER_NKEYS:(2 * h + 1) * PEER_NKEYS]
        q2 = q_ref[:, (2 * h + 1) * PEER_NKEYS:(2 * h + 2) * PEER_NKEYS]
        s1 = _nt_dot(keys_ref[h, 0], q1)
        s2 = _nt_dot(keys_ref[h, 1], q2)
        sv1, rank1 = _topk_desc(s1, kk)
        sv2, rank2 = _topk_desc(s2, kk)
        cand = jnp.concatenate([sv1[a:a + 1] + sv2 for a in range(kk)], axis=0)
        t = cand.shape[1]
        iota = lax.broadcasted_iota(jnp.int32, cand.shape, 0)
        aidx = lax.broadcasted_iota(jnp.int32, (kk, t), 0)
        count = jnp.zeros((kk, t), F32)
        z = jnp.zeros((1, t), F32)
        cur = cand
        best0 = sv1[0:1] + sv2[0:1]
        for _ in range(kk):
            m = jnp.max(cur, axis=0, keepdims=True)
            first = jnp.min(jnp.where(cur == m, iota, kk * kk), axis=0, keepdims=True)
            cur = jnp.where(iota == first, -jnp.inf, cur)
            count = count + jnp.where(aidx == first // kk, 1.0, 0.0)
            z = z + jnp.exp(m - best0)
        l1 = jnp.zeros_like(rank1)
        for a in range(kk):
            l1 = jnp.where(rank1 == float(a), count[a:a + 1], l1)
        sel1 = rank1 < float(kk)
        l1_ref[h] = l1
        c_ref[h] = jnp.where(sel1, jnp.exp(jnp.minimum(s1 - sv1[0:1], 0.0)) / z, 0.0)
        r2_ref[h] = rank2
        e2_ref[h] = jnp.where(rank2 < float(kk), jnp.exp(jnp.minimum(s2 - sv2[0:1], 0.0)), 0.0)


def peer_topk(q, keys, *, tt=256):
    n = q.shape[0]
    tt = min(tt, n)
    tab = jax.ShapeDtypeStruct((PEER_HEADS, PEER_NKEYS, n), F32)
    tab_spec = pl.BlockSpec((PEER_HEADS, PEER_NKEYS, tt), lambda i: (0, 0, i))
    return pl.pallas_call(
        _peer_topk_kernel,
        grid=(n // tt,),
        in_specs=[pl.BlockSpec((tt, q.shape[1]), lambda i: (i, 0)),
                  pl.BlockSpec(keys.shape, lambda i: (0, 0, 0, 0))],
        out_specs=[tab_spec] * 4,
        out_shape=[tab] * 4,
        compiler_params=_cparams(("parallel",)),
        name="peer_topk",
    )(q, keys)


def _gelu_exact(x):
    return 0.5 * x * (1.0 + lax.erf(x * (2.0 ** -0.5)))


def _peer_dense_kernel(hn_ref, down_ref, upt_ref, l1_ref, c_ref, r2_ref, e2_ref, o_ref, *, groups):
    e = pl.program_id(1)

    @pl.when(e == 0)
    def _():
        o_ref[...] = jnp.zeros_like(o_ref)

    act = _gelu_exact(_nt_dot(down_ref[...], hn_ref[...]))
    parts = []
    for gi in range(groups):
        i1 = e * groups + gi
        w = None
        for h in range(PEER_HEADS):
            l1 = l1_ref[h, pl.ds(i1, 1), :]
            cc = c_ref[h, pl.ds(i1, 1), :]
            term = jnp.where(r2_ref[h] < l1, e2_ref[h] * cc, 0.0)
            w = term if w is None else w + term
        parts.append(w)
    gate = jnp.concatenate(parts, axis=0) if groups > 1 else parts[0]
    wg = (gate * act).astype(BF16)
    o_ref[...] += jnp.dot(upt_ref[...], wg, preferred_element_type=F32)


def peer_dense(hn, down, upt, tables, *, tt=256, te=512):
    n, d = hn.shape
    n_exp = down.shape[0]
    tt = min(tt, n)
    groups = te // PEER_NKEYS
    tab_spec = pl.BlockSpec((PEER_HEADS, PEER_NKEYS, tt), lambda i, e: (0, 0, i))
    return pl.pallas_call(
        functools.partial(_peer_dense_kernel, groups=groups),
        grid=(n // tt, n_exp // te),
        in_specs=[pl.BlockSpec((tt, d), lambda i, e: (i, 0)),
                  pl.BlockSpec((te, d), lambda i, e: (e, 0)),
                  pl.BlockSpec((d, te), lambda i, e: (0, e))] + [tab_spec] * 4,
        out_specs=pl.BlockSpec((d, tt), lambda i, e: (0, i)),
        out_shape=jax.ShapeDtypeStruct((d, n), F32),
        compiler_params=_cparams(("parallel", "arbitrary")),
        name="peer_dense",
    )(hn, down, upt, *tables)


def _final_norm_kernel(h_ref, p_ref, g_ref, o_ref):
    x = h_ref[...] + p_ref[...]
    ms = jnp.mean(x * x, axis=-1, keepdims=True)
    o_ref[...] = x * lax.rsqrt(ms + EPS) * g_ref[...]


def final_norm(h, p, g, *, tm=256):
    m, d = h.shape
    tm = min(tm, m)
    row = pl.BlockSpec((tm, d), lambda i: (i, 0))
    return pl.pallas_call(
        _final_norm_kernel,
        grid=(m // tm,),
        in_specs=[row, row, pl.BlockSpec((1, d), lambda i: (0, 0))],
        out_specs=row,
        out_shape=jax.ShapeDtypeStruct((m, d), F32),
        compiler_params=_cparams(("parallel",)),
        name="final_norm",
    )(h, p, g.reshape(1, d).astype(F32))


def kernel(x, mem, mix_norm_g, w_in, da_lambda_q1, da_lambda_k1, da_lambda_q2, da_lambda_k2, da_subln_g, hgrn_gamma, hgrn_norm_g, w_out, cross_norm_g, mem_norm_g, w_cq, w_ckv, w_co, ffn_norm_g, peer_wq, peer_subkeys, peer_down, peer_up, final_norm_g):
    batch, seq, d = x.shape
    mem_len = mem.shape[1]
    depth = w_in.shape[0]
    assert depth == 1 and hgrn_gamma.shape[0] == depth + 1
    l = 0
    da_w = DA_HEADS * HEAD_W
    x2 = x.reshape(batch * seq, d)

    proj = norm_matmul(x2, mix_norm_g[l], w_in[l].astype(BF16))
    lam4 = jnp.stack([da_lambda_q1[l], da_lambda_k1[l], da_lambda_q2[l], da_lambda_k2[l]]).astype(F32)
    a_out = diff_attention(proj, lam4, da_subln_g[l], batch, seq, layer_idx=l)
    b_out = hgrn2(proj, hgrn_gamma, hgrn_norm_g[l], batch, seq, layer_idx=l)
    wo = w_out[l].astype(BF16)
    h1 = matmul_residual([a_out, b_out], [wo[:da_w], wo[da_w:]], x2)

    qc = norm_matmul(h1, cross_norm_g[l], w_cq[l].astype(BF16))
    kv = norm_matmul(mem.reshape(batch * mem_len, d), mem_norm_g[l], w_ckv[l].astype(BF16))
    oc = cross_attention(qc, kv, batch, seq, mem_len)
    h2 = matmul_residual([oc], [w_co[l].astype(BF16)], h1)

    qp, hn = norm_matmul(h2, ffn_norm_g[l], peer_wq[l].astype(BF16), tn=512, emit_xn=True)
    tables = peer_topk(qp, peer_subkeys[l].astype(BF16))
    peer_t = peer_dense(hn, peer_down[l].astype(BF16), peer_up[l].T.astype(BF16), tables)
    out = final_norm(h2, peer_t.T, final_norm_g)
    return out.reshape(batch, seq, d)
```

```python
import functools
import math

import numpy as np
import jax
import jax.numpy as jnp
from jax import lax
from jax.experimental import pallas as pl
from jax.experimental.pallas import tpu as pltpu

F32 = jnp.float32
BF16 = jnp.bfloat16
EPS = 1e-6
NEG_BIG = -1e30

DA_HEADS = 16
DA_HALF = 64
HEAD_W = 128
HG_HEADS = 16
HG_CHUNK = 64
HG_SUB = 16
XA_HEADS = 4
PEER_HEADS = 8
PEER_NKEYS = 128
PEER_TOPK = 16
NOT_SELECTED = 99.0

VMEM_LIMIT = 56 * 1024 * 1024


def _cparams(sem):
    return pltpu.CompilerParams(dimension_semantics=sem, vmem_limit_bytes=VMEM_LIMIT)


def _nt_dot(a, b):
    return lax.dot_general(a, b, (((1,), (1,)), ((), ())), preferred_element_type=F32)


def _tn_dot(a, b):
    return lax.dot_general(a, b, (((0,), (0,)), ((), ())), preferred_element_type=F32)


def _sigmoid(x):
    return 1.0 / (1.0 + jnp.exp(-x))


def _norm_matmul_kernel(x_ref, g_ref, w_ref, o_ref, xn_ref, *, rows):
    @pl.when(pl.program_id(1) == 0)
    def _():
        g = g_ref[...]

        def chunk(ci, _):
            r0 = pl.multiple_of(ci * rows, rows)
            x = x_ref[pl.ds(r0, rows), :]
            ms = jnp.mean(x * x, axis=-1, keepdims=True)
            xn_ref[pl.ds(r0, rows), :] = (x * lax.rsqrt(ms + EPS) * g).astype(BF16)
            return 0

        lax.fori_loop(0, x_ref.shape[0] // rows, chunk, 0)

    o_ref[...] = jnp.dot(xn_ref[...], w_ref[...], preferred_element_type=F32).astype(o_ref.dtype)


def norm_matmul(x, g, w, *, tm=512, tn=1024, emit_xn=False):
    m, k = x.shape
    n = w.shape[1]
    tm, tn = min(tm, m), min(tn, n)
    assert m % tm == 0 and n % tn == 0
    rows = min(128, tm)
    out_shape = [jax.ShapeDtypeStruct((m, n), BF16)]
    out_specs = [pl.BlockSpec((tm, tn), lambda i, j: (i, j))]
    scratch = [pltpu.VMEM((tm, k), BF16)]
    if emit_xn:
        out_shape.append(jax.ShapeDtypeStruct((m, k), BF16))
        out_specs.append(pl.BlockSpec((tm, k), lambda i, j: (i, 0)))
        scratch = []
    res = pl.pallas_call(
        functools.partial(_norm_matmul_kernel, rows=rows),
        grid=(m // tm, n // tn),
        in_specs=[pl.BlockSpec((tm, k), lambda i, j: (i, 0)),
                  pl.BlockSpec((1, k), lambda i, j: (0, 0)),
                  pl.BlockSpec((k, tn), lambda i, j: (0, j))],
        out_specs=out_specs,
        out_shape=out_shape,
        scratch_shapes=scratch,
        compiler_params=_cparams(("parallel", "arbitrary")),
        name="norm_matmul",
    )(x, g.reshape(1, k).astype(F32), w)
    return res if emit_xn else res[0]


def _matmul_residual_kernel(*refs, n_lhs):
    lhs = refs[:n_lhs]
    ws = refs[n_lhs:2 * n_lhs]
    res_ref, o_ref = refs[2 * n_lhs], refs[2 * n_lhs + 1]
    acc = res_ref[...]
    for a_ref, w_ref in zip(lhs, ws):
        acc = acc + jnp.dot(a_ref[...], w_ref[...], preferred_element_type=F32)
    o_ref[...] = acc


def matmul_residual(lhs_list, w_list, res, *, tm=512, tn=1024):
    m, n = res.shape
    tm, tn = min(tm, m), min(tn, n)
    assert m % tm == 0 and n % tn == 0
    n_lhs = len(lhs_list)
    in_specs = [pl.BlockSpec((tm, a.shape[1]), lambda i, j: (i, 0)) for a in lhs_list]
    in_specs += [pl.BlockSpec((w.shape[0], tn), lambda i, j: (0, j)) for w in w_list]
    in_specs += [pl.BlockSpec((tm, tn), lambda i, j: (i, j))]
    return pl.pallas_call(
        functools.partial(_matmul_residual_kernel, n_lhs=n_lhs),
        grid=(m // tm, n // tn),
        in_specs=in_specs,
        out_specs=pl.BlockSpec((tm, tn), lambda i, j: (i, j)),
        out_shape=jax.ShapeDtypeStruct((m, n), F32),
        compiler_params=_cparams(("parallel", "parallel")),
        name="matmul_residual",
    )(*lhs_list, *w_list, res)


def _diff_attn_kernel(lam_ref, subg_ref, slope_ref, q_ref, k_ref, v_ref, o_ref, *, tq, lam_init):
    qi = pl.program_id(2)
    scale = DA_HALF ** -0.5
    q = q_ref[...] * jnp.asarray(scale, BF16)
    lane = lax.broadcasted_iota(jnp.int32, q.shape, 1)
    zero = jnp.zeros_like(q)
    qc = (jnp.where(lane < DA_HALF, q, zero), jnp.where(lane >= DA_HALF, q, zero))
    slope = slope_ref[0:1, 0:1]
    kiota = lax.broadcasted_iota(jnp.int32, (1, tq), 1)

    def step(j, carry, masked):
        r0 = pl.multiple_of(j * tq, tq)
        kj = k_ref[pl.ds(r0, tq), :]
        vj = v_ref[pl.ds(r0, tq), :]
        rel = (kiota + (j - qi) * tq).astype(F32)
        bias = slope * rel
        if masked:
            row = lax.broadcasted_iota(jnp.int32, (tq, tq), 0)
            col = lax.broadcasted_iota(jnp.int32, (tq, tq), 1)
            causal = col <= row
        new = []
        for c in range(2):
            m, l, acc = carry[c]
            s = _nt_dot(qc[c], kj) + bias
            if masked:
                s = jnp.where(causal, s, NEG_BIG)
            m_new = jnp.maximum(m, jnp.max(s, axis=-1, keepdims=True))
            p = jnp.exp(s - m_new)
            alpha = jnp.exp(m - m_new)
            l = alpha * l + jnp.sum(p, axis=-1, keepdims=True)
            acc = alpha * acc + jnp.dot(p.astype(BF16), vj, preferred_element_type=F32)
            new.append((m_new, l, acc))
        return tuple(new)

    init_c = (jnp.full((tq, 1), NEG_BIG, F32), jnp.zeros((tq, 1), F32), jnp.zeros((tq, HEAD_W), F32))
    carry = lax.fori_loop(0, qi, functools.partial(step, masked=False), (init_c, init_c))
    (_, l1, a1), (_, l2, a2) = step(qi, carry, True)

    lv = lam_ref[...]
    lam = (jnp.exp(jnp.sum(lv[0:1] * lv[1:2], axis=-1, keepdims=True))
           - jnp.exp(jnp.sum(lv[2:3] * lv[3:4], axis=-1, keepdims=True)) + lam_init)
    o = a1 / l1 - lam * (a2 / l2)
    ms = jnp.mean(o * o, axis=-1, keepdims=True)
    o = o * lax.rsqrt(ms + EPS) * subg_ref[...] * (1.0 - lam_init)
    o_ref[...] = o.astype(o_ref.dtype)


def diff_attention(proj, lam4, subln_g, batch, seq, *, tq=512, layer_idx=0):
    n = proj.shape[0]
    tq = min(tq, seq)
    nq = seq // tq
    lam_init = 0.8 - 0.6 * math.exp(-0.3 * layer_idx)
    slopes = np.array([2.0 ** (-8.0 * (h + 1) / DA_HEADS) for h in range(DA_HEADS)], dtype=np.float32)
    slope_tab = jnp.asarray(np.broadcast_to(slopes[:, None, None], (DA_HEADS, 8, 128)).copy())
    return pl.pallas_call(
        functools.partial(_diff_attn_kernel, tq=tq, lam_init=lam_init),
        grid=(batch, DA_HEADS, nq),
        in_specs=[pl.BlockSpec((4, DA_HALF), lambda b, h, i: (0, 0)),
                  pl.BlockSpec((1, HEAD_W), lambda b, h, i: (0, 0)),
                  pl.BlockSpec((None, 8, 128), lambda b, h, i: (h, 0, 0)),
                  pl.BlockSpec((tq, HEAD_W), lambda b, h, i: (b * nq + i, h)),
                  pl.BlockSpec((seq, HEAD_W), lambda b, h, i: (b, DA_HEADS + h)),
                  pl.BlockSpec((seq, HEAD_W), lambda b, h, i: (b, 2 * DA_HEADS + h))],
        out_specs=pl.BlockSpec((tq, HEAD_W), lambda b, h, i: (b * nq + i, h)),
        out_shape=jax.ShapeDtypeStruct((n, DA_HEADS * HEAD_W), BF16),
        compiler_params=_cparams(("parallel", "parallel", "arbitrary")),
        name="diff_attention",
    )(lam4, subln_g.reshape(1, HEAD_W).astype(F32), slope_tab, proj, proj, proj)


def _hgrn_chunk(q, k, b, v):
    c = HG_CHUNK
    row = lax.broadcasted_iota(jnp.int32, (c, 1), 0)
    arow = lax.broadcasted_iota(jnp.int32, (c, c), 0)
    acol = lax.broadcasted_iota(jnp.int32, (c, c), 1)

    def block_level(half):
        groups = c // (2 * half)
        ref = b[half - 1:half]
        for gidx in range(1, groups):
            lo = gidx * 2 * half
            ref = jnp.where(row >= lo, b[lo + half - 1:lo + half], ref)
        upper = (row % (2 * half)) >= half
        qs = jnp.where(upper, q * jnp.exp(jnp.minimum(b - ref, 0.0)), 0.0)
        ks = jnp.where(upper, 0.0, k * jnp.exp(jnp.minimum(ref - b, 0.0)))
        a = _nt_dot(qs.astype(BF16), ks.astype(BF16))
        keep = ((arow // (2 * half)) == (acol // (2 * half))) & ((arow % (2 * half)) >= half) & (
            (acol % (2 * half)) < half)
        return jnp.where(keep, a, 0.0)

    a_tot = None
    half = c // 2
    while half >= HG_SUB:
        lvl = block_level(half)
        a_tot = lvl if a_tot is None else a_tot + lvl
        half //= 2

    nsub = c // HG_SUB
    q3 = q.reshape(nsub, HG_SUB, HEAD_W)
    k3 = k.reshape(nsub, HG_SUB, HEAD_W)
    b3 = b.reshape(nsub, HG_SUB, HEAD_W)
    tgt_base = (arow // HG_SUB) * HG_SUB
    a_diag = jnp.zeros((c, c), F32)
    for s in range(HG_SUB):
        e = jnp.exp(jnp.minimum(b3 - b3[:, s:s + 1, :], 0.0))
        col = jnp.sum(q3 * (k3[:, s:s + 1, :] * e), axis=-1, keepdims=True)
        col = col.reshape(c, 1)
        a_diag = jnp.where(acol == tgt_base + s, col, a_diag)
    same_blk = (arow // HG_SUB) == (acol // HG_SUB)
    a_tot = a_tot + jnp.where(same_blk & (acol <= arow), a_diag, 0.0)
    return jnp.dot(a_tot.astype(BF16), v, preferred_element_type=F32)


def _hgrn_kernel(gam_ref, ng_ref, q_ref, f_ref, i_ref, g_ref, o_ref, st_ref, *, n_chunks, layer_idx):
    @pl.when(pl.program_id(2) == 0)
    def _():
        st_ref[...] = jnp.zeros_like(st_ref)

    gam = gam_ref[...]
    ge = jnp.exp(gam - jnp.max(gam, axis=0, keepdims=True))
    lb = jnp.sum(ge[0:layer_idx + 1], axis=0, keepdims=True) / jnp.sum(ge, axis=0, keepdims=True)
    ng = ng_ref[...]
    c = HG_CHUNK
    tri = (lax.broadcasted_iota(jnp.int32, (c, c), 0) >= lax.broadcasted_iota(jnp.int32, (c, c), 1))
    tri = jnp.where(tri, 1.0, 0.0).astype(BF16)

    def chunk(ci, _):
        r0 = pl.multiple_of(ci * c, c)
        qr = q_ref[pl.ds(r0, c), :].astype(F32)
        fr = f_ref[pl.ds(r0, c), :].astype(F32)
        v = i_ref[pl.ds(r0, c), :]
        gr = g_ref[pl.ds(r0, c), :].astype(F32)
        q = qr * _sigmoid(qr)
        f = lb + (1.0 - lb) * _sigmoid(fr)
        logf = jnp.log(f)
        k = 1.0 - f
        hi = logf.astype(BF16)
        r1 = logf - hi.astype(F32)
        mid = r1.astype(BF16)
        lo = (r1 - mid.astype(F32)).astype(BF16)
        b = (jnp.dot(tri, hi, preferred_element_type=F32) + jnp.dot(tri, mid, preferred_element_type=F32)
             + jnp.dot(tri, lo, preferred_element_type=F32))
        st = st_ref[...]
        o = _nt_dot((q * jnp.exp(b)).astype(BF16), st.astype(BF16))
        o = o + _hgrn_chunk(q, k, b, v)
        bend = b[c - 1:c]
        kd = (k * jnp.exp(bend - b)).astype(BF16)
        st_ref[...] = st * jnp.exp(bend) + _tn_dot(v, kd)
        ms = jnp.mean(o * o, axis=-1, keepdims=True)
        on = o * lax.rsqrt(ms + EPS) * ng * (gr * _sigmoid(gr))
        o_ref[pl.ds(r0, c), :] = on.astype(o_ref.dtype)
        return 0

    lax.fori_loop(0, n_chunks, chunk, 0)


def hgrn2(proj, gamma, norm_g, batch, seq, *, tc=512, layer_idx=0, col0=3 * DA_HEADS):
    n = proj.shape[0]
    tc = min(tc, seq)
    ng = seq // tc
    depth1 = gamma.shape[0]

    def spec(off):
        return pl.BlockSpec((tc, HEAD_W), lambda b, h, g: (b * ng + g, col0 + off * HG_HEADS + h))

    return pl.pallas_call(
        functools.partial(_hgrn_kernel, n_chunks=tc // HG_CHUNK, layer_idx=layer_idx),
        grid=(batch, HG_HEADS, ng),
        in_specs=[pl.BlockSpec((depth1, HEAD_W), lambda b, h, g: (0, h)),
                  pl.BlockSpec((1, HEAD_W), lambda b, h, g: (0, 0)),
                  spec(0), spec(1), spec(2), spec(3)],
        out_specs=pl.BlockSpec((tc, HEAD_W), lambda b, h, g: (b * ng + g, h)),
        out_shape=jax.ShapeDtypeStruct((n, HG_HEADS * HEAD_W), BF16),
        scratch_shapes=[pltpu.VMEM((HEAD_W, HEAD_W), F32)],
        compiler_params=_cparams(("parallel", "parallel", "arbitrary")),
        name="hgrn2",
    )(gamma.astype(F32), norm_g.reshape(1, HEAD_W).astype(F32), proj, proj, proj, proj)


def _cross_attn_kernel(q_ref, k_ref, v_ref, o_ref, *, scale):
    q = q_ref[...] * jnp.asarray(scale, BF16)
    s = _nt_dot(q, k_ref[...])
    m = jnp.max(s, axis=-1, keepdims=True)
    p = jnp.exp(s - m)
    l = jnp.sum(p, axis=-1, keepdims=True)
    o = jnp.dot(p.astype(BF16), v_ref[...], preferred_element_type=F32) / l
    o_ref[...] = o.astype(o_ref.dtype)


def cross_attention(q, kv, batch, seq, mem_len, *, tq=512):
    n, d = q.shape
    hd = d // XA_HEADS
    tq = min(tq, seq)
    nq = seq // tq
    return pl.pallas_call(
        functools.partial(_cross_attn_kernel, scale=hd ** -0.5),
        grid=(batch, nq, XA_HEADS),
        in_specs=[pl.BlockSpec((tq, hd), lambda b, i, h: (b * nq + i, h)),
                  pl.BlockSpec((mem_len, hd), lambda b, i, h: (b, h)),
                  pl.BlockSpec((mem_len, hd), lambda b, i, h: (b, XA_HEADS + h))],
        out_specs=pl.BlockSpec((tq, hd), lambda b, i, h: (b * nq + i, h)),
        out_shape=jax.ShapeDtypeStruct((n, d), BF16),
        compiler_params=_cparams(("parallel", "parallel", "parallel")),
        name="cross_attention",
    )(q, kv, kv)


def _topk_desc(s, k):
    rows = s.shape[0]
    iota = lax.broadcasted_iota(jnp.int32, s.shape, 0)
    rank = jnp.full(s.shape, NOT_SELECTED, F32)
    vals = []
    cur = s
    for r in range(k):
        m = jnp.max(cur, axis=0, keepdims=True)
        first = jnp.min(jnp.where(cur == m, iota, rows), axis=0, keepdims=True)
        sel = iota == first
        rank = jnp.where(sel, float(r), rank)
        cur = jnp.where(sel, -jnp.inf, cur)
        vals.append(m)
    return jnp.concatenate(vals, axis=0), rank


def _peer_topk_kernel(q_ref, keys_ref, l1_ref, c_ref, r2_ref, e2_ref):
    kk = PEER_TOPK
    for h in range(PEER_HEADS):
        q1 = q_ref[:, (2 * h) * PEER_NKEYS:(2 * h + 1) * PEER_NKEYS]
        q2 = q_ref[:, (2 * h + 1) * PEER_NKEYS:(2 * h + 2) * PEER_NKEYS]
        s1 = _nt_dot(keys_ref[h, 0], q1)
        s2 = _nt_dot(keys_ref[h, 1], q2)
        sv1, rank1 = _topk_desc(s1, kk)
        sv2, rank2 = _topk_desc(s2, kk)
        cand = jnp.concatenate([sv1[a:a + 1] + sv2 for a in range(kk)], axis=0)
        t = cand.shape[1]
        iota = lax.broadcasted_iota(jnp.int32, cand.shape, 0)
        aidx = lax.broadcasted_iota(jnp.int32, (kk, t), 0)
        count = jnp.zeros((kk, t), F32)
        z = jnp.zeros((1, t), F32)
        cur = cand
        best0 = sv1[0:1] + sv2[0:1]
        for _ in range(kk):
            m = jnp.max(cur, axis=0, keepdims=True)
            first = jnp.min(jnp.where(cur == m, iota, kk * kk), axis=0, keepdims=True)
            cur = jnp.where(iota == first, -jnp.inf, cur)
            count = count + jnp.where(aidx == first // kk, 1.0, 0.0)
            z = z + jnp.exp(m - best0)
        l1 = jnp.zeros_like(rank1)
        for a in range(kk):
            l1 = jnp.where(rank1 == float(a), count[a:a + 1], l1)
        sel1 = rank1 < float(kk)
        l1_ref[h] = l1
        c_ref[h] = jnp.where(sel1, jnp.exp(jnp.minimum(s1 - sv1[0:1], 0.0)) / z, 0.0)
        r2_ref[h] = rank2.astype(r2_ref.dtype)
        e2 = jnp.where(rank2 < float(kk), jnp.exp(jnp.minimum(s2 - sv2[0:1], 0.0)), 0.0)
        e2_ref[h] = e2.astype(e2_ref.dtype)


def peer_topk(q, keys, *, tt=256):
    n = q.shape[0]
    tt = min(tt, n)
    tab_spec = pl.BlockSpec((PEER_HEADS, PEER_NKEYS, tt), lambda i: (0, 0, i))
    return pl.pallas_call(
        _peer_topk_kernel,
        grid=(n // tt,),
        in_specs=[pl.BlockSpec((tt, q.shape[1]), lambda i: (i, 0)),
                  pl.BlockSpec(keys.shape, lambda i: (0, 0, 0, 0))],
        out_specs=[tab_spec] * 4,
        out_shape=[jax.ShapeDtypeStruct((PEER_HEADS, PEER_NKEYS, n), dt) for dt in (F32, F32, BF16, BF16)],
        compiler_params=_cparams(("parallel",)),
        name="peer_topk",
    )(q, keys)


def _gelu_exact(x):
    return 0.5 * x * (1.0 + lax.erf(x * (2.0 ** -0.5)))


def _peer_dense_kernel(hn_ref, down_ref, upt_ref, l1_ref, c_ref, r2_ref, e2_ref, o_ref, act_ref, wg_ref, *,
                       groups):
    s = pl.program_id(1)

    @pl.when(s == 0)
    def _():
        o_ref[...] = jnp.zeros_like(o_ref)
        act_ref[1] = jnp.zeros(act_ref.shape[1:], act_ref.dtype)
        wg_ref[1] = jnp.zeros(wg_ref.shape[1:], wg_ref.dtype)

    d, tt = o_ref.shape
    lane_chunk = 256
    units = [(gi, t0) for gi in range(groups) for t0 in range(0, tt, lane_chunk)]
    n_piece = len(units)
    kc, mc = d // n_piece, d // n_piece

    def gate_unit(cur, prv, gi, t0):
        rows = slice(gi * PEER_NKEYS, (gi + 1) * PEER_NKEYS)
        lanes = slice(t0, t0 + lane_chunk)
        g = None
        tile = (PEER_NKEYS // 8, 8, lane_chunk)
        for h in range(PEER_HEADS):
            l1 = jnp.broadcast_to(l1_ref[h, gi:gi + 1, lanes], tile[1:])
            cc = jnp.broadcast_to(c_ref[h, gi:gi + 1, lanes], tile[1:])
            r2 = r2_ref[h, :, lanes].astype(F32).reshape(tile)
            e2 = e2_ref[h, :, lanes].astype(F32).reshape(tile)
            term = jnp.where(r2 < l1[None], e2 * cc[None], 0.0)
            g = term if g is None else g + term
        g = g.reshape(PEER_NKEYS, lane_chunk)
        wg_ref[cur, rows, lanes] = (g * _gelu_exact(act_ref[prv, rows, lanes])).astype(BF16)

    def step(cur, prv):
        wg_prev = wg_ref[prv]
        for p in range(n_piece):
            ms = slice(p * mc, (p + 1) * mc)
            o_ref[ms, :] += jnp.dot(upt_ref[ms, :], wg_prev, preferred_element_type=F32)
            ks = slice(p * kc, (p + 1) * kc)
            part = _nt_dot(down_ref[:, ks], hn_ref[:, ks])
            if p == 0:
                act_ref[cur] = part
            else:
                act_ref[cur] += part
            gate_unit(cur, prv, *units[p])

    step(s % 2, (s + 1) % 2)


def peer_dense(hn, down, upt, tables, *, tt=512, te=512):
    n, d = hn.shape
    n_exp = down.shape[0]
    tt = min(tt, n)
    groups = te // PEER_NKEYS
    nblk = n_exp // te
    l1, c, r2, e2 = tables
    l1 = l1.reshape(PEER_HEADS, PEER_NKEYS // groups, groups, n)
    c = c.reshape(PEER_HEADS, PEER_NKEYS // groups, groups, n)
    def blk(s, lag):
        return jnp.clip(s - lag, 0, nblk - 1)

    row_spec = pl.BlockSpec((PEER_HEADS, None, groups, tt), lambda i, s: (0, blk(s, 1), 0, i))
    tab_spec = pl.BlockSpec((PEER_HEADS, PEER_NKEYS, tt), lambda i, s: (0, 0, i))
    return pl.pallas_call(
        functools.partial(_peer_dense_kernel, groups=groups),
        grid=(n // tt, nblk + 2),
        in_specs=[pl.BlockSpec((tt, d), lambda i, s: (i, 0)),
                  pl.BlockSpec((te, d), lambda i, s: (blk(s, 0), 0)),
                  pl.BlockSpec((d, te), lambda i, s: (0, blk(s, 2))),
                  row_spec, row_spec, tab_spec, tab_spec],
        out_specs=pl.BlockSpec((d, tt), lambda i, s: (0, i)),
        out_shape=jax.ShapeDtypeStruct((d, n), F32),
        scratch_shapes=[pltpu.VMEM((2, te, tt), F32), pltpu.VMEM((2, te, tt), BF16)],
        compiler_params=_cparams(("parallel", "arbitrary")),
        name="peer_dense",
    )(hn, down, upt, l1, c, r2, e2)


def _final_norm_kernel(h_ref, p_ref, g_ref, o_ref):
    x = h_ref[...] + p_ref[...]
    ms = jnp.mean(x * x, axis=-1, keepdims=True)
    o_ref[...] = x * lax.rsqrt(ms + EPS) * g_ref[...]


def final_norm(h, p, g, *, tm=256):
    m, d = h.shape
    tm = min(tm, m)
    row = pl.BlockSpec((tm, d), lambda i: (i, 0))
    return pl.pallas_call(
        _final_norm_kernel,
        grid=(m // tm,),
        in_specs=[row, row, pl.BlockSpec((1, d), lambda i: (0, 0))],
        out_specs=row,
        out_shape=jax.ShapeDtypeStruct((m, d), F32),
        compiler_params=_cparams(("parallel",)),
        name="final_norm",
    )(h, p, g.reshape(1, d).astype(F32))


def kernel(x, mem, mix_norm_g, w_in, da_lambda_q1, da_lambda_k1, da_lambda_q2, da_lambda_k2, da_subln_g, hgrn_gamma, hgrn_norm_g, w_out, cross_norm_g, mem_norm_g, w_cq, w_ckv, w_co, ffn_norm_g, peer_wq, peer_subkeys, peer_down, peer_up, final_norm_g):
    batch, seq, d = x.shape
    mem_len = mem.shape[1]
    depth = w_in.shape[0]
    assert depth == 1 and hgrn_gamma.shape[0] == depth + 1
    l = 0
    da_w = DA_HEADS * HEAD_W
    x2 = x.reshape(batch * seq, d)

    proj = norm_matmul(x2, mix_norm_g[l], w_in[l].astype(BF16))
    lam4 = jnp.stack([da_lambda_q1[l], da_lambda_k1[l], da_lambda_q2[l], da_lambda_k2[l]]).astype(F32)
    a_out = diff_attention(proj, lam4, da_subln_g[l], batch, seq, layer_idx=l)
    b_out = hgrn2(proj, hgrn_gamma, hgrn_norm_g[l], batch, seq, layer_idx=l)
    wo = w_out[l].astype(BF16)
    h1 = matmul_residual([a_out, b_out], [wo[:da_w], wo[da_w:]], x2)

    qc = norm_matmul(h1, cross_norm_g[l], w_cq[l].astype(BF16))
    kv = norm_matmul(mem.reshape(batch * mem_len, d), mem_norm_g[l], w_ckv[l].astype(BF16))
    oc = cross_attention(qc, kv, batch, seq, mem_len)
    h2 = matmul_residual([oc], [w_co[l].astype(BF16)], h1)

    qp, hn = norm_matmul(h2, ffn_norm_g[l], peer_wq[l].astype(BF16), tn=512, emit_xn=True)
    tables = peer_topk(qp, peer_subkeys[l].astype(BF16))
    peer_t = peer_dense(hn, peer_down[l].astype(BF16), peer_up[l].T.astype(BF16), tables)
    out = final_norm(h2, peer_t.T, final_norm_g)
    return out.reshape(batch, seq, d)
```

```python
import functools
import math

import numpy as np
import jax
import jax.numpy as jnp
from jax import lax
from jax.experimental import pallas as pl
from jax.experimental.pallas import tpu as pltpu

F32 = jnp.float32
BF16 = jnp.bfloat16
EPS = 1e-6
NEG_BIG = -1e30

DA_HEADS = 16
DA_HALF = 64
HEAD_W = 128
HG_HEADS = 16
HG_BLOCK = 256
HG_SUB = 8
XA_HEADS = 4
PEER_HEADS = 8
PEER_NKEYS = 128
PEER_TOPK = 16
NOT_SELECTED = 99.0

VMEM_LIMIT = 56 * 1024 * 1024


def _cparams(sem):
    return pltpu.CompilerParams(dimension_semantics=sem, vmem_limit_bytes=VMEM_LIMIT)


def _nt_dot(a, b):
    return lax.dot_general(a, b, (((1,), (1,)), ((), ())), preferred_element_type=F32)


def _tn_dot(a, b):
    return lax.dot_general(a, b, (((0,), (0,)), ((), ())), preferred_element_type=F32)


def _sigmoid(x):
    return 1.0 / (1.0 + jnp.exp(-x))


def _norm_matmul_kernel(x_ref, g_ref, w_ref, o_ref, xn_ref, *, rows):
    @pl.when(pl.program_id(1) == 0)
    def _():
        g = g_ref[...]

        def chunk(ci, _):
            r0 = pl.multiple_of(ci * rows, rows)
            x = x_ref[pl.ds(r0, rows), :]
            ms = jnp.mean(x * x, axis=-1, keepdims=True)
            xn_ref[pl.ds(r0, rows), :] = (x * lax.rsqrt(ms + EPS) * g).astype(BF16)
            return 0

        lax.fori_loop(0, x_ref.shape[0] // rows, chunk, 0)

    o_ref[...] = jnp.dot(xn_ref[...], w_ref[...], preferred_element_type=F32).astype(o_ref.dtype)


def norm_matmul(x, g, w, *, tm=512, tn=1024, emit_xn=False):
    m, k = x.shape
    n = w.shape[1]
    tm, tn = min(tm, m), min(tn, n)
    assert m % tm == 0 and n % tn == 0
    rows = min(128, tm)
    out_shape = [jax.ShapeDtypeStruct((m, n), BF16)]
    out_specs = [pl.BlockSpec((tm, tn), lambda i, j: (i, j))]
    scratch = [pltpu.VMEM((tm, k), BF16)]
    if emit_xn:
        out_shape.append(jax.ShapeDtypeStruct((m, k), BF16))
        out_specs.append(pl.BlockSpec((tm, k), lambda i, j: (i, 0)))
        scratch = []
    res = pl.pallas_call(
        functools.partial(_norm_matmul_kernel, rows=rows),
        grid=(m // tm, n // tn),
        in_specs=[pl.BlockSpec((tm, k), lambda i, j: (i, 0)),
                  pl.BlockSpec((1, k), lambda i, j: (0, 0)),
                  pl.BlockSpec((k, tn), lambda i, j: (0, j))],
        out_specs=out_specs,
        out_shape=out_shape,
        scratch_shapes=scratch,
        compiler_params=_cparams(("parallel", "arbitrary")),
        name="norm_matmul",
    )(x, g.reshape(1, k).astype(F32), w)
    return res if emit_xn else res[0]


def _matmul_residual_kernel(*refs, n_lhs):
    lhs = refs[:n_lhs]
    ws = refs[n_lhs:2 * n_lhs]
    res_ref, o_ref = refs[2 * n_lhs], refs[2 * n_lhs + 1]
    acc = res_ref[...]
    for a_ref, w_ref in zip(lhs, ws):
        acc = acc + jnp.dot(a_ref[...], w_ref[...], preferred_element_type=F32)
    o_ref[...] = acc


def matmul_residual(lhs_list, w_list, res, *, tm=512, tn=1024):
    m, n = res.shape
    tm, tn = min(tm, m), min(tn, n)
    assert m % tm == 0 and n % tn == 0
    n_lhs = len(lhs_list)
    in_specs = [pl.BlockSpec((tm, a.shape[1]), lambda i, j: (i, 0)) for a in lhs_list]
    in_specs += [pl.BlockSpec((w.shape[0], tn), lambda i, j: (0, j)) for w in w_list]
    in_specs += [pl.BlockSpec((tm, tn), lambda i, j: (i, j))]
    return pl.pallas_call(
        functools.partial(_matmul_residual_kernel, n_lhs=n_lhs),
        grid=(m // tm, n // tn),
        in_specs=in_specs,
        out_specs=pl.BlockSpec((tm, tn), lambda i, j: (i, j)),
        out_shape=jax.ShapeDtypeStruct((m, n), F32),
        compiler_params=_cparams(("parallel", "parallel")),
        name="matmul_residual",
    )(*lhs_list, *w_list, res)


def _diff_attn_kernel(lam_ref, subg_ref, slope_ref, q_ref, k_ref, vt_ref, o_ref, base_ref, acc_ref, *,
                      tq, tc, lam_init):
    qi = pl.program_id(2)
    ncol = 2 * tq
    per_q = tq // tc
    log2e = math.log2(math.e)
    q = (q_ref[...].astype(F32) * (DA_HALF ** -0.5 * log2e)).astype(BF16)
    lane = lax.broadcasted_iota(jnp.int32, q.shape, 1)
    zero = jnp.zeros_like(q)
    q2 = jnp.concatenate([jnp.where(lane < DA_HALF, q, zero), jnp.where(lane >= DA_HALF, q, zero)], axis=0)
    slope = slope_ref[0:1, 0:1] * log2e
    krow = lax.broadcasted_iota(jnp.int32, (tc, ncol), 0)
    base_ref[...] = slope * krow.astype(F32)
    acc_ref[...] = jnp.zeros_like(acc_ref)

    def process(j, stats, masked):
        m, l = stats
        rel = j * tc - qi * tq
        off = slope * rel.astype(F32)
        kj = k_ref[pl.ds(pl.multiple_of(j * tc, tc), tc), :]
        s = _nt_dot(kj, q2) + base_ref[...]
        if masked:
            qpos = lax.broadcasted_iota(jnp.int32, (tc, ncol), 1) % tq
            s = jnp.where(krow + rel <= qpos, s, NEG_BIG)
        m_new = jnp.maximum(m, jnp.max(s, axis=0, keepdims=True) + off)
        p = jnp.exp2(s - (m_new - off))
        alpha = jnp.exp2(m - m_new)
        l = alpha * l + jnp.sum(p, axis=0, keepdims=True)
        pv = jnp.dot(vt_ref[j], p.astype(BF16), preferred_element_type=F32)
        acc_ref[...] = acc_ref[...] * alpha + pv
        return m_new, l

    stats = (jnp.full((1, ncol), NEG_BIG, F32), jnp.zeros((1, ncol), F32))
    stats = lax.fori_loop(0, per_q * qi, lambda j, st: process(j, st, False), stats)
    for u in range(per_q):
        stats = process(per_q * qi + u, stats, True)

    lv = lam_ref[...]
    lam = (jnp.exp(jnp.sum(lv[0:1] * lv[1:2], axis=-1, keepdims=True))
           - jnp.exp(jnp.sum(lv[2:3] * lv[3:4], axis=-1, keepdims=True)) + lam_init)
    on = acc_ref[...] / stats[1]
    o = on[:, :tq] - lam * on[:, tq:]
    ms = jnp.mean(o * o, axis=0, keepdims=True)
    o = o * lax.rsqrt(ms + EPS) * subg_ref[...] * (1.0 - lam_init)
    o_ref[...] = o.T.astype(o_ref.dtype)


def diff_attention(proj, lam4, subln_g, batch, seq, *, tq=512, tc=512, layer_idx=0):
    n = proj.shape[0]
    tq = min(tq, seq)
    tc = min(tc, tq)
    nq, nc = seq // tq, seq // tc
    lam_init = 0.8 - 0.6 * math.exp(-0.3 * layer_idx)
    slopes = np.array([2.0 ** (-8.0 * (h + 1) / DA_HEADS) for h in range(DA_HEADS)], dtype=np.float32)
    slope_tab = jnp.asarray(np.broadcast_to(slopes[:, None, None], (DA_HEADS, 8, 128)).copy())
    da_w = DA_HEADS * HEAD_W
    vt5 = proj[:, 2 * da_w:3 * da_w].reshape(batch, nc, tc, DA_HEADS, HEAD_W).transpose(0, 3, 1, 4, 2)
    return pl.pallas_call(
        functools.partial(_diff_attn_kernel, tq=tq, tc=tc, lam_init=lam_init),
        grid=(batch, DA_HEADS, nq),
        in_specs=[pl.BlockSpec((4, DA_HALF), lambda b, h, i: (0, 0)),
                  pl.BlockSpec((HEAD_W, 1), lambda b, h, i: (0, 0)),
                  pl.BlockSpec((None, 8, 128), lambda b, h, i: (h, 0, 0)),
                  pl.BlockSpec((tq, HEAD_W), lambda b, h, i: (b * nq + i, h)),
                  pl.BlockSpec((seq, HEAD_W), lambda b, h, i: (b, DA_HEADS + h)),
                  pl.BlockSpec((None, None, nc, HEAD_W, tc), lambda b, h, i: (b, h, 0, 0, 0))],
        out_specs=pl.BlockSpec((tq, HEAD_W), lambda b, h, i: (b * nq + i, h)),
        out_shape=jax.ShapeDtypeStruct((n, da_w), BF16),
        scratch_shapes=[pltpu.VMEM((tc, 2 * tq), F32), pltpu.VMEM((HEAD_W, 2 * tq), F32)],
        compiler_params=_cparams(("parallel", "parallel", "arbitrary")),
        name="diff_attention",
    )(lam4, subln_g.reshape(HEAD_W, 1).astype(F32), slope_tab, proj, proj, vt5)


def _hgrn_chunk(q, k, b, v):
    c = q.shape[0]
    row = lax.broadcasted_iota(jnp.int32, (c, 1), 0)
    arow = lax.broadcasted_iota(jnp.int32, (c, c), 0)
    acol = lax.broadcasted_iota(jnp.int32, (c, c), 1)

    def block_level(half):
        grp = 2 * half
        bg = b.reshape(c // grp, grp, HEAD_W)
        ref = jnp.broadcast_to(bg[:, half - 1:half, :], bg.shape).reshape(c, HEAD_W)
        upper = (row % grp) >= half
        qs = jnp.where(upper, q * jnp.exp(jnp.minimum(b - ref, 0.0)), 0.0)
        ks = jnp.where(upper, 0.0, k * jnp.exp(jnp.minimum(ref - b, 0.0)))
        a = _nt_dot(qs.astype(BF16), ks.astype(BF16))
        keep = ((arow // grp) == (acol // grp)) & ((arow % grp) >= half) & ((acol % grp) < half)
        return jnp.where(keep, a, 0.0)

    a_tot = None
    half = c // 2
    while half >= HG_SUB:
        lvl = block_level(half)
        a_tot = lvl if a_tot is None else a_tot + lvl
        half //= 2
    o = jnp.dot(a_tot.astype(BF16), v, preferred_element_type=F32)

    nsub = c // HG_SUB
    q3 = q.reshape(nsub, HG_SUB, HEAD_W)
    k3 = k.reshape(nsub, HG_SUB, HEAD_W)
    b3 = b.reshape(nsub, HG_SUB, HEAD_W)
    v3 = v.astype(F32).reshape(nsub, HG_SUB, HEAD_W)
    trow = lax.broadcasted_iota(jnp.int32, (nsub, HG_SUB, 1), 1)
    o3 = jnp.zeros((nsub, HG_SUB, HEAD_W), F32)
    for s in range(HG_SUB):
        e = jnp.exp(jnp.minimum(b3 - b3[:, s:s + 1, :], 0.0))
        col = jnp.sum(q3 * (k3[:, s:s + 1, :] * e), axis=-1, keepdims=True)
        o3 = o3 + jnp.where(trow >= s, col, 0.0) * v3[:, s:s + 1, :]
    return o + o3.reshape(c, HEAD_W)


def _hgrn_kernel(gam_ref, ng_ref, q_ref, f_ref, i_ref, g_ref, o_ref, st_ref, *, n_chunks, layer_idx):
    @pl.when(pl.program_id(2) == 0)
    def _():
        st_ref[...] = jnp.zeros_like(st_ref)

    gam = gam_ref[...]
    ge = jnp.exp(gam - jnp.max(gam, axis=0, keepdims=True))
    lb = jnp.sum(ge[0:layer_idx + 1], axis=0, keepdims=True) / jnp.sum(ge, axis=0, keepdims=True)
    ng = ng_ref[...]
    c = q_ref.shape[0] // n_chunks
    tri = (lax.broadcasted_iota(jnp.int32, (c, c), 0) >= lax.broadcasted_iota(jnp.int32, (c, c), 1))
    tri = jnp.where(tri, 1.0, 0.0).astype(BF16)

    def chunk(ci, _):
        r0 = pl.multiple_of(ci * c, c)
        qr = q_ref[pl.ds(r0, c), :].astype(F32)
        fr = f_ref[pl.ds(r0, c), :].astype(F32)
        v = i_ref[pl.ds(r0, c), :]
        gr = g_ref[pl.ds(r0, c), :].astype(F32)
        q = qr * _sigmoid(qr)
        f = lb + (1.0 - lb) * _sigmoid(fr)
        logf = jnp.log(f)
        k = 1.0 - f
        hi = logf.astype(BF16)
        r1 = logf - hi.astype(F32)
        mid = r1.astype(BF16)
        lo = (r1 - mid.astype(F32)).astype(BF16)
        b = (jnp.dot(tri, hi, preferred_element_type=F32) + jnp.dot(tri, mid, preferred_element_type=F32)
             + jnp.dot(tri, lo, preferred_element_type=F32))
        st = st_ref[...]
        o = _nt_dot((q * jnp.exp(b)).astype(BF16), st.astype(BF16))
        o = o + _hgrn_chunk(q, k, b, v)
        bend = b[c - 1:c]
        kd = (k * jnp.exp(bend - b)).astype(BF16)
        st_ref[...] = st * jnp.exp(bend) + _tn_dot(v, kd)
        ms = jnp.mean(o * o, axis=-1, keepdims=True)
        on = o * lax.rsqrt(ms + EPS) * ng * (gr * _sigmoid(gr))
        o_ref[pl.ds(r0, c), :] = on.astype(o_ref.dtype)
        return 0

    lax.fori_loop(0, n_chunks, chunk, 0)


def hgrn2(proj, gamma, norm_g, batch, seq, *, tc=512, layer_idx=0, col0=3 * DA_HEADS):
    n = proj.shape[0]
    tc = min(tc, seq)
    ng = seq // tc
    depth1 = gamma.shape[0]

    def spec(off):
        return pl.BlockSpec((tc, HEAD_W), lambda b, h, g: (b * ng + g, col0 + off * HG_HEADS + h))

    return pl.pallas_call(
        functools.partial(_hgrn_kernel, n_chunks=tc // min(HG_BLOCK, tc), layer_idx=layer_idx),
        grid=(batch, HG_HEADS, ng),
        in_specs=[pl.BlockSpec((depth1, HEAD_W), lambda b, h, g: (0, h)),
                  pl.BlockSpec((1, HEAD_W), lambda b, h, g: (0, 0)),
                  spec(0), spec(1), spec(2), spec(3)],
        out_specs=pl.BlockSpec((tc, HEAD_W), lambda b, h, g: (b * ng + g, h)),
        out_shape=jax.ShapeDtypeStruct((n, HG_HEADS * HEAD_W), BF16),
        scratch_shapes=[pltpu.VMEM((HEAD_W, HEAD_W), F32)],
        compiler_params=_cparams(("parallel", "parallel", "arbitrary")),
        name="hgrn2",
    )(gamma.astype(F32), norm_g.reshape(1, HEAD_W).astype(F32), proj, proj, proj, proj)


def _cross_attn_kernel(q_ref, k_ref, v_ref, o_ref, *, scale):
    q = q_ref[...] * jnp.asarray(scale, BF16)
    s = _nt_dot(q, k_ref[...])
    m = jnp.max(s, axis=-1, keepdims=True)
    p = jnp.exp(s - m)
    l = jnp.sum(p, axis=-1, keepdims=True)
    o = jnp.dot(p.astype(BF16), v_ref[...], preferred_element_type=F32) / l
    o_ref[...] = o.astype(o_ref.dtype)


def cross_attention(q, kv, batch, seq, mem_len, *, tq=512):
    n, d = q.shape
    hd = d // XA_HEADS
    tq = min(tq, seq)
    nq = seq // tq
    return pl.pallas_call(
        functools.partial(_cross_attn_kernel, scale=hd ** -0.5),
        grid=(batch, nq, XA_HEADS),
        in_specs=[pl.BlockSpec((tq, hd), lambda b, i, h: (b * nq + i, h)),
                  pl.BlockSpec((mem_len, hd), lambda b, i, h: (b, h)),
                  pl.BlockSpec((mem_len, hd), lambda b, i, h: (b, XA_HEADS + h))],
        out_specs=pl.BlockSpec((tq, hd), lambda b, i, h: (b * nq + i, h)),
        out_shape=jax.ShapeDtypeStruct((n, d), BF16),
        compiler_params=_cparams(("parallel", "parallel", "parallel")),
        name="cross_attention",
    )(q, kv, kv)


def _topk_desc(s, k):
    rows = s.shape[0]
    iota = lax.broadcasted_iota(jnp.int32, s.shape, 0)
    rank = jnp.full(s.shape, NOT_SELECTED, F32)
    vals = []
    cur = s
    for r in range(k):
        m = jnp.max(cur, axis=0, keepdims=True)
        first = jnp.min(jnp.where(cur == m, iota, rows), axis=0, keepdims=True)
        sel = iota == first
        rank = jnp.where(sel, float(r), rank)
        cur = jnp.where(sel, -jnp.inf, cur)
        vals.append(m)
    return jnp.concatenate(vals, axis=0), rank


def _peer_topk_kernel(q_ref, keys_ref, l1_ref, c_ref, r2_ref, e2_ref):
    kk = PEER_TOPK
    for h in range(PEER_HEADS):
        q1 = q_ref[:, (2 * h) * PEER_NKEYS:(2 * h + 1) * PEER_NKEYS]
        q2 = q_ref[:, (2 * h + 1) * PEER_NKEYS:(2 * h + 2) * PEER_NKEYS]
        s1 = _nt_dot(keys_ref[h, 0], q1)
        s2 = _nt_dot(keys_ref[h, 1], q2)
        sv1, rank1 = _topk_desc(s1, kk)
        sv2, rank2 = _topk_desc(s2, kk)
        cand = jnp.concatenate([sv1[a:a + 1] + sv2 for a in range(kk)], axis=0)
        t = cand.shape[1]
        iota = lax.broadcasted_iota(jnp.int32, cand.shape, 0)
        aidx = lax.broadcasted_iota(jnp.int32, (kk, t), 0)
        count = jnp.zeros((kk, t), F32)
        z = jnp.zeros((1, t), F32)
        cur = cand
        best0 = sv1[0:1] + sv2[0:1]
        for _ in range(kk):
            m = jnp.max(cur, axis=0, keepdims=True)
            first = jnp.min(jnp.where(cur == m, iota, kk * kk), axis=0, keepdims=True)
            cur = jnp.where(iota == first, -jnp.inf, cur)
            count = count + jnp.where(aidx == first // kk, 1.0, 0.0)
            z = z + jnp.exp(m - best0)
        l1 = jnp.zeros_like(rank1)
        for a in range(kk):
            l1 = jnp.where(rank1 == float(a), count[a:a + 1], l1)
        sel1 = rank1 < float(kk)
        l1_ref[h] = l1
        c_ref[h] = jnp.where(sel1, jnp.exp(jnp.minimum(s1 - sv1[0:1], 0.0)) / z, 0.0)
        r2_ref[h] = rank2.astype(r2_ref.dtype)
        e2 = jnp.where(rank2 < float(kk), jnp.exp(jnp.minimum(s2 - sv2[0:1], 0.0)), 0.0)
        e2_ref[h] = e2.astype(e2_ref.dtype)


def peer_topk(q, keys, *, tt=256):
    n = q.shape[0]
    tt = min(tt, n)
    tab_spec = pl.BlockSpec((PEER_HEADS, PEER_NKEYS, tt), lambda i: (0, 0, i))
    return pl.pallas_call(
        _peer_topk_kernel,
        grid=(n // tt,),
        in_specs=[pl.BlockSpec((tt, q.shape[1]), lambda i: (i, 0)),
                  pl.BlockSpec(keys.shape, lambda i: (0, 0, 0, 0))],
        out_specs=[tab_spec] * 4,
        out_shape=[jax.ShapeDtypeStruct((PEER_HEADS, PEER_NKEYS, n), dt) for dt in (F32, F32, BF16, BF16)],
        compiler_params=_cparams(("parallel",)),
        name="peer_topk",
    )(q, keys)


def _gelu_exact(x):
    return 0.5 * x * (1.0 + lax.erf(x * (2.0 ** -0.5)))


def _peer_dense_kernel(hn_ref, down_ref, upt_ref, l1_ref, c_ref, r2_ref, e2_ref, o_ref, act_ref, wg_ref, *,
                       groups):
    s = pl.program_id(1)

    @pl.when(s == 0)
    def _():
        o_ref[...] = jnp.zeros_like(o_ref)
        act_ref[1] = jnp.zeros(act_ref.shape[1:], act_ref.dtype)
        wg_ref[1] = jnp.zeros(wg_ref.shape[1:], wg_ref.dtype)

    d, tt = o_ref.shape
    lane_chunk = 256
    units = [(gi, t0) for gi in range(groups) for t0 in range(0, tt, lane_chunk)]
    n_piece = len(units)
    kc, mc = d // n_piece, d // n_piece

    def gate_unit(cur, prv, gi, t0):
        rows = slice(gi * PEER_NKEYS, (gi + 1) * PEER_NKEYS)
        lanes = slice(t0, t0 + lane_chunk)
        g = None
        tile = (PEER_NKEYS // 8, 8, lane_chunk)
        for h in range(PEER_HEADS):
            l1 = jnp.broadcast_to(l1_ref[h, gi:gi + 1, lanes], tile[1:])
            cc = jnp.broadcast_to(c_ref[h, gi:gi + 1, lanes], tile[1:])
            r2 = r2_ref[h, :, lanes].astype(F32).reshape(tile)
            e2 = e2_ref[h, :, lanes].astype(F32).reshape(tile)
            term = jnp.where(r2 < l1[None], e2 * cc[None], 0.0)
            g = term if g is None else g + term
        g = g.reshape(PEER_NKEYS, lane_chunk)
        wg_ref[cur, rows, lanes] = (g * _gelu_exact(act_ref[prv, rows, lanes])).astype(BF16)

    def step(cur, prv):
        wg_prev = wg_ref[prv]
        for p in range(n_piece):
            ms = slice(p * mc, (p + 1) * mc)
            o_ref[ms, :] += jnp.dot(upt_ref[ms, :], wg_prev, preferred_element_type=F32)
            ks = slice(p * kc, (p + 1) * kc)
            part = _nt_dot(down_ref[:, ks], hn_ref[:, ks])
            if p == 0:
                act_ref[cur] = part
            else:
                act_ref[cur] += part
            gate_unit(cur, prv, *units[p])

    step(s % 2, (s + 1) % 2)


def peer_dense(hn, down, upt, tables, *, tt=512, te=512):
    n, d = hn.shape
    n_exp = down.shape[0]
    tt = min(tt, n)
    groups = te // PEER_NKEYS
    nblk = n_exp // te
    l1, c, r2, e2 = tables
    l1 = l1.reshape(PEER_HEADS, PEER_NKEYS // groups, groups, n)
    c = c.reshape(PEER_HEADS, PEER_NKEYS // groups, groups, n)
    def blk(s, lag):
        return jnp.clip(s - lag, 0, nblk - 1)

    row_spec = pl.BlockSpec((PEER_HEADS, None, groups, tt), lambda i, s: (0, blk(s, 1), 0, i))
    tab_spec = pl.BlockSpec((PEER_HEADS, PEER_NKEYS, tt), lambda i, s: (0, 0, i))
    return pl.pallas_call(
        functools.partial(_peer_dense_kernel, groups=groups),
        grid=(n // tt, nblk + 2),
        in_specs=[pl.BlockSpec((tt, d), lambda i, s: (i, 0)),
                  pl.BlockSpec((te, d), lambda i, s: (blk(s, 0), 0)),
                  pl.BlockSpec((d, te), lambda i, s: (0, blk(s, 2))),
                  row_spec, row_spec, tab_spec, tab_spec],
        out_specs=pl.BlockSpec((d, tt), lambda i, s: (0, i)),
        out_shape=jax.ShapeDtypeStruct((d, n), F32),
        scratch_shapes=[pltpu.VMEM((2, te, tt), F32), pltpu.VMEM((2, te, tt), BF16)],
        compiler_params=_cparams(("parallel", "arbitrary")),
        name="peer_dense",
    )(hn, down, upt, l1, c, r2, e2)


def _final_norm_kernel(h_ref, p_ref, g_ref, o_ref):
    x = h_ref[...] + p_ref[...]
    ms = jnp.mean(x * x, axis=-1, keepdims=True)
    o_ref[...] = x * lax.rsqrt(ms + EPS) * g_ref[...]


def final_norm(h, p, g, *, tm=256):
    m, d = h.shape
    tm = min(tm, m)
    row = pl.BlockSpec((tm, d), lambda i: (i, 0))
    return pl.pallas_call(
        _final_norm_kernel,
        grid=(m // tm,),
        in_specs=[row, row, pl.BlockSpec((1, d), lambda i: (0, 0))],
        out_specs=row,
        out_shape=jax.ShapeDtypeStruct((m, d), F32),
        compiler_params=_cparams(("parallel",)),
        name="final_norm",
    )(h, p, g.reshape(1, d).astype(F32))


def kernel(x, mem, mix_norm_g, w_in, da_lambda_q1, da_lambda_k1, da_lambda_q2, da_lambda_k2, da_subln_g, hgrn_gamma, hgrn_norm_g, w_out, cross_norm_g, mem_norm_g, w_cq, w_ckv, w_co, ffn_norm_g, peer_wq, peer_subkeys, peer_down, peer_up, final_norm_g):
    batch, seq, d = x.shape
    mem_len = mem.shape[1]
    depth = w_in.shape[0]
    assert depth == 1 and hgrn_gamma.shape[0] == depth + 1
    l = 0
    da_w = DA_HEADS * HEAD_W
    x2 = x.reshape(batch * seq, d)

    proj = norm_matmul(x2, mix_norm_g[l], w_in[l].astype(BF16))
    lam4 = jnp.stack([da_lambda_q1[l], da_lambda_k1[l], da_lambda_q2[l], da_lambda_k2[l]]).astype(F32)
    a_out = diff_attention(proj, lam4, da_subln_g[l], batch, seq, layer_idx=l)
    b_out = hgrn2(proj, hgrn_gamma, hgrn_norm_g[l], batch, seq, layer_idx=l)
    wo = w_out[l].astype(BF16)
    h1 = matmul_residual([a_out, b_out], [wo[:da_w], wo[da_w:]], x2)

    qc = norm_matmul(h1, cross_norm_g[l], w_cq[l].astype(BF16))
    kv = norm_matmul(mem.reshape(batch * mem_len, d), mem_norm_g[l], w_ckv[l].astype(BF16))
    oc = cross_attention(qc, kv, batch, seq, mem_len)
    h2 = matmul_residual([oc], [w_co[l].astype(BF16)], h1)

    qp, hn = norm_matmul(h2, ffn_norm_g[l], peer_wq[l].astype(BF16), tn=512, emit_xn=True)
    tables = peer_topk(qp, peer_subkeys[l].astype(BF16))
    peer_t = peer_dense(hn, peer_down[l].astype(BF16), peer_up[l].T.astype(BF16), tables)
    out = final_norm(h2, peer_t.T, final_norm_g)
    return out.reshape(batch, seq, d)
```

```python
import functools
import math

import numpy as np
import jax
import jax.numpy as jnp
from jax import lax
from jax.experimental import pallas as pl
from jax.experimental.pallas import tpu as pltpu

F32 = jnp.float32
BF16 = jnp.bfloat16
EPS = 1e-6
NEG_BIG = -1e30

DA_HEADS = 16
DA_HALF = 64
HEAD_W = 128
HG_HEADS = 16
HG_BLOCK = 256
HG_SUB = 8
XA_HEADS = 4
PEER_HEADS = 8
PEER_NKEYS = 128
PEER_TOPK = 16
NOT_SELECTED = 99.0
SUBLANES = 8

VMEM_LIMIT = 56 * 1024 * 1024


def _cparams(sem):
    return pltpu.CompilerParams(dimension_semantics=sem, vmem_limit_bytes=VMEM_LIMIT)


def _nt_dot(a, b):
    return lax.dot_general(a, b, (((1,), (1,)), ((), ())), preferred_element_type=F32)


def _tn_dot(a, b):
    return lax.dot_general(a, b, (((0,), (0,)), ((), ())), preferred_element_type=F32)


def _sigmoid(x):
    return 1.0 / (1.0 + jnp.exp(-x))


def _norm_matmul_kernel(x_ref, g_ref, w_ref, o_ref, xn_ref, *, rows):
    @pl.when(pl.program_id(1) == 0)
    def _():
        g = g_ref[...]

        def chunk(ci, _):
            r0 = pl.multiple_of(ci * rows, rows)
            x = x_ref[pl.ds(r0, rows), :]
            ms = jnp.mean(x * x, axis=-1, keepdims=True)
            xn_ref[pl.ds(r0, rows), :] = (x * lax.rsqrt(ms + EPS) * g).astype(BF16)
            return 0

        lax.fori_loop(0, x_ref.shape[0] // rows, chunk, 0)

    o_ref[...] = jnp.dot(xn_ref[...], w_ref[...], preferred_element_type=F32).astype(o_ref.dtype)


def norm_matmul(x, g, w, *, tm=512, tn=1024, emit_xn=False):
    m, k = x.shape
    n = w.shape[1]
    tm, tn = min(tm, m), min(tn, n)
    assert m % tm == 0 and n % tn == 0
    rows = min(128, tm)
    out_shape = [jax.ShapeDtypeStruct((m, n), BF16)]
    out_specs = [pl.BlockSpec((tm, tn), lambda i, j: (i, j))]
    scratch = [pltpu.VMEM((tm, k), BF16)]
    if emit_xn:
        out_shape.append(jax.ShapeDtypeStruct((m, k), BF16))
        out_specs.append(pl.BlockSpec((tm, k), lambda i, j: (i, 0)))
        scratch = []
    res = pl.pallas_call(
        functools.partial(_norm_matmul_kernel, rows=rows),
        grid=(m // tm, n // tn),
        in_specs=[pl.BlockSpec((tm, k), lambda i, j: (i, 0)),
                  pl.BlockSpec((1, k), lambda i, j: (0, 0)),
                  pl.BlockSpec((k, tn), lambda i, j: (0, j))],
        out_specs=out_specs,
        out_shape=out_shape,
        scratch_shapes=scratch,
        compiler_params=_cparams(("parallel", "arbitrary")),
        name="norm_matmul",
    )(x, g.reshape(1, k).astype(F32), w)
    return res if emit_xn else res[0]


def _matmul_residual_kernel(*refs, n_lhs):
    lhs = refs[:n_lhs]
    ws = refs[n_lhs:2 * n_lhs]
    res_ref, o_ref = refs[2 * n_lhs], refs[2 * n_lhs + 1]
    acc = res_ref[...]
    for a_ref, w_ref in zip(lhs, ws):
        acc = acc + jnp.dot(a_ref[...], w_ref[...], preferred_element_type=F32)
    o_ref[...] = acc


def matmul_residual(lhs_list, w_list, res, *, tm=512, tn=1024):
    m, n = res.shape
    tm, tn = min(tm, m), min(tn, n)
    assert m % tm == 0 and n % tn == 0
    n_lhs = len(lhs_list)
    in_specs = [pl.BlockSpec((tm, a.shape[1]), lambda i, j: (i, 0)) for a in lhs_list]
    in_specs += [pl.BlockSpec((w.shape[0], tn), lambda i, j: (0, j)) for w in w_list]
    in_specs += [pl.BlockSpec((tm, tn), lambda i, j: (i, j))]
    return pl.pallas_call(
        functools.partial(_matmul_residual_kernel, n_lhs=n_lhs),
        grid=(m // tm, n // tn),
        in_specs=in_specs,
        out_specs=pl.BlockSpec((tm, tn), lambda i, j: (i, j)),
        out_shape=jax.ShapeDtypeStruct((m, n), F32),
        compiler_params=_cparams(("parallel", "parallel")),
        name="matmul_residual",
    )(*lhs_list, *w_list, res)


def _diff_attn_kernel(lam_ref, subg_ref, slope_ref, q_ref, k_ref, vt_ref, o_ref, base_ref, acc_ref, *,
                      tq, tc, lam_init):
    qi = pl.program_id(2)
    ncol = 2 * tq
    per_q = tq // tc
    log2e = math.log2(math.e)
    q = (q_ref[...].astype(F32) * (DA_HALF ** -0.5 * log2e)).astype(BF16)
    lane = lax.broadcasted_iota(jnp.int32, q.shape, 1)
    zero = jnp.zeros_like(q)
    q2 = jnp.concatenate([jnp.where(lane < DA_HALF, q, zero), jnp.where(lane >= DA_HALF, q, zero)], axis=0)
    slope = slope_ref[0:1, 0:1] * log2e
    krow = lax.broadcasted_iota(jnp.int32, (tc, ncol), 0)
    base_ref[...] = slope * krow.astype(F32)
    acc_ref[...] = jnp.zeros_like(acc_ref)

    def process(j, stats, masked):
        m, l = stats
        rel = j * tc - qi * tq
        off = slope * rel.astype(F32)
        kj = k_ref[pl.ds(pl.multiple_of(j * tc, tc), tc), :]
        s = _nt_dot(kj, q2) + base_ref[...]
        if masked:
            qpos = lax.broadcasted_iota(jnp.int32, (tc, ncol), 1) % tq
            s = jnp.where(krow + rel <= qpos, s, NEG_BIG)
        m_new = jnp.maximum(m, jnp.max(s, axis=0, keepdims=True) + off)
        p = jnp.exp2(s - (m_new - off))
        alpha = jnp.exp2(m - m_new)
        l = alpha * l + jnp.sum(p, axis=0, keepdims=True)
        pv = jnp.dot(vt_ref[j], p.astype(BF16), preferred_element_type=F32)
        acc_ref[...] = acc_ref[...] * alpha + pv
        return m_new, l

    stats = (jnp.full((1, ncol), NEG_BIG, F32), jnp.zeros((1, ncol), F32))
    stats = lax.fori_loop(0, per_q * qi, lambda j, st: process(j, st, False), stats)
    for u in range(per_q):
        stats = process(per_q * qi + u, stats, True)

    lv = lam_ref[...]
    lam = (jnp.exp(jnp.sum(lv[0:1] * lv[1:2], axis=-1, keepdims=True))
           - jnp.exp(jnp.sum(lv[2:3] * lv[3:4], axis=-1, keepdims=True)) + lam_init)
    on = acc_ref[...] / stats[1]
    o = on[:, :tq] - lam * on[:, tq:]
    ms = jnp.mean(o * o, axis=0, keepdims=True)
    o = o * lax.rsqrt(ms + EPS) * subg_ref[...] * (1.0 - lam_init)
    o_ref[...] = o.T.astype(o_ref.dtype)


def diff_attention(proj, lam4, subln_g, batch, seq, *, tq=512, tc=512, layer_idx=0):
    n = proj.shape[0]
    tq = min(tq, seq)
    tc = min(tc, tq)
    nq, nc = seq // tq, seq // tc
    lam_init = 0.8 - 0.6 * math.exp(-0.3 * layer_idx)
    slopes = np.array([2.0 ** (-8.0 * (h + 1) / DA_HEADS) for h in range(DA_HEADS)], dtype=np.float32)
    slope_tab = jnp.asarray(np.broadcast_to(slopes[:, None, None], (DA_HEADS, 8, 128)).copy())
    da_w = DA_HEADS * HEAD_W
    vt5 = proj[:, 2 * da_w:3 * da_w].reshape(batch, nc, tc, DA_HEADS, HEAD_W).transpose(0, 3, 1, 4, 2)
    return pl.pallas_call(
        functools.partial(_diff_attn_kernel, tq=tq, tc=tc, lam_init=lam_init),
        grid=(batch, DA_HEADS, nq),
        in_specs=[pl.BlockSpec((4, DA_HALF), lambda b, h, i: (0, 0)),
                  pl.BlockSpec((HEAD_W, 1), lambda b, h, i: (0, 0)),
                  pl.BlockSpec((None, 8, 128), lambda b, h, i: (h, 0, 0)),
                  pl.BlockSpec((tq, HEAD_W), lambda b, h, i: (b * nq + i, h)),
                  pl.BlockSpec((seq, HEAD_W), lambda b, h, i: (b, DA_HEADS + h)),
                  pl.BlockSpec((None, None, nc, HEAD_W, tc), lambda b, h, i: (b, h, 0, 0, 0))],
        out_specs=pl.BlockSpec((tq, HEAD_W), lambda b, h, i: (b * nq + i, h)),
        out_shape=jax.ShapeDtypeStruct((n, da_w), BF16),
        scratch_shapes=[pltpu.VMEM((tc, 2 * tq), F32), pltpu.VMEM((HEAD_W, 2 * tq), F32)],
        compiler_params=_cparams(("parallel", "parallel", "arbitrary")),
        name="diff_attention",
    )(lam4, subln_g.reshape(HEAD_W, 1).astype(F32), slope_tab, proj, proj, vt5)


def _hgrn_chunk(q, k, b, v):
    c = q.shape[0]
    row = lax.broadcasted_iota(jnp.int32, (c, 1), 0)
    arow = lax.broadcasted_iota(jnp.int32, (c, c), 0)
    acol = lax.broadcasted_iota(jnp.int32, (c, c), 1)

    def block_level(half):
        grp = 2 * half
        bg = b.reshape(c // grp, grp, HEAD_W)
        ref = jnp.broadcast_to(bg[:, half - 1:half, :], bg.shape).reshape(c, HEAD_W)
        upper = (row % grp) >= half
        qs = jnp.where(upper, q * jnp.exp(jnp.minimum(b - ref, 0.0)), 0.0)
        ks = jnp.where(upper, 0.0, k * jnp.exp(jnp.minimum(ref - b, 0.0)))
        a = _nt_dot(qs.astype(BF16), ks.astype(BF16))
        keep = ((arow // grp) == (acol // grp)) & ((arow % grp) >= half) & ((acol % grp) < half)
        return jnp.where(keep, a, 0.0)

    a_tot = None
    half = c // 2
    while half >= HG_SUB:
        lvl = block_level(half)
        a_tot = lvl if a_tot is None else a_tot + lvl
        half //= 2
    o = jnp.dot(a_tot.astype(BF16), v, preferred_element_type=F32)

    nsub = c // HG_SUB
    q3 = q.reshape(nsub, HG_SUB, HEAD_W)
    k3 = k.reshape(nsub, HG_SUB, HEAD_W)
    b3 = b.reshape(nsub, HG_SUB, HEAD_W)
    v3 = v.astype(F32).reshape(nsub, HG_SUB, HEAD_W)
    trow = lax.broadcasted_iota(jnp.int32, (nsub, HG_SUB, 1), 1)
    o3 = jnp.zeros((nsub, HG_SUB, HEAD_W), F32)
    for s in range(HG_SUB):
        e = jnp.exp(jnp.minimum(b3 - b3[:, s:s + 1, :], 0.0))
        col = jnp.sum(q3 * (k3[:, s:s + 1, :] * e), axis=-1, keepdims=True)
        o3 = o3 + jnp.where(trow >= s, col, 0.0) * v3[:, s:s + 1, :]
    return o + o3.reshape(c, HEAD_W)


def _hgrn_kernel(gam_ref, ng_ref, q_ref, f_ref, i_ref, g_ref, o_ref, st_ref, *, n_chunks, layer_idx):
    @pl.when(pl.program_id(2) == 0)
    def _():
        st_ref[...] = jnp.zeros_like(st_ref)

    gam = gam_ref[...]
    ge = jnp.exp(gam - jnp.max(gam, axis=0, keepdims=True))
    lb = jnp.sum(ge[0:layer_idx + 1], axis=0, keepdims=True) / jnp.sum(ge, axis=0, keepdims=True)
    ng = ng_ref[...]
    c = q_ref.shape[0] // n_chunks
    tri = (lax.broadcasted_iota(jnp.int32, (c, c), 0) >= lax.broadcasted_iota(jnp.int32, (c, c), 1))
    tri = jnp.where(tri, 1.0, 0.0).astype(BF16)

    def chunk(ci, _):
        r0 = pl.multiple_of(ci * c, c)
        qr = q_ref[pl.ds(r0, c), :].astype(F32)
        fr = f_ref[pl.ds(r0, c), :].astype(F32)
        v = i_ref[pl.ds(r0, c), :]
        gr = g_ref[pl.ds(r0, c), :].astype(F32)
        q = qr * _sigmoid(qr)
        f = lb + (1.0 - lb) * _sigmoid(fr)
        logf = jnp.log(f)
        k = 1.0 - f
        hi = logf.astype(BF16)
        r1 = logf - hi.astype(F32)
        mid = r1.astype(BF16)
        lo = (r1 - mid.astype(F32)).astype(BF16)
        b = (jnp.dot(tri, hi, preferred_element_type=F32) + jnp.dot(tri, mid, preferred_element_type=F32)
             + jnp.dot(tri, lo, preferred_element_type=F32))
        st = st_ref[...]
        o = _nt_dot((q * jnp.exp(b)).astype(BF16), st.astype(BF16))
        o = o + _hgrn_chunk(q, k, b, v)
        bend = b[c - 1:c]
        kd = (k * jnp.exp(bend - b)).astype(BF16)
        st_ref[...] = st * jnp.exp(bend) + _tn_dot(v, kd)
        ms = jnp.mean(o * o, axis=-1, keepdims=True)
        on = o * lax.rsqrt(ms + EPS) * ng * (gr * _sigmoid(gr))
        o_ref[pl.ds(r0, c), :] = on.astype(o_ref.dtype)
        return 0

    lax.fori_loop(0, n_chunks, chunk, 0)


def hgrn2(proj, gamma, norm_g, batch, seq, *, tc=512, layer_idx=0, col0=3 * DA_HEADS):
    n = proj.shape[0]
    tc = min(tc, seq)
    ng = seq // tc
    depth1 = gamma.shape[0]

    def spec(off):
        return pl.BlockSpec((tc, HEAD_W), lambda b, h, g: (b * ng + g, col0 + off * HG_HEADS + h))

    return pl.pallas_call(
        functools.partial(_hgrn_kernel, n_chunks=tc // min(HG_BLOCK, tc), layer_idx=layer_idx),
        grid=(batch, HG_HEADS, ng),
        in_specs=[pl.BlockSpec((depth1, HEAD_W), lambda b, h, g: (0, h)),
                  pl.BlockSpec((1, HEAD_W), lambda b, h, g: (0, 0)),
                  spec(0), spec(1), spec(2), spec(3)],
        out_specs=pl.BlockSpec((tc, HEAD_W), lambda b, h, g: (b * ng + g, h)),
        out_shape=jax.ShapeDtypeStruct((n, HG_HEADS * HEAD_W), BF16),
        scratch_shapes=[pltpu.VMEM((HEAD_W, HEAD_W), F32)],
        compiler_params=_cparams(("parallel", "parallel", "arbitrary")),
        name="hgrn2",
    )(gamma.astype(F32), norm_g.reshape(1, HEAD_W).astype(F32), proj, proj, proj, proj)


def _cross_attn_kernel(q_ref, k_ref, v_ref, o_ref, *, scale):
    q = q_ref[...] * jnp.asarray(scale, BF16)
    s = _nt_dot(q, k_ref[...])
    m = jnp.max(s, axis=-1, keepdims=True)
    p = jnp.exp(s - m)
    l = jnp.sum(p, axis=-1, keepdims=True)
    o = jnp.dot(p.astype(BF16), v_ref[...], preferred_element_type=F32) / l
    o_ref[...] = o.astype(o_ref.dtype)


def cross_attention(q, kv, batch, seq, mem_len, *, tq=512):
    n, d = q.shape
    hd = d // XA_HEADS
    tq = min(tq, seq)
    nq = seq // tq
    return pl.pallas_call(
        functools.partial(_cross_attn_kernel, scale=hd ** -0.5),
        grid=(batch, nq, XA_HEADS),
        in_specs=[pl.BlockSpec((tq, hd), lambda b, i, h: (b * nq + i, h)),
                  pl.BlockSpec((mem_len, hd), lambda b, i, h: (b, h)),
                  pl.BlockSpec((mem_len, hd), lambda b, i, h: (b, XA_HEADS + h))],
        out_specs=pl.BlockSpec((tq, hd), lambda b, i, h: (b * nq + i, h)),
        out_shape=jax.ShapeDtypeStruct((n, d), BF16),
        compiler_params=_cparams(("parallel", "parallel", "parallel")),
        name="cross_attention",
    )(q, kv, kv)


def _topk_desc(s, k):
    rows = s.shape[0]
    iota = lax.broadcasted_iota(jnp.int32, s.shape, 0)
    rank = jnp.full(s.shape, NOT_SELECTED, F32)
    vals = []
    cur = s
    for r in range(k):
        m = jnp.max(cur, axis=0, keepdims=True)
        first = jnp.min(jnp.where(cur == m, iota, rows), axis=0, keepdims=True)
        sel = iota == first
        rank = jnp.where(sel, float(r), rank)
        cur = jnp.where(sel, -jnp.inf, cur)
        vals.append(m)
    return jnp.concatenate(vals, axis=0), rank


def _peer_topk_kernel(q_ref, keys_ref, l1_ref, c_ref, r2_ref, e2_ref):
    kk = PEER_TOPK
    for h in range(PEER_HEADS):
        q1 = q_ref[:, (2 * h) * PEER_NKEYS:(2 * h + 1) * PEER_NKEYS]
        q2 = q_ref[:, (2 * h + 1) * PEER_NKEYS:(2 * h + 2) * PEER_NKEYS]
        s1 = _nt_dot(keys_ref[h, 0], q1)
        s2 = _nt_dot(keys_ref[h, 1], q2)
        sv1, rank1 = _topk_desc(s1, kk)
        sv2, rank2 = _topk_desc(s2, kk)
        t = s1.shape[1]
        segs, seg_rows = [], []
        for a in range(kk):
            nb = kk // (a + 1)
            if nb > 1:
                rows = -(-nb // SUBLANES) * SUBLANES
                seg = sv1[a:a + 1] + sv2[0:rows]
                if rows != nb:
                    seg = jnp.where(lax.broadcasted_iota(jnp.int32, seg.shape, 0) < nb, seg, -jnp.inf)
                segs.append(seg)
                seg_rows.append(rows)
        single0 = len(segs)
        n_single = kk - single0
        assert n_single % SUBLANES == 0
        segs.append(sv1[single0:kk] + sv2[0:1])
        cand = jnp.concatenate(segs, axis=0)
        iota = lax.broadcasted_iota(jnp.int32, cand.shape, 0)
        picked = jnp.zeros(cand.shape, F32)
        z = jnp.zeros((1, t), F32)
        cur = cand
        best0 = sv1[0:1] + sv2[0:1]
        for _ in range(kk):
            m = jnp.max(cur, axis=0, keepdims=True)
            first = jnp.min(jnp.where(cur == m, iota, cand.shape[0]), axis=0, keepdims=True)
            sel = iota == first
            cur = jnp.where(sel, -jnp.inf, cur)
            picked = jnp.where(sel, 1.0, picked)
            z = z + jnp.exp(m - best0)
        count, r0 = [], 0
        for rows in seg_rows:
            count.append(jnp.sum(picked[r0:r0 + rows], axis=0, keepdims=True))
            r0 += rows
        count += [picked[r0 + i:r0 + i + 1] for i in range(n_single)]
        l1 = jnp.zeros_like(rank1)
        for a in range(kk):
            l1 = jnp.where(rank1 == float(a), count[a], l1)
        sel1 = rank1 < float(kk)
        l1_ref[h] = l1
        c_ref[h] = jnp.where(sel1, jnp.exp(jnp.minimum(s1 - sv1[0:1], 0.0)) / z, 0.0)
        r2_ref[h] = rank2.astype(r2_ref.dtype)
        e2 = jnp.where(rank2 < float(kk), jnp.exp(jnp.minimum(s2 - sv2[0:1], 0.0)), 0.0)
        e2_ref[h] = e2.astype(e2_ref.dtype)


def peer_topk(q, keys, *, tt=256):
    n = q.shape[0]
    tt = min(tt, n)
    tab_spec = pl.BlockSpec((PEER_HEADS, PEER_NKEYS, tt), lambda i: (0, 0, i))
    return pl.pallas_call(
        _peer_topk_kernel,
        grid=(n // tt,),
        in_specs=[pl.BlockSpec((tt, q.shape[1]), lambda i: (i, 0)),
                  pl.BlockSpec(keys.shape, lambda i: (0, 0, 0, 0))],
        out_specs=[tab_spec] * 4,
        out_shape=[jax.ShapeDtypeStruct((PEER_HEADS, PEER_NKEYS, n), dt) for dt in (F32, F32, BF16, BF16)],
        compiler_params=_cparams(("parallel",)),
        name="peer_topk",
    )(q, keys)


def _gelu_exact(x):
    return 0.5 * x * (1.0 + lax.erf(x * (2.0 ** -0.5)))


def _peer_dense_kernel(hn_ref, down_ref, upt_ref, l1_ref, c_ref, r2_ref, e2_ref, o_ref, act_ref, wg_ref, *,
                       groups, nblk):
    s = pl.program_id(1)
    row0 = (jnp.clip(s - 1, 0, nblk - 1) * groups) % SUBLANES

    @pl.when(s == 0)
    def _():
        o_ref[...] = jnp.zeros_like(o_ref)
        act_ref[1] = jnp.zeros(act_ref.shape[1:], act_ref.dtype)
        wg_ref[1] = jnp.zeros(wg_ref.shape[1:], wg_ref.dtype)

    d, tt = o_ref.shape
    lane_chunk = 256
    units = [(gi, t0) for gi in range(groups) for t0 in range(0, tt, lane_chunk)]
    n_piece = len(units)
    kc, mc = d // n_piece, d // n_piece

    def gate_unit(cur, prv, gi, t0):
        rows = slice(gi * PEER_NKEYS, (gi + 1) * PEER_NKEYS)
        lanes = slice(t0, t0 + lane_chunk)
        g = None
        for h in range(PEER_HEADS):
            l1 = l1_ref[h, pl.ds(row0 + gi, 1), lanes].astype(BF16)
            cc = c_ref[h, pl.ds(row0 + gi, 1), lanes].astype(BF16)
            e2 = e2_ref[h, :, lanes]
            term = jnp.where(r2_ref[h, :, lanes] < l1, e2 * cc, jnp.zeros_like(e2))
            g = term if g is None else g + term
        wg_ref[cur, rows, lanes] = g * _gelu_exact(act_ref[prv, rows, lanes]).astype(BF16)

    def step(cur, prv):
        wg_prev = wg_ref[prv]
        for p in range(n_piece):
            ms = slice(p * mc, (p + 1) * mc)
            o_ref[ms, :] += jnp.dot(upt_ref[ms, :], wg_prev, preferred_element_type=F32)
            ks = slice(p * kc, (p + 1) * kc)
            part = _nt_dot(down_ref[:, ks], hn_ref[:, ks])
            if p == 0:
                act_ref[cur] = part
            else:
                act_ref[cur] += part
            gate_unit(cur, prv, *units[p])

    step(s % 2, (s + 1) % 2)


def peer_dense(hn, down, upt, tables, *, tt=512, te=512):
    n, d = hn.shape
    n_exp = down.shape[0]
    tt = min(tt, n)
    groups = te // PEER_NKEYS
    nblk = n_exp // te
    l1, c, r2, e2 = tables
    assert SUBLANES % groups == 0

    def blk(s, lag):
        return jnp.clip(s - lag, 0, nblk - 1)

    row_spec = pl.BlockSpec((PEER_HEADS, SUBLANES, tt), lambda i, s: (0, blk(s, 1) * groups // SUBLANES, i))
    tab_spec = pl.BlockSpec((PEER_HEADS, PEER_NKEYS, tt), lambda i, s: (0, 0, i))
    return pl.pallas_call(
        functools.partial(_peer_dense_kernel, groups=groups, nblk=nblk),
        grid=(n // tt, nblk + 2),
        in_specs=[pl.BlockSpec((tt, d), lambda i, s: (i, 0)),
                  pl.BlockSpec((te, d), lambda i, s: (blk(s, 0), 0)),
                  pl.BlockSpec((d, te), lambda i, s: (0, blk(s, 2))),
                  row_spec, row_spec, tab_spec, tab_spec],
        out_specs=pl.BlockSpec((d, tt), lambda i, s: (0, i)),
        out_shape=jax.ShapeDtypeStruct((d, n), F32),
        scratch_shapes=[pltpu.VMEM((2, te, tt), F32), pltpu.VMEM((2, te, tt), BF16)],
        compiler_params=_cparams(("parallel", "arbitrary")),
        name="peer_dense",
    )(hn, down, upt, l1, c, r2, e2)


def _final_norm_kernel(h_ref, pt_ref, g_ref, o_ref):
    x = h_ref[...] + pt_ref[...].T
    ms = jnp.mean(x * x, axis=-1, keepdims=True)
    o_ref[...] = x * lax.rsqrt(ms + EPS) * g_ref[...]


def final_norm(h, pt, g, *, tm=256):
    m, d = h.shape
    tm = min(tm, m)
    row = pl.BlockSpec((tm, d), lambda i: (i, 0))
    return pl.pallas_call(
        _final_norm_kernel,
        grid=(m // tm,),
        in_specs=[row, pl.BlockSpec((d, tm), lambda i: (0, i)), pl.BlockSpec((1, d), lambda i: (0, 0))],
        out_specs=row,
        out_shape=jax.ShapeDtypeStruct((m, d), F32),
        compiler_params=_cparams(("parallel",)),
        name="final_norm",
    )(h, pt, g.reshape(1, d).astype(F32))


def kernel(x, mem, mix_norm_g, w_in, da_lambda_q1, da_lambda_k1, da_lambda_q2, da_lambda_k2, da_subln_g, hgrn_gamma, hgrn_norm_g, w_out, cross_norm_g, mem_norm_g, w_cq, w_ckv, w_co, ffn_norm_g, peer_wq, peer_subkeys, peer_down, peer_up, final_norm_g):
    batch, seq, d = x.shape
    mem_len = mem.shape[1]
    depth = w_in.shape[0]
    assert depth == 1 and hgrn_gamma.shape[0] == depth + 1
    l = 0
    da_w = DA_HEADS * HEAD_W
    x2 = x.reshape(batch * seq, d)

    proj = norm_matmul(x2, mix_norm_g[l], w_in[l].astype(BF16))
    lam4 = jnp.stack([da_lambda_q1[l], da_lambda_k1[l], da_lambda_q2[l], da_lambda_k2[l]]).astype(F32)
    a_out = diff_attention(proj, lam4, da_subln_g[l], batch, seq, layer_idx=l)
    b_out = hgrn2(proj, hgrn_gamma, hgrn_norm_g[l], batch, seq, layer_idx=l)
    wo = w_out[l].astype(BF16)
    h1 = matmul_residual([a_out, b_out], [wo[:da_w], wo[da_w:]], x2)

    qc = norm_matmul(h1, cross_norm_g[l], w_cq[l].astype(BF16))
    kv = norm_matmul(mem.reshape(batch * mem_len, d), mem_norm_g[l], w_ckv[l].astype(BF16))
    oc = cross_attention(qc, kv, batch, seq, mem_len)
    h2 = matmul_residual([oc], [w_co[l].astype(BF16)], h1)

    qp, hn = norm_matmul(h2, ffn_norm_g[l], peer_wq[l].astype(BF16), tn=512, emit_xn=True)
    tables = peer_topk(qp, peer_subkeys[l].astype(BF16))
    peer_t = peer_dense(hn, peer_down[l].astype(BF16), peer_up[l].T.astype(BF16), tables)
    out = final_norm(h2, peer_t, final_norm_g)
    return out.reshape(batch, seq, d)
```

```python
import functools
import math

import numpy as np
import jax
import jax.numpy as jnp
from jax import lax
from jax.experimental import pallas as pl
from jax.experimental.pallas import tpu as pltpu

F32 = jnp.float32
BF16 = jnp.bfloat16
EPS = 1e-6
NEG_BIG = -1e30

DA_HEADS = 16
DA_HALF = 64
HEAD_W = 128
HG_HEADS = 16
HG_BLOCK = 256
HG_SUB = 8
XA_HEADS = 4
PEER_HEADS = 8
PEER_NKEYS = 128
PEER_TOPK = 16
NOT_SELECTED = 99.0
SUBLANES = 8

VMEM_LIMIT = 56 * 1024 * 1024


def _cparams(sem):
    return pltpu.CompilerParams(dimension_semantics=sem, vmem_limit_bytes=VMEM_LIMIT)


def _nt_dot(a, b):
    return lax.dot_general(a, b, (((1,), (1,)), ((), ())), preferred_element_type=F32)


def _tn_dot(a, b):
    return lax.dot_general(a, b, (((0,), (0,)), ((), ())), preferred_element_type=F32)


def _sigmoid(x):
    return 0.5 * jnp.tanh(0.5 * x) + 0.5


def _norm_matmul_kernel(x_ref, g_ref, w_ref, o_ref, xn_ref, *, rows):
    @pl.when(pl.program_id(1) == 0)
    def _():
        g = g_ref[...]

        def chunk(ci, _):
            r0 = pl.multiple_of(ci * rows, rows)
            x = x_ref[pl.ds(r0, rows), :]
            ms = jnp.mean(x * x, axis=-1, keepdims=True)
            xn_ref[pl.ds(r0, rows), :] = (x * lax.rsqrt(ms + EPS) * g).astype(BF16)
            return 0

        lax.fori_loop(0, x_ref.shape[0] // rows, chunk, 0)

    o_ref[...] = jnp.dot(xn_ref[...], w_ref[...], preferred_element_type=F32).astype(o_ref.dtype)


def norm_matmul(x, g, w, *, tm=512, tn=1024, emit_xn=False):
    m, k = x.shape
    n = w.shape[1]
    tm, tn = min(tm, m), min(tn, n)
    assert m % tm == 0 and n % tn == 0
    rows = min(128, tm)
    out_shape = [jax.ShapeDtypeStruct((m, n), BF16)]
    out_specs = [pl.BlockSpec((tm, tn), lambda i, j: (i, j))]
    scratch = [pltpu.VMEM((tm, k), BF16)]
    if emit_xn:
        out_shape.append(jax.ShapeDtypeStruct((m, k), BF16))
        out_specs.append(pl.BlockSpec((tm, k), lambda i, j: (i, 0)))
        scratch = []
    res = pl.pallas_call(
        functools.partial(_norm_matmul_kernel, rows=rows),
        grid=(m // tm, n // tn),
        in_specs=[pl.BlockSpec((tm, k), lambda i, j: (i, 0)),
                  pl.BlockSpec((1, k), lambda i, j: (0, 0)),
                  pl.BlockSpec((k, tn), lambda i, j: (0, j))],
        out_specs=out_specs,
        out_shape=out_shape,
        scratch_shapes=scratch,
        compiler_params=_cparams(("parallel", "arbitrary")),
        name="norm_matmul",
    )(x, g.reshape(1, k).astype(F32), w)
    return res if emit_xn else res[0]


def _matmul_residual_kernel(*refs, n_lhs):
    lhs = refs[:n_lhs]
    ws = refs[n_lhs:2 * n_lhs]
    res_ref, o_ref = refs[2 * n_lhs], refs[2 * n_lhs + 1]
    acc = res_ref[...]
    for a_ref, w_ref in zip(lhs, ws):
        acc = acc + jnp.dot(a_ref[...], w_ref[...], preferred_element_type=F32)
    o_ref[...] = acc


def matmul_residual(lhs_list, w_list, res, *, tm=512, tn=1024):
    m, n = res.shape
    tm, tn = min(tm, m), min(tn, n)
    assert m % tm == 0 and n % tn == 0
    n_lhs = len(lhs_list)
    in_specs = [pl.BlockSpec((tm, a.shape[1]), lambda i, j: (i, 0)) for a in lhs_list]
    in_specs += [pl.BlockSpec((w.shape[0], tn), lambda i, j: (0, j)) for w in w_list]
    in_specs += [pl.BlockSpec((tm, tn), lambda i, j: (i, j))]
    return pl.pallas_call(
        functools.partial(_matmul_residual_kernel, n_lhs=n_lhs),
        grid=(m // tm, n // tn),
        in_specs=in_specs,
        out_specs=pl.BlockSpec((tm, tn), lambda i, j: (i, j)),
        out_shape=jax.ShapeDtypeStruct((m, n), F32),
        compiler_params=_cparams(("parallel", "parallel")),
        name="matmul_residual",
    )(*lhs_list, *w_list, res)


def _diff_attn_kernel(lam_ref, subg_ref, slope_ref, q_ref, k_ref, vt_ref, o_ref, base_ref, acc_ref, *,
                      tq, tc, lam_init):
    qi = pl.program_id(2)
    ncol = 2 * tq
    per_q = tq // tc
    log2e = math.log2(math.e)
    q = (q_ref[...].astype(F32) * (DA_HALF ** -0.5 * log2e)).astype(BF16)
    lane = lax.broadcasted_iota(jnp.int32, q.shape, 1)
    zero = jnp.zeros_like(q)
    q2 = jnp.concatenate([jnp.where(lane < DA_HALF, q, zero), jnp.where(lane >= DA_HALF, q, zero)], axis=0)
    slope = slope_ref[0:1, 0:1] * log2e
    krow = lax.broadcasted_iota(jnp.int32, (tc, ncol), 0)
    base_ref[...] = slope * krow.astype(F32)
    acc_ref[...] = jnp.zeros_like(acc_ref)

    def process(j, m, masked):
        rel = j * tc - qi * tq
        off = slope * rel.astype(F32)
        kj = k_ref[pl.ds(pl.multiple_of(j * tc, tc), tc), :]
        s = _nt_dot(kj, q2) + base_ref[...]
        if masked:
            qpos = lax.broadcasted_iota(jnp.int32, (tc, ncol), 1) % tq
            s = jnp.where(krow + rel <= qpos, s, NEG_BIG)
        m_new = jnp.maximum(m, jnp.max(s, axis=0, keepdims=True) + off)
        p = jnp.exp2((s - (m_new - off)).astype(BF16))
        alpha = jnp.exp2(m - m_new)
        acc_ref[...] = acc_ref[...] * alpha + jnp.dot(vt_ref[j], p, preferred_element_type=F32)
        return m_new

    m = jnp.full((1, ncol), NEG_BIG, F32)
    m = lax.fori_loop(0, per_q * qi, lambda j, mm: process(j, mm, False), m)
    for u in range(per_q):
        m = process(per_q * qi + u, m, True)

    lv = lam_ref[...]
    lam = (jnp.exp(jnp.sum(lv[0:1] * lv[1:2], axis=-1, keepdims=True))
           - jnp.exp(jnp.sum(lv[2:3] * lv[3:4], axis=-1, keepdims=True)) + lam_init)
    on = acc_ref[0:HEAD_W, :] / acc_ref[HEAD_W:HEAD_W + 1, :]
    o = on[:, :tq] - lam * on[:, tq:]
    ms = jnp.mean(o * o, axis=0, keepdims=True)
    o = o * lax.rsqrt(ms + EPS) * subg_ref[...] * (1.0 - lam_init)
    o_ref[...] = o.T.astype(o_ref.dtype)


def diff_attention(proj, lam4, subln_g, batch, seq, *, tq=512, tc=512, layer_idx=0):
    n = proj.shape[0]
    tq = min(tq, seq)
    tc = min(tc, tq)
    nq, nc = seq // tq, seq // tc
    lam_init = 0.8 - 0.6 * math.exp(-0.3 * layer_idx)
    slopes = np.array([2.0 ** (-8.0 * (h + 1) / DA_HEADS) for h in range(DA_HEADS)], dtype=np.float32)
    slope_tab = jnp.asarray(np.broadcast_to(slopes[:, None, None], (DA_HEADS, 8, 128)).copy())
    da_w = DA_HEADS * HEAD_W
    vt5 = proj[:, 2 * da_w:3 * da_w].reshape(batch, nc, tc, DA_HEADS, HEAD_W).transpose(0, 3, 1, 4, 2)
    ones_tile = jnp.zeros((2 * SUBLANES, tc), BF16).at[0].set(1)
    vt5 = jnp.concatenate([vt5, jnp.broadcast_to(ones_tile, vt5.shape[:3] + ones_tile.shape)], axis=3)
    vrows = HEAD_W + 2 * SUBLANES
    return pl.pallas_call(
        functools.partial(_diff_attn_kernel, tq=tq, tc=tc, lam_init=lam_init),
        grid=(batch, DA_HEADS, nq),
        in_specs=[pl.BlockSpec((4, DA_HALF), lambda b, h, i: (0, 0)),
                  pl.BlockSpec((HEAD_W, 1), lambda b, h, i: (0, 0)),
                  pl.BlockSpec((None, 8, 128), lambda b, h, i: (h, 0, 0)),
                  pl.BlockSpec((tq, HEAD_W), lambda b, h, i: (b * nq + i, h)),
                  pl.BlockSpec((seq, HEAD_W), lambda b, h, i: (b, DA_HEADS + h)),
                  pl.BlockSpec((None, None, nc, vrows, tc), lambda b, h, i: (b, h, 0, 0, 0))],
        out_specs=pl.BlockSpec((tq, HEAD_W), lambda b, h, i: (b * nq + i, h)),
        out_shape=jax.ShapeDtypeStruct((n, da_w), BF16),
        scratch_shapes=[pltpu.VMEM((tc, 2 * tq), F32), pltpu.VMEM((vrows, 2 * tq), F32)],
        compiler_params=_cparams(("parallel", "parallel", "arbitrary")),
        name="diff_attention",
    )(lam4, subln_g.reshape(HEAD_W, 1).astype(F32), slope_tab, proj, proj, vt5)


def _hgrn_chunk(q, k, b, v):
    c = q.shape[0]
    row = lax.broadcasted_iota(jnp.int32, (c, 1), 0)
    arow = lax.broadcasted_iota(jnp.int32, (c, c), 0)
    acol = lax.broadcasted_iota(jnp.int32, (c, c), 1)

    def block_level(half):
        grp = 2 * half
        bg = b.reshape(c // grp, grp, HEAD_W)
        ref = jnp.broadcast_to(bg[:, half - 1:half, :], bg.shape).reshape(c, HEAD_W)
        upper = (row % grp) >= half
        qs = jnp.where(upper, q * jnp.exp2(jnp.minimum(b - ref, 0.0)), 0.0)
        ks = jnp.where(upper, 0.0, k * jnp.exp2(jnp.minimum(ref - b, 0.0)))
        a = _nt_dot(qs.astype(BF16), ks.astype(BF16))
        keep = ((arow // grp) == (acol // grp)) & ((arow % grp) >= half) & ((acol % grp) < half)
        return jnp.where(keep, a, 0.0)

    a_tot = None
    half = c // 2
    while half >= HG_SUB:
        lvl = block_level(half)
        a_tot = lvl if a_tot is None else a_tot + lvl
        half //= 2
    o = jnp.dot(a_tot.astype(BF16), v, preferred_element_type=F32)

    nsub = c // HG_SUB
    q3 = q.reshape(nsub, HG_SUB, HEAD_W)
    k3 = k.reshape(nsub, HG_SUB, HEAD_W)
    b3 = b.reshape(nsub, HG_SUB, HEAD_W)
    v3 = v.astype(F32).reshape(nsub, HG_SUB, HEAD_W)
    trow = lax.broadcasted_iota(jnp.int32, (nsub, HG_SUB, 1), 1)
    o3 = jnp.zeros((nsub, HG_SUB, HEAD_W), F32)
    for s in range(HG_SUB):
        e = jnp.exp2(jnp.minimum(b3 - b3[:, s:s + 1, :], 0.0))
        col = jnp.sum(q3 * (k3[:, s:s + 1, :] * e), axis=-1, keepdims=True)
        o3 = o3 + jnp.where(trow >= s, col, 0.0) * v3[:, s:s + 1, :]
    return o + o3.reshape(c, HEAD_W)


def _hgrn_kernel(gam_ref, ng_ref, q_ref, f_ref, i_ref, g_ref, o_ref, st_ref, *, n_chunks, layer_idx):
    @pl.when(pl.program_id(2) == 0)
    def _():
        st_ref[...] = jnp.zeros_like(st_ref)

    gam = gam_ref[...]
    ge = jnp.exp(gam - jnp.max(gam, axis=0, keepdims=True))
    lb = jnp.sum(ge[0:layer_idx + 1], axis=0, keepdims=True) / jnp.sum(ge, axis=0, keepdims=True)
    ng = ng_ref[...]
    c = q_ref.shape[0] // n_chunks
    tri = (lax.broadcasted_iota(jnp.int32, (c, c), 0) >= lax.broadcasted_iota(jnp.int32, (c, c), 1))
    tri = jnp.where(tri, 1.0, 0.0).astype(BF16)

    def chunk(ci, _):
        r0 = pl.multiple_of(ci * c, c)
        qr = q_ref[pl.ds(r0, c), :].astype(F32)
        fr = f_ref[pl.ds(r0, c), :].astype(F32)
        v = i_ref[pl.ds(r0, c), :]
        gr = g_ref[pl.ds(r0, c), :].astype(F32)
        q = qr * _sigmoid(qr)
        f = lb + (1.0 - lb) * _sigmoid(fr)
        logf = jnp.log2(f)
        k = 1.0 - f
        hi = logf.astype(BF16)
        r1 = logf - hi.astype(F32)
        mid = r1.astype(BF16)
        lo = (r1 - mid.astype(F32)).astype(BF16)
        b = (jnp.dot(tri, hi, preferred_element_type=F32) + jnp.dot(tri, mid, preferred_element_type=F32)
             + jnp.dot(tri, lo, preferred_element_type=F32))
        st = st_ref[...]
        o = _nt_dot((q * jnp.exp2(b)).astype(BF16), st.astype(BF16))
        o = o + _hgrn_chunk(q, k, b, v)
        bend = b[c - 1:c]
        kd = (k * jnp.exp2(bend - b)).astype(BF16)
        st_ref[...] = st * jnp.exp2(bend) + _tn_dot(v, kd)
        ms = jnp.mean(o * o, axis=-1, keepdims=True)
        on = o * lax.rsqrt(ms + EPS) * ng * (gr * _sigmoid(gr))
        o_ref[pl.ds(r0, c), :] = on.astype(o_ref.dtype)
        return 0

    lax.fori_loop(0, n_chunks, chunk, 0)


def hgrn2(proj, gamma, norm_g, batch, seq, *, tc=512, layer_idx=0, col0=3 * DA_HEADS):
    n = proj.shape[0]
    tc = min(tc, seq)
    ng = seq // tc
    depth1 = gamma.shape[0]

    def spec(off):
        return pl.BlockSpec((tc, HEAD_W), lambda b, h, g: (b * ng + g, col0 + off * HG_HEADS + h))

    return pl.pallas_call(
        functools.partial(_hgrn_kernel, n_chunks=tc // min(HG_BLOCK, tc), layer_idx=layer_idx),
        grid=(batch, HG_HEADS, ng),
        in_specs=[pl.BlockSpec((depth1, HEAD_W), lambda b, h, g: (0, h)),
                  pl.BlockSpec((1, HEAD_W), lambda b, h, g: (0, 0)),
                  spec(0), spec(1), spec(2), spec(3)],
        out_specs=pl.BlockSpec((tc, HEAD_W), lambda b, h, g: (b * ng + g, h)),
        out_shape=jax.ShapeDtypeStruct((n, HG_HEADS * HEAD_W), BF16),
        scratch_shapes=[pltpu.VMEM((HEAD_W, HEAD_W), F32)],
        compiler_params=_cparams(("parallel", "parallel", "arbitrary")),
        name="hgrn2",
    )(gamma.astype(F32), norm_g.reshape(1, HEAD_W).astype(F32), proj, proj, proj, proj)


def _cross_attn_kernel(q_ref, k_ref, v_ref, o_ref, *, scale):
    q = q_ref[...] * jnp.asarray(scale, BF16)
    s = _nt_dot(q, k_ref[...])
    m = jnp.max(s, axis=-1, keepdims=True)
    p = jnp.exp(s - m)
    l = jnp.sum(p, axis=-1, keepdims=True)
    o = jnp.dot(p.astype(BF16), v_ref[...], preferred_element_type=F32) / l
    o_ref[...] = o.astype(o_ref.dtype)


def cross_attention(q, kv, batch, seq, mem_len, *, tq=512):
    n, d = q.shape
    hd = d // XA_HEADS
    tq = min(tq, seq)
    nq = seq // tq
    return pl.pallas_call(
        functools.partial(_cross_attn_kernel, scale=hd ** -0.5),
        grid=(batch, nq, XA_HEADS),
        in_specs=[pl.BlockSpec((tq, hd), lambda b, i, h: (b * nq + i, h)),
                  pl.BlockSpec((mem_len, hd), lambda b, i, h: (b, h)),
                  pl.BlockSpec((mem_len, hd), lambda b, i, h: (b, XA_HEADS + h))],
        out_specs=pl.BlockSpec((tq, hd), lambda b, i, h: (b * nq + i, h)),
        out_shape=jax.ShapeDtypeStruct((n, d), BF16),
        compiler_params=_cparams(("parallel", "parallel", "parallel")),
        name="cross_attention",
    )(q, kv, kv)


def _topk_desc(s, k):
    rows = s.shape[0]
    iota = lax.broadcasted_iota(jnp.int32, s.shape, 0)
    rank = jnp.full(s.shape, NOT_SELECTED, F32)
    vals = []
    cur = s
    for r in range(k):
        m = jnp.max(cur, axis=0, keepdims=True)
        first = jnp.min(jnp.where(cur == m, iota, rows), axis=0, keepdims=True)
        sel = iota == first
        rank = jnp.where(sel, float(r), rank)
        cur = jnp.where(sel, -jnp.inf, cur)
        vals.append(m)
    return jnp.concatenate(vals, axis=0), rank


def _peer_topk_kernel(q_ref, keys_ref, l1_ref, c_ref, r2_ref, e2_ref):
    kk = PEER_TOPK
    for h in range(PEER_HEADS):
        q1 = q_ref[:, (2 * h) * PEER_NKEYS:(2 * h + 1) * PEER_NKEYS]
        q2 = q_ref[:, (2 * h + 1) * PEER_NKEYS:(2 * h + 2) * PEER_NKEYS]
        s1 = _nt_dot(keys_ref[h, 0], q1)
        s2 = _nt_dot(keys_ref[h, 1], q2)
        sv1, rank1 = _topk_desc(s1, kk)
        sv2, rank2 = _topk_desc(s2, kk)
        t = s1.shape[1]
        segs, seg_rows = [], []
        for a in range(kk):
            nb = kk // (a + 1)
            if nb > 1:
                rows = -(-nb // SUBLANES) * SUBLANES
                seg = sv1[a:a + 1] + sv2[0:rows]
                if rows != nb:
                    seg = jnp.where(lax.broadcasted_iota(jnp.int32, seg.shape, 0) < nb, seg, -jnp.inf)
                segs.append(seg)
                seg_rows.append(rows)
        single0 = len(segs)
        n_single = kk - single0
        assert n_single % SUBLANES == 0
        segs.append(sv1[single0:kk] + sv2[0:1])
        cand = jnp.concatenate(segs, axis=0)
        iota = lax.broadcasted_iota(jnp.int32, cand.shape, 0)
        picked = jnp.zeros(cand.shape, F32)
        z = jnp.zeros((1, t), F32)
        cur = cand
        best0 = sv1[0:1] + sv2[0:1]
        for _ in range(kk):
            m = jnp.max(cur, axis=0, keepdims=True)
            first = jnp.min(jnp.where(cur == m, iota, cand.shape[0]), axis=0, keepdims=True)
            sel = iota == first
            cur = jnp.where(sel, -jnp.inf, cur)
            picked = jnp.where(sel, 1.0, picked)
            z = z + jnp.exp(m - best0)
        count, r0 = [], 0
        for rows in seg_rows:
            count.append(jnp.sum(picked[r0:r0 + rows], axis=0, keepdims=True))
            r0 += rows
        count += [picked[r0 + i:r0 + i + 1] for i in range(n_single)]
        l1 = jnp.zeros_like(rank1)
        for a in range(kk):
            l1 = jnp.where(rank1 == float(a), count[a], l1)
        sel1 = rank1 < float(kk)
        l1_ref[h] = l1
        c_ref[h] = jnp.where(sel1, jnp.exp(jnp.minimum(s1 - sv1[0:1], 0.0)) / z, 0.0)
        r2_ref[h] = rank2.astype(r2_ref.dtype)
        e2 = jnp.where(rank2 < float(kk), jnp.exp(jnp.minimum(s2 - sv2[0:1], 0.0)), 0.0)
        e2_ref[h] = e2.astype(e2_ref.dtype)


def peer_topk(q, keys, *, tt=256):
    n = q.shape[0]
    tt = min(tt, n)
    tab_spec = pl.BlockSpec((PEER_HEADS, PEER_NKEYS, tt), lambda i: (0, 0, i))
    return pl.pallas_call(
        _peer_topk_kernel,
        grid=(n // tt,),
        in_specs=[pl.BlockSpec((tt, q.shape[1]), lambda i: (i, 0)),
                  pl.BlockSpec(keys.shape, lambda i: (0, 0, 0, 0))],
        out_specs=[tab_spec] * 4,
        out_shape=[jax.ShapeDtypeStruct((PEER_HEADS, PEER_NKEYS, n), dt) for dt in (F32, F32, BF16, BF16)],
        compiler_params=_cparams(("parallel",)),
        name="peer_topk",
    )(q, keys)


def _gelu_exact(x):
    return 0.5 * x * (1.0 + lax.erf(x * (2.0 ** -0.5)))


def _peer_dense_kernel(hn_ref, down_ref, upt_ref, l1_ref, c_ref, r2_ref, e2_ref, o_ref, act_ref, wg_ref, *,
                       groups, nblk):
    s = pl.program_id(1)
    row0 = (jnp.clip(s - 1, 0, nblk - 1) * groups) % SUBLANES

    @pl.when(s == 0)
    def _():
        o_ref[...] = jnp.zeros_like(o_ref)
        act_ref[1] = jnp.zeros(act_ref.shape[1:], act_ref.dtype)
        wg_ref[1] = jnp.zeros(wg_ref.shape[1:], wg_ref.dtype)

    d, tt = o_ref.shape
    lane_chunk = 256
    units = [(gi, t0) for gi in range(groups) for t0 in range(0, tt, lane_chunk)]
    n_piece = len(units)
    kc, mc = d // n_piece, d // n_piece

    def gate_unit(cur, prv, gi, t0):
        rows = slice(gi * PEER_NKEYS, (gi + 1) * PEER_NKEYS)
        lanes = slice(t0, t0 + lane_chunk)
        g = None
        for h in range(PEER_HEADS):
            l1 = l1_ref[h, pl.ds(row0 + gi, 1), lanes].astype(BF16)
            cc = c_ref[h, pl.ds(row0 + gi, 1), lanes].astype(BF16)
            e2 = e2_ref[h, :, lanes]
            term = jnp.where(r2_ref[h, :, lanes] < l1, e2 * cc, jnp.zeros_like(e2))
            g = term if g is None else g + term
        wg_ref[cur, rows, lanes] = g * _gelu_exact(act_ref[prv, rows, lanes]).astype(BF16)

    def step(cur, prv):
        wg_prev = wg_ref[prv]
        for p in range(n_piece):
            ms = slice(p * mc, (p + 1) * mc)
            o_ref[ms, :] += jnp.dot(upt_ref[ms, :], wg_prev, preferred_element_type=F32)
            ks = slice(p * kc, (p + 1) * kc)
            part = _nt_dot(down_ref[:, ks], hn_ref[:, ks])
            if p == 0:
                act_ref[cur] = part
            else:
                act_ref[cur] += part
            gate_unit(cur, prv, *units[p])

    step(s % 2, (s + 1) % 2)


def peer_dense(hn, down, upt, tables, *, tt=512, te=512):
    n, d = hn.shape
    n_exp = down.shape[0]
    tt = min(tt, n)
    groups = te // PEER_NKEYS
    nblk = n_exp // te
    l1, c, r2, e2 = tables
    assert SUBLANES % groups == 0

    def blk(s, lag):
        return jnp.clip(s - lag, 0, nblk - 1)

    row_spec = pl.BlockSpec((PEER_HEADS, SUBLANES, tt), lambda i, s: (0, blk(s, 1) * groups // SUBLANES, i))
    tab_spec = pl.BlockSpec((PEER_HEADS, PEER_NKEYS, tt), lambda i, s: (0, 0, i))
    return pl.pallas_call(
        functools.partial(_peer_dense_kernel, groups=groups, nblk=nblk),
        grid=(n // tt, nblk + 2),
        in_specs=[pl.BlockSpec((tt, d), lambda i, s: (i, 0)),
                  pl.BlockSpec((te, d), lambda i, s: (blk(s, 0), 0)),
                  pl.BlockSpec((d, te), lambda i, s: (0, blk(s, 2))),
                  row_spec, row_spec, tab_spec, tab_spec],
        out_specs=pl.BlockSpec((d, tt), lambda i, s: (0, i)),
        out_shape=jax.ShapeDtypeStruct((d, n), F32),
        scratch_shapes=[pltpu.VMEM((2, te, tt), F32), pltpu.VMEM((2, te, tt), BF16)],
        compiler_params=_cparams(("parallel", "arbitrary")),
        name="peer_dense",
    )(hn, down, upt, l1, c, r2, e2)


def _final_norm_kernel(h_ref, pt_ref, g_ref, o_ref):
    x = h_ref[...] + pt_ref[...].T
    ms = jnp.mean(x * x, axis=-1, keepdims=True)
    o_ref[...] = x * lax.rsqrt(ms + EPS) * g_ref[...]


def final_norm(h, pt, g, *, tm=256):
    m, d = h.shape
    tm = min(tm, m)
    row = pl.BlockSpec((tm, d), lambda i: (i, 0))
    return pl.pallas_call(
        _final_norm_kernel,
        grid=(m // tm,),
        in_specs=[row, pl.BlockSpec((d, tm), lambda i: (0, i)), pl.BlockSpec((1, d), lambda i: (0, 0))],
        out_specs=row,
        out_shape=jax.ShapeDtypeStruct((m, d), F32),
        compiler_params=_cparams(("parallel",)),
        name="final_norm",
    )(h, pt, g.reshape(1, d).astype(F32))


def kernel(x, mem, mix_norm_g, w_in, da_lambda_q1, da_lambda_k1, da_lambda_q2, da_lambda_k2, da_subln_g, hgrn_gamma, hgrn_norm_g, w_out, cross_norm_g, mem_norm_g, w_cq, w_ckv, w_co, ffn_norm_g, peer_wq, peer_subkeys, peer_down, peer_up, final_norm_g):
    batch, seq, d = x.shape
    mem_len = mem.shape[1]
    depth = w_in.shape[0]
    assert depth == 1 and hgrn_gamma.shape[0] == depth + 1
    l = 0
    da_w = DA_HEADS * HEAD_W
    x2 = x.reshape(batch * seq, d)

    proj = norm_matmul(x2, mix_norm_g[l], w_in[l].astype(BF16))
    lam4 = jnp.stack([da_lambda_q1[l], da_lambda_k1[l], da_lambda_q2[l], da_lambda_k2[l]]).astype(F32)
    a_out = diff_attention(proj, lam4, da_subln_g[l], batch, seq, layer_idx=l)
    b_out = hgrn2(proj, hgrn_gamma, hgrn_norm_g[l], batch, seq, layer_idx=l)
    wo = w_out[l].astype(BF16)
    h1 = matmul_residual([a_out, b_out], [wo[:da_w], wo[da_w:]], x2)

    qc = norm_matmul(h1, cross_norm_g[l], w_cq[l].astype(BF16))
    kv = norm_matmul(mem.reshape(batch * mem_len, d), mem_norm_g[l], w_ckv[l].astype(BF16))
    oc = cross_attention(qc, kv, batch, seq, mem_len)
    h2 = matmul_residual([oc], [w_co[l].astype(BF16)], h1)

    qp, hn = norm_matmul(h2, ffn_norm_g[l], peer_wq[l].astype(BF16), tn=512, emit_xn=True)
    tables = peer_topk(qp, peer_subkeys[l].astype(BF16))
    peer_t = peer_dense(hn, peer_down[l].astype(BF16), peer_up[l].T.astype(BF16), tables)
    out = final_norm(h2, peer_t, final_norm_g)
    return out.reshape(batch, seq, d)
```

```python
import functools
import math

import numpy as np
import jax
import jax.numpy as jnp
from jax import lax
from jax.experimental import pallas as pl
from jax.experimental.pallas import tpu as pltpu

F32 = jnp.float32
BF16 = jnp.bfloat16
EPS = 1e-6
NEG_BIG = -1e30

DA_HEADS = 16
DA_HALF = 64
HEAD_W = 128
HG_HEADS = 16
HG_BLOCK = 256
HG_SUB = 8
XA_HEADS = 4
PEER_HEADS = 8
PEER_NKEYS = 128
PEER_TOPK = 16
NOT_SELECTED = 99.0
SUBLANES = 8

VMEM_LIMIT = 56 * 1024 * 1024


def _cparams(sem):
    return pltpu.CompilerParams(dimension_semantics=sem, vmem_limit_bytes=VMEM_LIMIT)


def _nt_dot(a, b):
    return lax.dot_general(a, b, (((1,), (1,)), ((), ())), preferred_element_type=F32)


def _tn_dot(a, b):
    return lax.dot_general(a, b, (((0,), (0,)), ((), ())), preferred_element_type=F32)


def _sigmoid(x):
    return 0.5 * jnp.tanh(0.5 * x) + 0.5


def _norm_matmul_kernel(x_ref, g_ref, w_ref, o_ref, xn_ref, *, rows):
    @pl.when(pl.program_id(1) == 0)
    def _():
        g = g_ref[...]

        def chunk(ci, _):
            r0 = pl.multiple_of(ci * rows, rows)
            x = x_ref[pl.ds(r0, rows), :]
            ms = jnp.mean(x * x, axis=-1, keepdims=True)
            xn_ref[pl.ds(r0, rows), :] = (x * lax.rsqrt(ms + EPS) * g).astype(BF16)
            return 0

        lax.fori_loop(0, x_ref.shape[0] // rows, chunk, 0)

    o_ref[...] = jnp.dot(xn_ref[...], w_ref[...], preferred_element_type=F32).astype(o_ref.dtype)


def norm_matmul(x, g, w, *, tm=512, tn=1024, emit_xn=False):
    m, k = x.shape
    n = w.shape[1]
    tm, tn = min(tm, m), min(tn, n)
    assert m % tm == 0 and n % tn == 0
    rows = min(128, tm)
    out_shape = [jax.ShapeDtypeStruct((m, n), BF16)]
    out_specs = [pl.BlockSpec((tm, tn), lambda i, j: (i, j))]
    scratch = [pltpu.VMEM((tm, k), BF16)]
    if emit_xn:
        out_shape.append(jax.ShapeDtypeStruct((m, k), BF16))
        out_specs.append(pl.BlockSpec((tm, k), lambda i, j: (i, 0)))
        scratch = []
    res = pl.pallas_call(
        functools.partial(_norm_matmul_kernel, rows=rows),
        grid=(m // tm, n // tn),
        in_specs=[pl.BlockSpec((tm, k), lambda i, j: (i, 0)),
                  pl.BlockSpec((1, k), lambda i, j: (0, 0)),
                  pl.BlockSpec((k, tn), lambda i, j: (0, j))],
        out_specs=out_specs,
        out_shape=out_shape,
        scratch_shapes=scratch,
        compiler_params=_cparams(("parallel", "arbitrary")),
        name="norm_matmul",
    )(x, g.reshape(1, k).astype(F32), w)
    return res if emit_xn else res[0]


def _matmul_residual_kernel(*refs, n_lhs):
    lhs = refs[:n_lhs]
    ws = refs[n_lhs:2 * n_lhs]
    res_ref, o_ref = refs[2 * n_lhs], refs[2 * n_lhs + 1]
    acc = res_ref[...]
    for a_ref, w_ref in zip(lhs, ws):
        acc = acc + jnp.dot(a_ref[...], w_ref[...], preferred_element_type=F32)
    o_ref[...] = acc


def matmul_residual(lhs_list, w_list, res, *, tm=512, tn=1024):
    m, n = res.shape
    tm, tn = min(tm, m), min(tn, n)
    assert m % tm == 0 and n % tn == 0
    n_lhs = len(lhs_list)
    in_specs = [pl.BlockSpec((tm, a.shape[1]), lambda i, j: (i, 0)) for a in lhs_list]
    in_specs += [pl.BlockSpec((w.shape[0], tn), lambda i, j: (0, j)) for w in w_list]
    in_specs += [pl.BlockSpec((tm, tn), lambda i, j: (i, j))]
    return pl.pallas_call(
        functools.partial(_matmul_residual_kernel, n_lhs=n_lhs),
        grid=(m // tm, n // tn),
        in_specs=in_specs,
        out_specs=pl.BlockSpec((tm, tn), lambda i, j: (i, j)),
        out_shape=jax.ShapeDtypeStruct((m, n), F32),
        compiler_params=_cparams(("parallel", "parallel")),
        name="matmul_residual",
    )(*lhs_list, *w_list, res)


def _diff_attn_kernel(lam_ref, subg_ref, slope_ref, q_ref, k_ref, vt_ref, o_ref, base_ref, acc_ref, *,
                      tq, tc, lam_init):
    qi = pl.program_id(2)
    ncol = 2 * tq
    per_q = tq // tc
    log2e = math.log2(math.e)
    q = (q_ref[...].astype(F32) * (DA_HALF ** -0.5 * log2e)).astype(BF16)
    lane = lax.broadcasted_iota(jnp.int32, q.shape, 1)
    zero = jnp.zeros_like(q)
    q2 = jnp.concatenate([jnp.where(lane < DA_HALF, q, zero), jnp.where(lane >= DA_HALF, q, zero)], axis=0)
    slope = slope_ref[0:1, 0:1] * log2e
    krow = lax.broadcasted_iota(jnp.int32, (tc, ncol), 0)
    base_ref[...] = slope * krow.astype(F32)
    acc_ref[...] = jnp.zeros_like(acc_ref)

    def process(j, m, masked):
        rel = j * tc - qi * tq
        off = slope * rel.astype(F32)
        kj = k_ref[pl.ds(pl.multiple_of(j * tc, tc), tc), :]
        s = _nt_dot(kj, q2) + base_ref[...]
        if masked:
            qpos = lax.broadcasted_iota(jnp.int32, (tc, ncol), 1) % tq
            s = jnp.where(krow + rel <= qpos, s, NEG_BIG)
        m_new = jnp.maximum(m, jnp.max(s, axis=0, keepdims=True) + off)
        p = jnp.exp2((s - (m_new - off)).astype(BF16))
        alpha = jnp.exp2(m - m_new)
        acc_ref[...] = acc_ref[...] * alpha + jnp.dot(vt_ref[j], p, preferred_element_type=F32)
        return m_new

    m = jnp.full((1, ncol), NEG_BIG, F32)
    m = lax.fori_loop(0, per_q * qi, lambda j, mm: process(j, mm, False), m)
    for u in range(per_q):
        m = process(per_q * qi + u, m, True)

    lv = lam_ref[...]
    lam = (jnp.exp(jnp.sum(lv[0:1] * lv[1:2], axis=-1, keepdims=True))
           - jnp.exp(jnp.sum(lv[2:3] * lv[3:4], axis=-1, keepdims=True)) + lam_init)
    on = acc_ref[0:HEAD_W, :] / acc_ref[HEAD_W:HEAD_W + 1, :]
    o = on[:, :tq] - lam * on[:, tq:]
    ms = jnp.mean(o * o, axis=0, keepdims=True)
    o = o * lax.rsqrt(ms + EPS) * subg_ref[...] * (1.0 - lam_init)
    o_ref[...] = o.T.astype(o_ref.dtype)


def diff_attention(proj, lam4, subln_g, batch, seq, *, tq=512, tc=512, layer_idx=0):
    n = proj.shape[0]
    tq = min(tq, seq)
    tc = min(tc, tq)
    nq, nc = seq // tq, seq // tc
    lam_init = 0.8 - 0.6 * math.exp(-0.3 * layer_idx)
    slopes = np.array([2.0 ** (-8.0 * (h + 1) / DA_HEADS) for h in range(DA_HEADS)], dtype=np.float32)
    slope_tab = jnp.asarray(np.broadcast_to(slopes[:, None, None], (DA_HEADS, 8, 128)).copy())
    da_w = DA_HEADS * HEAD_W
    vt5 = proj[:, 2 * da_w:3 * da_w].reshape(batch, nc, tc, DA_HEADS, HEAD_W).transpose(0, 3, 1, 4, 2)
    ones_tile = jnp.zeros((2 * SUBLANES, tc), BF16).at[0].set(1)
    vt5 = jnp.concatenate([vt5, jnp.broadcast_to(ones_tile, vt5.shape[:3] + ones_tile.shape)], axis=3)
    vrows = HEAD_W + 2 * SUBLANES
    return pl.pallas_call(
        functools.partial(_diff_attn_kernel, tq=tq, tc=tc, lam_init=lam_init),
        grid=(batch, DA_HEADS, nq),
        in_specs=[pl.BlockSpec((4, DA_HALF), lambda b, h, i: (0, 0)),
                  pl.BlockSpec((HEAD_W, 1), lambda b, h, i: (0, 0)),
                  pl.BlockSpec((None, 8, 128), lambda b, h, i: (h, 0, 0)),
                  pl.BlockSpec((tq, HEAD_W), lambda b, h, i: (b * nq + i, h)),
                  pl.BlockSpec((seq, HEAD_W), lambda b, h, i: (b, DA_HEADS + h)),
                  pl.BlockSpec((None, None, nc, vrows, tc), lambda b, h, i: (b, h, 0, 0, 0))],
        out_specs=pl.BlockSpec((tq, HEAD_W), lambda b, h, i: (b * nq + i, h)),
        out_shape=jax.ShapeDtypeStruct((n, da_w), BF16),
        scratch_shapes=[pltpu.VMEM((tc, 2 * tq), F32), pltpu.VMEM((vrows, 2 * tq), F32)],
        compiler_params=_cparams(("parallel", "parallel", "arbitrary")),
        name="diff_attention",
    )(lam4, subln_g.reshape(HEAD_W, 1).astype(F32), slope_tab, proj, proj, vt5)


def _hgrn_chunk(q, k, b, v):
    c = q.shape[0]
    row = lax.broadcasted_iota(jnp.int32, (c, 1), 0)
    arow = lax.broadcasted_iota(jnp.int32, (c, c), 0)
    acol = lax.broadcasted_iota(jnp.int32, (c, c), 1)

    def block_level(half):
        grp = 2 * half
        bg = b.reshape(c // grp, grp, HEAD_W)
        ref = jnp.broadcast_to(bg[:, half - 1:half, :], bg.shape).reshape(c, HEAD_W)
        upper = (row % grp) >= half
        qs = jnp.where(upper, q * jnp.exp2(jnp.minimum(b - ref, 0.0)), 0.0)
        ks = jnp.where(upper, 0.0, k * jnp.exp2(jnp.minimum(ref - b, 0.0)))
        a = _nt_dot(qs.astype(BF16), ks.astype(BF16))
        keep = ((arow // grp) == (acol // grp)) & ((arow % grp) >= half) & ((acol % grp) < half)
        return jnp.where(keep, a, 0.0)

    a_tot = None
    half = c // 2
    while half >= HG_SUB:
        lvl = block_level(half)
        a_tot = lvl if a_tot is None else a_tot + lvl
        half //= 2
    o = jnp.dot(a_tot.astype(BF16), v, preferred_element_type=F32)

    nsub = c // HG_SUB
    q3 = q.reshape(nsub, HG_SUB, HEAD_W)
    k3 = k.reshape(nsub, HG_SUB, HEAD_W)
    b3 = b.reshape(nsub, HG_SUB, HEAD_W)
    v3 = v.astype(F32).reshape(nsub, HG_SUB, HEAD_W)
    trow = lax.broadcasted_iota(jnp.int32, (nsub, HG_SUB, 1), 1)
    o3 = jnp.zeros((nsub, HG_SUB, HEAD_W), F32)
    for s in range(HG_SUB):
        e = jnp.exp2(jnp.minimum(b3 - b3[:, s:s + 1, :], 0.0))
        col = jnp.sum(q3 * (k3[:, s:s + 1, :] * e), axis=-1, keepdims=True)
        o3 = o3 + jnp.where(trow >= s, col, 0.0) * v3[:, s:s + 1, :]
    return o + o3.reshape(c, HEAD_W)


def _hgrn_kernel(gam_ref, ng_ref, q_ref, f_ref, i_ref, g_ref, o_ref, st_ref, *, n_chunks, layer_idx):
    @pl.when(pl.program_id(2) == 0)
    def _():
        st_ref[...] = jnp.zeros_like(st_ref)

    gam = gam_ref[...]
    ge = jnp.exp(gam - jnp.max(gam, axis=0, keepdims=True))
    lb = jnp.sum(ge[0:layer_idx + 1], axis=0, keepdims=True) / jnp.sum(ge, axis=0, keepdims=True)
    ng = ng_ref[...]
    c = q_ref.shape[0] // n_chunks
    tri = (lax.broadcasted_iota(jnp.int32, (c, c), 0) >= lax.broadcasted_iota(jnp.int32, (c, c), 1))
    tri = jnp.where(tri, 1.0, 0.0).astype(BF16)

    def chunk(ci, _):
        r0 = pl.multiple_of(ci * c, c)
        qr = q_ref[pl.ds(r0, c), :].astype(F32)
        fr = f_ref[pl.ds(r0, c), :].astype(F32)
        v = i_ref[pl.ds(r0, c), :]
        gr = g_ref[pl.ds(r0, c), :].astype(F32)
        q = qr * _sigmoid(qr)
        f = lb + (1.0 - lb) * _sigmoid(fr)
        logf = jnp.log2(f)
        k = 1.0 - f
        hi = logf.astype(BF16)
        r1 = logf - hi.astype(F32)
        mid = r1.astype(BF16)
        lo = (r1 - mid.astype(F32)).astype(BF16)
        b = (jnp.dot(tri, hi, preferred_element_type=F32) + jnp.dot(tri, mid, preferred_element_type=F32)
             + jnp.dot(tri, lo, preferred_element_type=F32))
        st = st_ref[...]
        o = _nt_dot((q * jnp.exp2(b)).astype(BF16), st.astype(BF16))
        o = o + _hgrn_chunk(q, k, b, v)
        bend = b[c - 1:c]
        kd = (k * jnp.exp2(bend - b)).astype(BF16)
        st_ref[...] = st * jnp.exp2(bend) + _tn_dot(v, kd)
        ms = jnp.mean(o * o, axis=-1, keepdims=True)
        on = o * lax.rsqrt(ms + EPS) * ng * (gr * _sigmoid(gr))
        o_ref[pl.ds(r0, c), :] = on.astype(o_ref.dtype)
        return 0

    lax.fori_loop(0, n_chunks, chunk, 0)


def hgrn2(proj, gamma, norm_g, batch, seq, *, tc=512, layer_idx=0, col0=3 * DA_HEADS):
    n = proj.shape[0]
    tc = min(tc, seq)
    ng = seq // tc
    depth1 = gamma.shape[0]

    def spec(off):
        return pl.BlockSpec((tc, HEAD_W), lambda b, h, g: (b * ng + g, col0 + off * HG_HEADS + h))

    return pl.pallas_call(
        functools.partial(_hgrn_kernel, n_chunks=tc // min(HG_BLOCK, tc), layer_idx=layer_idx),
        grid=(batch, HG_HEADS, ng),
        in_specs=[pl.BlockSpec((depth1, HEAD_W), lambda b, h, g: (0, h)),
                  pl.BlockSpec((1, HEAD_W), lambda b, h, g: (0, 0)),
                  spec(0), spec(1), spec(2), spec(3)],
        out_specs=pl.BlockSpec((tc, HEAD_W), lambda b, h, g: (b * ng + g, h)),
        out_shape=jax.ShapeDtypeStruct((n, HG_HEADS * HEAD_W), BF16),
        scratch_shapes=[pltpu.VMEM((HEAD_W, HEAD_W), F32)],
        compiler_params=_cparams(("parallel", "parallel", "arbitrary")),
        name="hgrn2",
    )(gamma.astype(F32), norm_g.reshape(1, HEAD_W).astype(F32), proj, proj, proj, proj)


def _cross_attn_kernel(q_ref, k_ref, v_ref, o_ref, *, scale):
    q = q_ref[...] * jnp.asarray(scale, BF16)
    s = _nt_dot(q, k_ref[...])
    m = jnp.max(s, axis=-1, keepdims=True)
    p = jnp.exp(s - m)
    l = jnp.sum(p, axis=-1, keepdims=True)
    o = jnp.dot(p.astype(BF16), v_ref[...], preferred_element_type=F32) / l
    o_ref[...] = o.astype(o_ref.dtype)


def cross_attention(q, kv, batch, seq, mem_len, *, tq=512):
    n, d = q.shape
    hd = d // XA_HEADS
    tq = min(tq, seq)
    nq = seq // tq
    return pl.pallas_call(
        functools.partial(_cross_attn_kernel, scale=hd ** -0.5),
        grid=(batch, nq, XA_HEADS),
        in_specs=[pl.BlockSpec((tq, hd), lambda b, i, h: (b * nq + i, h)),
                  pl.BlockSpec((mem_len, hd), lambda b, i, h: (b, h)),
                  pl.BlockSpec((mem_len, hd), lambda b, i, h: (b, XA_HEADS + h))],
        out_specs=pl.BlockSpec((tq, hd), lambda b, i, h: (b * nq + i, h)),
        out_shape=jax.ShapeDtypeStruct((n, d), BF16),
        compiler_params=_cparams(("parallel", "parallel", "parallel")),
        name="cross_attention",
    )(q, kv, kv)


def _topk_desc(s, k):
    rows = s.shape[0]

    def extract(first_only):
        iota = lax.broadcasted_iota(jnp.int32, s.shape, 0)
        rank = jnp.full(s.shape, NOT_SELECTED, F32)
        vals = []
        cur = s
        for r in range(k):
            m = jnp.max(cur, axis=0, keepdims=True)
            sel = cur == m
            if first_only:
                sel = iota == jnp.min(jnp.where(sel, iota, rows), axis=0, keepdims=True)
            rank = jnp.where(sel, float(r), rank)
            cur = jnp.where(sel, -jnp.inf, cur)
            vals.append(m)
        return jnp.concatenate(vals, axis=0), rank, cur

    sv, rank, cur = extract(False)
    removed = jnp.sum(jnp.where(cur == -jnp.inf, 1.0, 0.0), axis=0, keepdims=True)
    return lax.cond(jnp.max(removed) == float(k), lambda: (sv, rank), lambda: extract(True)[:2])


def _peer_topk_kernel(q_ref, keys_ref, l1_ref, c_ref, r2_ref, e2_ref):
    kk = PEER_TOPK
    for h in range(PEER_HEADS):
        q1 = q_ref[:, (2 * h) * PEER_NKEYS:(2 * h + 1) * PEER_NKEYS]
        q2 = q_ref[:, (2 * h + 1) * PEER_NKEYS:(2 * h + 2) * PEER_NKEYS]
        s1 = _nt_dot(keys_ref[h, 0], q1)
        s2 = _nt_dot(keys_ref[h, 1], q2)
        t = s1.shape[1]
        sv12, rank12 = _topk_desc(jnp.concatenate([s1, s2], axis=1), kk)
        sv1, sv2, rank1, rank2 = sv12[:, :t], sv12[:, t:], rank12[:, :t], rank12[:, t:]
        t = s1.shape[1]
        segs, seg_rows = [], []
        for a in range(kk):
            nb = kk // (a + 1)
            if nb > 1:
                rows = -(-nb // SUBLANES) * SUBLANES
                seg = sv1[a:a + 1] + sv2[0:rows]
                if rows != nb:
                    seg = jnp.where(lax.broadcasted_iota(jnp.int32, seg.shape, 0) < nb, seg, -jnp.inf)
                segs.append(seg)
                seg_rows.append(rows)
        single0 = len(segs)
        n_single = kk - single0
        assert n_single % SUBLANES == 0
        segs.append(sv1[single0:kk] + sv2[0:1])
        cand = jnp.concatenate(segs, axis=0)
        best0 = sv1[0:1] + sv2[0:1]

        def pick(first_only, cand=cand, best0=best0):
            iota = lax.broadcasted_iota(jnp.int32, cand.shape, 0)
            picked = jnp.zeros(cand.shape, F32)
            z = jnp.zeros((1, t), F32)
            cur = cand
            for _ in range(kk):
                m = jnp.max(cur, axis=0, keepdims=True)
                sel = cur == m
                if first_only:
                    sel = iota == jnp.min(jnp.where(sel, iota, cand.shape[0]), axis=0, keepdims=True)
                cur = jnp.where(sel, -jnp.inf, cur)
                picked = jnp.where(sel, 1.0, picked)
                z = z + jnp.exp(m - best0)
            return picked, z

        picked, z = pick(False)
        n_picked = jnp.sum(picked, axis=0, keepdims=True)
        picked, z = lax.cond(jnp.max(n_picked) == float(kk), lambda p=picked, zz=z: (p, zz),
                             lambda: pick(True))
        count, r0 = [], 0
        for rows in seg_rows:
            count.append(jnp.sum(picked[r0:r0 + rows], axis=0, keepdims=True))
            r0 += rows
        count += [picked[r0 + i:r0 + i + 1] for i in range(n_single)]
        l1 = jnp.zeros_like(rank1)
        for a in range(kk):
            l1 = jnp.where(rank1 == float(a), count[a], l1)
        sel1 = rank1 < float(kk)
        l1_ref[h] = l1
        c_ref[h] = jnp.where(sel1, jnp.exp(jnp.minimum(s1 - sv1[0:1], 0.0)) / z, 0.0)
        r2_ref[h] = rank2.astype(r2_ref.dtype)
        e2 = jnp.where(rank2 < float(kk), jnp.exp(jnp.minimum(s2 - sv2[0:1], 0.0)), 0.0)
        e2_ref[h] = e2.astype(e2_ref.dtype)


def peer_topk(q, keys, *, tt=256):
    n = q.shape[0]
    tt = min(tt, n)
    tab_spec = pl.BlockSpec((PEER_HEADS, PEER_NKEYS, tt), lambda i: (0, 0, i))
    return pl.pallas_call(
        _peer_topk_kernel,
        grid=(n // tt,),
        in_specs=[pl.BlockSpec((tt, q.shape[1]), lambda i: (i, 0)),
                  pl.BlockSpec(keys.shape, lambda i: (0, 0, 0, 0))],
        out_specs=[tab_spec] * 4,
        out_shape=[jax.ShapeDtypeStruct((PEER_HEADS, PEER_NKEYS, n), dt) for dt in (F32, F32, BF16, BF16)],
        compiler_params=_cparams(("parallel",)),
        name="peer_topk",
    )(q, keys)


def _gelu_exact(x):
    return 0.5 * x * (1.0 + lax.erf(x * (2.0 ** -0.5)))


def _peer_dense_kernel(hn_ref, down_ref, up_ref, l1_ref, c_ref, r2_ref, e2_ref, o_ref, act_ref, wg_ref, *,
                       groups, nblk):
    s = pl.program_id(1)
    row0 = (jnp.clip(s - 1, 0, nblk - 1) * groups) % SUBLANES

    @pl.when(s == 0)
    def _():
        o_ref[...] = jnp.zeros_like(o_ref)
        act_ref[1] = jnp.zeros(act_ref.shape[1:], act_ref.dtype)
        wg_ref[1] = jnp.zeros(wg_ref.shape[1:], wg_ref.dtype)

    d, tt = o_ref.shape
    lane_chunk = 256
    units = [(gi, t0) for gi in range(groups) for t0 in range(0, tt, lane_chunk)]
    n_piece = len(units)
    kc, mc = d // n_piece, d // n_piece

    def gate_unit(cur, prv, gi, t0):
        rows = slice(gi * PEER_NKEYS, (gi + 1) * PEER_NKEYS)
        lanes = slice(t0, t0 + lane_chunk)
        g = None
        for h in range(PEER_HEADS):
            l1 = l1_ref[h, pl.ds(row0 + gi, 1), lanes].astype(BF16)
            cc = c_ref[h, pl.ds(row0 + gi, 1), lanes].astype(BF16)
            e2 = e2_ref[h, :, lanes]
            term = jnp.where(r2_ref[h, :, lanes] < l1, e2 * cc, jnp.zeros_like(e2))
            g = term if g is None else g + term
        wg_ref[cur, rows, lanes] = g * _gelu_exact(act_ref[prv, rows, lanes]).astype(BF16)

    def step(cur, prv):
        wg_prev = wg_ref[prv]
        for p in range(n_piece):
            ms = slice(p * mc, (p + 1) * mc)
            o_ref[ms, :] += _tn_dot(up_ref[:, ms], wg_prev)
            ks = slice(p * kc, (p + 1) * kc)
            part = _nt_dot(down_ref[:, ks], hn_ref[:, ks])
            if p == 0:
                act_ref[cur] = part
            else:
                act_ref[cur] += part
            gate_unit(cur, prv, *units[p])

    step(s % 2, (s + 1) % 2)


def peer_dense(hn, down, up, tables, *, tt=512, te=512):
    n, d = hn.shape
    n_exp = down.shape[0]
    tt = min(tt, n)
    groups = te // PEER_NKEYS
    nblk = n_exp // te
    l1, c, r2, e2 = tables
    assert SUBLANES % groups == 0

    def blk(s, lag):
        return jnp.clip(s - lag, 0, nblk - 1)

    row_spec = pl.BlockSpec((PEER_HEADS, SUBLANES, tt), lambda i, s: (0, blk(s, 1) * groups // SUBLANES, i))
    tab_spec = pl.BlockSpec((PEER_HEADS, PEER_NKEYS, tt), lambda i, s: (0, 0, i))
    return pl.pallas_call(
        functools.partial(_peer_dense_kernel, groups=groups, nblk=nblk),
        grid=(n // tt, nblk + 2),
        in_specs=[pl.BlockSpec((tt, d), lambda i, s: (i, 0)),
                  pl.BlockSpec((te, d), lambda i, s: (blk(s, 0), 0)),
                  pl.BlockSpec((te, d), lambda i, s: (blk(s, 2), 0)),
                  row_spec, row_spec, tab_spec, tab_spec],
        out_specs=pl.BlockSpec((d, tt), lambda i, s: (0, i)),
        out_shape=jax.ShapeDtypeStruct((d, n), F32),
        scratch_shapes=[pltpu.VMEM((2, te, tt), F32), pltpu.VMEM((2, te, tt), BF16)],
        compiler_params=_cparams(("parallel", "arbitrary")),
        name="peer_dense",
    )(hn, down, up, l1, c, r2, e2)


def _final_norm_kernel(h_ref, pt_ref, g_ref, o_ref):
    x = h_ref[...] + pt_ref[...].T
    ms = jnp.mean(x * x, axis=-1, keepdims=True)
    o_ref[...] = x * lax.rsqrt(ms + EPS) * g_ref[...]


def final_norm(h, pt, g, *, tm=256):
    m, d = h.shape
    tm = min(tm, m)
    row = pl.BlockSpec((tm, d), lambda i: (i, 0))
    return pl.pallas_call(
        _final_norm_kernel,
        grid=(m // tm,),
        in_specs=[row, pl.BlockSpec((d, tm), lambda i: (0, i)), pl.BlockSpec((1, d), lambda i: (0, 0))],
        out_specs=row,
        out_shape=jax.ShapeDtypeStruct((m, d), F32),
        compiler_params=_cparams(("parallel",)),
        name="final_norm",
    )(h, pt, g.reshape(1, d).astype(F32))


def kernel(x, mem, mix_norm_g, w_in, da_lambda_q1, da_lambda_k1, da_lambda_q2, da_lambda_k2, da_subln_g, hgrn_gamma, hgrn_norm_g, w_out, cross_norm_g, mem_norm_g, w_cq, w_ckv, w_co, ffn_norm_g, peer_wq, peer_subkeys, peer_down, peer_up, final_norm_g):
    batch, seq, d = x.shape
    mem_len = mem.shape[1]
    depth = w_in.shape[0]
    assert depth == 1 and hgrn_gamma.shape[0] == depth + 1
    l = 0
    da_w = DA_HEADS * HEAD_W
    x2 = x.reshape(batch * seq, d)

    proj = norm_matmul(x2, mix_norm_g[l], w_in[l].astype(BF16))
    lam4 = jnp.stack([da_lambda_q1[l], da_lambda_k1[l], da_lambda_q2[l], da_lambda_k2[l]]).astype(F32)
    a_out = diff_attention(proj, lam4, da_subln_g[l], batch, seq, layer_idx=l)
    b_out = hgrn2(proj, hgrn_gamma, hgrn_norm_g[l], batch, seq, layer_idx=l)
    wo = w_out[l].astype(BF16)
    h1 = matmul_residual([a_out, b_out], [wo[:da_w], wo[da_w:]], x2)

    qc = norm_matmul(h1, cross_norm_g[l], w_cq[l].astype(BF16))
    kv = norm_matmul(mem.reshape(batch * mem_len, d), mem_norm_g[l], w_ckv[l].astype(BF16))
    oc = cross_attention(qc, kv, batch, seq, mem_len)
    h2 = matmul_residual([oc], [w_co[l].astype(BF16)], h1)

    qp, hn = norm_matmul(h2, ffn_norm_g[l], peer_wq[l].astype(BF16), tn=512, emit_xn=True)
    tables = peer_topk(qp, peer_subkeys[l].astype(BF16))
    peer_t = peer_dense(hn, peer_down[l].astype(BF16), peer_up[l].astype(BF16), tables)
    out = final_norm(h2, peer_t, final_norm_g)
    return out.reshape(batch, seq, d)
```

```python
import functools
import math

import numpy as np
import jax
import jax.numpy as jnp
from jax import lax
from jax.experimental import pallas as pl
from jax.experimental.pallas import tpu as pltpu

F32 = jnp.float32
BF16 = jnp.bfloat16
EPS = 1e-6
NEG_BIG = -1e30

DA_HEADS = 16
DA_HALF = 64
HEAD_W = 128
HG_HEADS = 16
HG_BLOCK = 256
HG_SUB = 8
XA_HEADS = 4
PEER_HEADS = 8
PEER_NKEYS = 128
PEER_TOPK = 16
NOT_SELECTED = 99.0
SUBLANES = 8

VMEM_LIMIT = 56 * 1024 * 1024


def _cparams(sem):
    return pltpu.CompilerParams(dimension_semantics=sem, vmem_limit_bytes=VMEM_LIMIT)


def _nt_dot(a, b):
    return lax.dot_general(a, b, (((1,), (1,)), ((), ())), preferred_element_type=F32)


def _tn_dot(a, b):
    return lax.dot_general(a, b, (((0,), (0,)), ((), ())), preferred_element_type=F32)


def _sigmoid(x):
    return 0.5 * jnp.tanh(0.5 * x) + 0.5


def _norm_matmul_kernel(x_ref, g_ref, w_ref, o_ref, xn_ref, *, rows):
    @pl.when(pl.program_id(1) == 0)
    def _():
        g = g_ref[...]

        def chunk(ci, _):
            r0 = pl.multiple_of(ci * rows, rows)
            x = x_ref[pl.ds(r0, rows), :]
            ms = jnp.mean(x * x, axis=-1, keepdims=True)
            xn_ref[pl.ds(r0, rows), :] = (x * lax.rsqrt(ms + EPS) * g).astype(BF16)
            return 0

        lax.fori_loop(0, x_ref.shape[0] // rows, chunk, 0)

    o_ref[...] = jnp.dot(xn_ref[...], w_ref[...], preferred_element_type=F32).astype(o_ref.dtype)


def norm_matmul(x, g, w, *, tm=512, tn=1024, emit_xn=False):
    m, k = x.shape
    n = w.shape[1]
    tm, tn = min(tm, m), min(tn, n)
    assert m % tm == 0 and n % tn == 0
    rows = min(128, tm)
    out_shape = [jax.ShapeDtypeStruct((m, n), BF16)]
    out_specs = [pl.BlockSpec((tm, tn), lambda i, j: (i, j))]
    scratch = [pltpu.VMEM((tm, k), BF16)]
    if emit_xn:
        out_shape.append(jax.ShapeDtypeStruct((m, k), BF16))
        out_specs.append(pl.BlockSpec((tm, k), lambda i, j: (i, 0)))
        scratch = []
    res = pl.pallas_call(
        functools.partial(_norm_matmul_kernel, rows=rows),
        grid=(m // tm, n // tn),
        in_specs=[pl.BlockSpec((tm, k), lambda i, j: (i, 0)),
                  pl.BlockSpec((1, k), lambda i, j: (0, 0)),
                  pl.BlockSpec((k, tn), lambda i, j: (0, j))],
        out_specs=out_specs,
        out_shape=out_shape,
        scratch_shapes=scratch,
        compiler_params=_cparams(("parallel", "arbitrary")),
        name="norm_matmul",
    )(x, g.reshape(1, k).astype(F32), w)
    return res if emit_xn else res[0]


def _matmul_residual_kernel(*refs, n_lhs):
    lhs = refs[:n_lhs]
    ws = refs[n_lhs:2 * n_lhs]
    res_ref, o_ref = refs[2 * n_lhs], refs[2 * n_lhs + 1]
    acc = res_ref[...]
    for a_ref, w_ref in zip(lhs, ws):
        acc = acc + jnp.dot(a_ref[...], w_ref[...], preferred_element_type=F32)
    o_ref[...] = acc


def matmul_residual(lhs_list, w_list, res, *, tm=512, tn=1024):
    m, n = res.shape
    tm, tn = min(tm, m), min(tn, n)
    assert m % tm == 0 and n % tn == 0
    n_lhs = len(lhs_list)
    in_specs = [pl.BlockSpec((tm, a.shape[1]), lambda i, j: (i, 0)) for a in lhs_list]
    in_specs += [pl.BlockSpec((w.shape[0], tn), lambda i, j: (0, j)) for w in w_list]
    in_specs += [pl.BlockSpec((tm, tn), lambda i, j: (i, j))]
    return pl.pallas_call(
        functools.partial(_matmul_residual_kernel, n_lhs=n_lhs),
        grid=(m // tm, n // tn),
        in_specs=in_specs,
        out_specs=pl.BlockSpec((tm, tn), lambda i, j: (i, j)),
        out_shape=jax.ShapeDtypeStruct((m, n), F32),
        compiler_params=_cparams(("parallel", "parallel")),
        name="matmul_residual",
    )(*lhs_list, *w_list, res)


def _diff_attn_kernel(lam_ref, subg_ref, slope_ref, q_ref, k_ref, vt_ref, o_ref, base_ref, acc_ref, *,
                      tq, tc, lam_init):
    qi = pl.program_id(2)
    ncol = 2 * tq
    per_q = tq // tc
    log2e = math.log2(math.e)
    q = (q_ref[...].astype(F32) * (DA_HALF ** -0.5 * log2e)).astype(BF16)
    lane = lax.broadcasted_iota(jnp.int32, q.shape, 1)
    zero = jnp.zeros_like(q)
    q2 = jnp.concatenate([jnp.where(lane < DA_HALF, q, zero), jnp.where(lane >= DA_HALF, q, zero)], axis=0)
    slope = slope_ref[0:1, 0:1] * log2e
    krow = lax.broadcasted_iota(jnp.int32, (tc, ncol), 0)
    base_ref[...] = slope * krow.astype(F32)
    acc_ref[...] = jnp.zeros_like(acc_ref)

    def process(j, m, masked):
        rel = j * tc - qi * tq
        off = slope * rel.astype(F32)
        kj = k_ref[pl.ds(pl.multiple_of(j * tc, tc), tc), :]
        s = _nt_dot(kj, q2) + base_ref[...]
        if masked:
            qpos = lax.broadcasted_iota(jnp.int32, (tc, ncol), 1) % tq
            s = jnp.where(krow + rel <= qpos, s, NEG_BIG)
        m_new = jnp.maximum(m, jnp.max(s, axis=0, keepdims=True) + off)
        p = jnp.exp2((s - (m_new - off)).astype(BF16))
        alpha = jnp.exp2(m - m_new)
        acc_ref[...] = acc_ref[...] * alpha + jnp.dot(vt_ref[j], p, preferred_element_type=F32)
        return m_new

    m = jnp.full((1, ncol), NEG_BIG, F32)
    n_full = per_q * qi
    m = lax.fori_loop(0, n_full // 2, lambda it, mm: process(2 * it + 1, process(2 * it, mm, False), False), m)
    m = lax.cond(n_full % 2 == 1, lambda mm: process(n_full - 1, mm, False), lambda mm: mm, m)
    for u in range(per_q):
        m = process(per_q * qi + u, m, True)

    lv = lam_ref[...]
    lam = (jnp.exp(jnp.sum(lv[0:1] * lv[1:2], axis=-1, keepdims=True))
           - jnp.exp(jnp.sum(lv[2:3] * lv[3:4], axis=-1, keepdims=True)) + lam_init)
    on = acc_ref[0:HEAD_W, :] / acc_ref[HEAD_W:HEAD_W + 1, :]
    o = on[:, :tq] - lam * on[:, tq:]
    ms = jnp.mean(o * o, axis=0, keepdims=True)
    o = o * lax.rsqrt(ms + EPS) * subg_ref[...] * (1.0 - lam_init)
    o_ref[...] = o.T.astype(o_ref.dtype)


def diff_attention(proj, lam4, subln_g, batch, seq, *, tq=512, tc=512, layer_idx=0):
    n = proj.shape[0]
    tq = min(tq, seq)
    tc = min(tc, tq)
    nq, nc = seq // tq, seq // tc
    lam_init = 0.8 - 0.6 * math.exp(-0.3 * layer_idx)
    slopes = np.array([2.0 ** (-8.0 * (h + 1) / DA_HEADS) for h in range(DA_HEADS)], dtype=np.float32)
    slope_tab = jnp.asarray(np.broadcast_to(slopes[:, None, None], (DA_HEADS, 8, 128)).copy())
    da_w = DA_HEADS * HEAD_W
    vt5 = proj[:, 2 * da_w:3 * da_w].reshape(batch, nc, tc, DA_HEADS, HEAD_W).transpose(0, 3, 1, 4, 2)
    ones_tile = jnp.zeros((2 * SUBLANES, tc), BF16).at[0].set(1)
    vt5 = jnp.concatenate([vt5, jnp.broadcast_to(ones_tile, vt5.shape[:3] + ones_tile.shape)], axis=3)
    vrows = HEAD_W + 2 * SUBLANES
    return pl.pallas_call(
        functools.partial(_diff_attn_kernel, tq=tq, tc=tc, lam_init=lam_init),
        grid=(batch, DA_HEADS, nq),
        in_specs=[pl.BlockSpec((4, DA_HALF), lambda b, h, i: (0, 0)),
                  pl.BlockSpec((HEAD_W, 1), lambda b, h, i: (0, 0)),
                  pl.BlockSpec((None, 8, 128), lambda b, h, i: (h, 0, 0)),
                  pl.BlockSpec((tq, HEAD_W), lambda b, h, i: (b * nq + i, h)),
                  pl.BlockSpec((seq, HEAD_W), lambda b, h, i: (b, DA_HEADS + h)),
                  pl.BlockSpec((None, None, nc, vrows, tc), lambda b, h, i: (b, h, 0, 0, 0))],
        out_specs=pl.BlockSpec((tq, HEAD_W), lambda b, h, i: (b * nq + i, h)),
        out_shape=jax.ShapeDtypeStruct((n, da_w), BF16),
        scratch_shapes=[pltpu.VMEM((tc, 2 * tq), F32), pltpu.VMEM((vrows, 2 * tq), F32)],
        compiler_params=_cparams(("parallel", "parallel", "arbitrary")),
        name="diff_attention",
    )(lam4, subln_g.reshape(HEAD_W, 1).astype(F32), slope_tab, proj, proj, vt5)


def _hgrn_chunk(q, k, b, v):
    c = q.shape[0]
    row = lax.broadcasted_iota(jnp.int32, (c, 1), 0)
    arow = lax.broadcasted_iota(jnp.int32, (c, c), 0)
    acol = lax.broadcasted_iota(jnp.int32, (c, c), 1)

    def block_level(half):
        grp = 2 * half
        bg = b.reshape(c // grp, grp, HEAD_W)
        ref = jnp.broadcast_to(bg[:, half - 1:half, :], bg.shape).reshape(c, HEAD_W)
        upper = (row % grp) >= half
        qs = jnp.where(upper, q * jnp.exp2(jnp.minimum(b - ref, 0.0)), 0.0)
        ks = jnp.where(upper, 0.0, k * jnp.exp2(jnp.minimum(ref - b, 0.0)))
        a = _nt_dot(qs.astype(BF16), ks.astype(BF16))
        keep = ((arow // grp) == (acol // grp)) & ((arow % grp) >= half) & ((acol % grp) < half)
        return jnp.where(keep, a, 0.0)

    a_tot = None
    half = c // 2
    while half >= HG_SUB:
        lvl = block_level(half)
        a_tot = lvl if a_tot is None else a_tot + lvl
        half //= 2
    o = jnp.dot(a_tot.astype(BF16), v, preferred_element_type=F32)

    nsub = c // HG_SUB
    q3 = q.reshape(nsub, HG_SUB, HEAD_W)
    k3 = k.reshape(nsub, HG_SUB, HEAD_W)
    b3 = b.reshape(nsub, HG_SUB, HEAD_W)
    v3 = v.astype(F32).reshape(nsub, HG_SUB, HEAD_W)
    trow = lax.broadcasted_iota(jnp.int32, (nsub, HG_SUB, 1), 1)
    o3 = jnp.zeros((nsub, HG_SUB, HEAD_W), F32)
    for s in range(HG_SUB):
        e = jnp.exp2(jnp.minimum(b3 - b3[:, s:s + 1, :], 0.0))
        col = jnp.sum(q3 * (k3[:, s:s + 1, :] * e), axis=-1, keepdims=True)
        o3 = o3 + jnp.where(trow >= s, col, 0.0) * v3[:, s:s + 1, :]
    return o + o3.reshape(c, HEAD_W)


def _hgrn_kernel(gam_ref, ng_ref, q_ref, f_ref, i_ref, g_ref, o_ref, st_ref, *, n_chunks, layer_idx):
    @pl.when(pl.program_id(2) == 0)
    def _():
        st_ref[...] = jnp.zeros_like(st_ref)

    gam = gam_ref[...]
    ge = jnp.exp(gam - jnp.max(gam, axis=0, keepdims=True))
    lb = jnp.sum(ge[0:layer_idx + 1], axis=0, keepdims=True) / jnp.sum(ge, axis=0, keepdims=True)
    ng = ng_ref[...]
    c = q_ref.shape[0] // n_chunks
    tri = (lax.broadcasted_iota(jnp.int32, (c, c), 0) >= lax.broadcasted_iota(jnp.int32, (c, c), 1))
    tri = jnp.where(tri, 1.0, 0.0).astype(BF16)

    def chunk(ci, _):
        r0 = pl.multiple_of(ci * c, c)
        qr = q_ref[pl.ds(r0, c), :].astype(F32)
        fr = f_ref[pl.ds(r0, c), :].astype(F32)
        v = i_ref[pl.ds(r0, c), :]
        gr = g_ref[pl.ds(r0, c), :].astype(F32)
        q = qr * _sigmoid(qr)
        f = lb + (1.0 - lb) * _sigmoid(fr)
        logf = jnp.log2(f)
        k = 1.0 - f
        hi = logf.astype(BF16)
        r1 = logf - hi.astype(F32)
        mid = r1.astype(BF16)
        lo = (r1 - mid.astype(F32)).astype(BF16)
        b = (jnp.dot(tri, hi, preferred_element_type=F32) + jnp.dot(tri, mid, preferred_element_type=F32)
             + jnp.dot(tri, lo, preferred_element_type=F32))
        st = st_ref[...]
        o = _nt_dot((q * jnp.exp2(b)).astype(BF16), st.astype(BF16))
        o = o + _hgrn_chunk(q, k, b, v)
        bend = b[c - 1:c]
        kd = (k * jnp.exp2(bend - b)).astype(BF16)
        st_ref[...] = st * jnp.exp2(bend) + _tn_dot(v, kd)
        ms = jnp.mean(o * o, axis=-1, keepdims=True)
        on = o * lax.rsqrt(ms + EPS) * ng * (gr * _sigmoid(gr))
        o_ref[pl.ds(r0, c), :] = on.astype(o_ref.dtype)
        return 0

    lax.fori_loop(0, n_chunks, chunk, 0)


def hgrn2(proj, gamma, norm_g, batch, seq, *, tc=512, layer_idx=0, col0=3 * DA_HEADS):
    n = proj.shape[0]
    tc = min(tc, seq)
    ng = seq // tc
    depth1 = gamma.shape[0]

    def spec(off):
        return pl.BlockSpec((tc, HEAD_W), lambda b, h, g: (b * ng + g, col0 + off * HG_HEADS + h))

    return pl.pallas_call(
        functools.partial(_hgrn_kernel, n_chunks=tc // min(HG_BLOCK, tc), layer_idx=layer_idx),
        grid=(batch, HG_HEADS, ng),
        in_specs=[pl.BlockSpec((depth1, HEAD_W), lambda b, h, g: (0, h)),
                  pl.BlockSpec((1, HEAD_W), lambda b, h, g: (0, 0)),
                  spec(0), spec(1), spec(2), spec(3)],
        out_specs=pl.BlockSpec((tc, HEAD_W), lambda b, h, g: (b * ng + g, h)),
        out_shape=jax.ShapeDtypeStruct((n, HG_HEADS * HEAD_W), BF16),
        scratch_shapes=[pltpu.VMEM((HEAD_W, HEAD_W), F32)],
        compiler_params=_cparams(("parallel", "parallel", "arbitrary")),
        name="hgrn2",
    )(gamma.astype(F32), norm_g.reshape(1, HEAD_W).astype(F32), proj, proj, proj, proj)


def _cross_attn_kernel(q_ref, k_ref, v_ref, o_ref, *, scale):
    q = q_ref[...] * jnp.asarray(scale, BF16)
    s = _nt_dot(q, k_ref[...])
    m = jnp.max(s, axis=-1, keepdims=True)
    p = jnp.exp(s - m)
    l = jnp.sum(p, axis=-1, keepdims=True)
    o = jnp.dot(p.astype(BF16), v_ref[...], preferred_element_type=F32) / l
    o_ref[...] = o.astype(o_ref.dtype)


def cross_attention(q, kv, batch, seq, mem_len, *, tq=512):
    n, d = q.shape
    hd = d // XA_HEADS
    tq = min(tq, seq)
    nq = seq // tq
    return pl.pallas_call(
        functools.partial(_cross_attn_kernel, scale=hd ** -0.5),
        grid=(batch, nq, XA_HEADS),
        in_specs=[pl.BlockSpec((tq, hd), lambda b, i, h: (b * nq + i, h)),
                  pl.BlockSpec((mem_len, hd), lambda b, i, h: (b, h)),
                  pl.BlockSpec((mem_len, hd), lambda b, i, h: (b, XA_HEADS + h))],
        out_specs=pl.BlockSpec((tq, hd), lambda b, i, h: (b * nq + i, h)),
        out_shape=jax.ShapeDtypeStruct((n, d), BF16),
        compiler_params=_cparams(("parallel", "parallel", "parallel")),
        name="cross_attention",
    )(q, kv, kv)


def _topk_desc(s, k):
    rows = s.shape[0]

    def extract(first_only):
        iota = lax.broadcasted_iota(jnp.int32, s.shape, 0)
        rank = jnp.full(s.shape, NOT_SELECTED, F32)
        vals = []
        cur = s
        for r in range(k):
            m = jnp.max(cur, axis=0, keepdims=True)
            sel = cur == m
            if first_only:
                sel = iota == jnp.min(jnp.where(sel, iota, rows), axis=0, keepdims=True)
            rank = jnp.where(sel, float(r), rank)
            cur = jnp.where(sel, -jnp.inf, cur)
            vals.append(m)
        return jnp.concatenate(vals, axis=0), rank, cur

    sv, rank, cur = extract(False)
    removed = jnp.sum(jnp.where(cur == -jnp.inf, 1.0, 0.0), axis=0, keepdims=True)
    return lax.cond(jnp.max(removed) == float(k), lambda: (sv, rank), lambda: extract(True)[:2])


def _peer_topk_kernel(q_ref, keys_ref, l1_ref, c_ref, r2_ref, e2_ref):
    kk = PEER_TOPK
    for h in range(PEER_HEADS):
        q1 = q_ref[:, (2 * h) * PEER_NKEYS:(2 * h + 1) * PEER_NKEYS]
        q2 = q_ref[:, (2 * h + 1) * PEER_NKEYS:(2 * h + 2) * PEER_NKEYS]
        s1 = _nt_dot(keys_ref[h, 0], q1)
        s2 = _nt_dot(keys_ref[h, 1], q2)
        t = s1.shape[1]
        sv12, rank12 = _topk_desc(jnp.concatenate([s1, s2], axis=1), kk)
        sv1, sv2, rank1, rank2 = sv12[:, :t], sv12[:, t:], rank12[:, :t], rank12[:, t:]
        t = s1.shape[1]
        segs, seg_rows = [], []
        for a in range(kk):
            nb = kk // (a + 1)
            if nb > 1:
                rows = -(-nb // SUBLANES) * SUBLANES
                seg = sv1[a:a + 1] + sv2[0:rows]
                if rows != nb:
                    seg = jnp.where(lax.broadcasted_iota(jnp.int32, seg.shape, 0) < nb, seg, -jnp.inf)
                segs.append(seg)
                seg_rows.append(rows)
        single0 = len(segs)
        n_single = kk - single0
        assert n_single % SUBLANES == 0
        segs.append(sv1[single0:kk] + sv2[0:1])
        cand = jnp.concatenate(segs, axis=0)
        best0 = sv1[0:1] + sv2[0:1]

        def pick(first_only, cand=cand, best0=best0):
            iota = lax.broadcasted_iota(jnp.int32, cand.shape, 0)
            picked = jnp.zeros(cand.shape, F32)
            z = jnp.zeros((1, t), F32)
            cur = cand
            for _ in range(kk):
                m = jnp.max(cur, axis=0, keepdims=True)
                sel = cur == m
                if first_only:
                    sel = iota == jnp.min(jnp.where(sel, iota, cand.shape[0]), axis=0, keepdims=True)
                cur = jnp.where(sel, -jnp.inf, cur)
                picked = jnp.where(sel, 1.0, picked)
                z = z + jnp.exp(m - best0)
            return picked, z

        picked, z = pick(False)
        n_picked = jnp.sum(picked, axis=0, keepdims=True)
        picked, z = lax.cond(jnp.max(n_picked) == float(kk), lambda p=picked, zz=z: (p, zz),
                             lambda: pick(True))
        count, r0 = [], 0
        for rows in seg_rows:
            count.append(jnp.sum(picked[r0:r0 + rows], axis=0, keepdims=True))
            r0 += rows
        count += [picked[r0 + i:r0 + i + 1] for i in range(n_single)]
        l1 = jnp.zeros_like(rank1)
        for a in range(kk):
            l1 = jnp.where(rank1 == float(a), count[a], l1)
        sel1 = rank1 < float(kk)
        l1_ref[h] = l1
        c_ref[h] = jnp.where(sel1, jnp.exp(jnp.minimum(s1 - sv1[0:1], 0.0)) / z, 0.0)
        r2_ref[h] = rank2.astype(r2_ref.dtype)
        e2 = jnp.where(rank2 < float(kk), jnp.exp(jnp.minimum(s2 - sv2[0:1], 0.0)), 0.0)
        e2_ref[h] = e2.astype(e2_ref.dtype)


def peer_topk(q, keys, *, tt=256):
    n = q.shape[0]
    tt = min(tt, n)
    tab_spec = pl.BlockSpec((PEER_HEADS, PEER_NKEYS, tt), lambda i: (0, 0, i))
    return pl.pallas_call(
        _peer_topk_kernel,
        grid=(n // tt,),
        in_specs=[pl.BlockSpec((tt, q.shape[1]), lambda i: (i, 0)),
                  pl.BlockSpec(keys.shape, lambda i: (0, 0, 0, 0))],
        out_specs=[tab_spec] * 4,
        out_shape=[jax.ShapeDtypeStruct((PEER_HEADS, PEER_NKEYS, n), dt) for dt in (F32, F32, BF16, BF16)],
        compiler_params=_cparams(("parallel",)),
        name="peer_topk",
    )(q, keys)


def _gelu_exact(x):
    return 0.5 * x * (1.0 + lax.erf(x * (2.0 ** -0.5)))


def _peer_dense_kernel(hn_ref, down_ref, up_ref, l1_ref, c_ref, r2_ref, e2_ref, o_ref, wg_ref, *, groups):
    s = pl.program_id(1)
    row0 = (s * groups) % SUBLANES

    @pl.when(s == 0)
    def _():
        o_ref[...] = jnp.zeros_like(o_ref)

    tt = o_ref.shape[1]
    lane_chunk = 256
    halves = 2
    gh = groups // halves
    wgs = []
    for hf in range(halves):
        er = slice(hf * gh * PEER_NKEYS, (hf + 1) * gh * PEER_NKEYS)
        act = _nt_dot(down_ref[er, :], hn_ref[...])
        for gl in range(gh):
            gi = hf * gh + gl
            rows = slice(gi * PEER_NKEYS, (gi + 1) * PEER_NKEYS)
            lrows = slice(gl * PEER_NKEYS, (gl + 1) * PEER_NKEYS)
            for t0 in range(0, tt, lane_chunk):
                lanes = slice(t0, t0 + lane_chunk)
                g = None
                for h in range(PEER_HEADS):
                    l1 = l1_ref[h, pl.ds(row0 + gi, 1), lanes].astype(BF16)
                    cc = c_ref[h, pl.ds(row0 + gi, 1), lanes].astype(BF16)
                    e2 = e2_ref[h, :, lanes]
                    term = jnp.where(r2_ref[h, :, lanes] < l1, e2 * cc, jnp.zeros_like(e2))
                    g = term if g is None else g + term
                wg_ref[rows, lanes] = g * _gelu_exact(act[lrows, lanes]).astype(BF16)
        wgs.append(wg_ref[er, :])
    acc = o_ref[...]
    for hf in range(halves):
        er = slice(hf * gh * PEER_NKEYS, (hf + 1) * gh * PEER_NKEYS)
        acc = acc + _tn_dot(up_ref[er, :], wgs[hf])
    o_ref[...] = acc


def peer_dense(hn, down, up, tables, *, tt=512, te=512):
    n, d = hn.shape
    n_exp = down.shape[0]
    tt = min(tt, n)
    groups = te // PEER_NKEYS
    nblk = n_exp // te
    l1, c, r2, e2 = tables
    assert SUBLANES % groups == 0 and groups % 2 == 0
    row_spec = pl.BlockSpec((PEER_HEADS, SUBLANES, tt), lambda i, s: (0, s * groups // SUBLANES, i))
    tab_spec = pl.BlockSpec((PEER_HEADS, PEER_NKEYS, tt), lambda i, s: (0, 0, i))
    return pl.pallas_call(
        functools.partial(_peer_dense_kernel, groups=groups),
        grid=(n // tt, nblk),
        in_specs=[pl.BlockSpec((tt, d), lambda i, s: (i, 0)),
                  pl.BlockSpec((te, d), lambda i, s: (s, 0)),
                  pl.BlockSpec((te, d), lambda i, s: (s, 0)),
                  row_spec, row_spec, tab_spec, tab_spec],
        out_specs=pl.BlockSpec((d, tt), lambda i, s: (0, i)),
        out_shape=jax.ShapeDtypeStruct((d, n), F32),
        scratch_shapes=[pltpu.VMEM((te, tt), BF16)],
        compiler_params=_cparams(("parallel", "arbitrary")),
        name="peer_dense",
    )(hn, down, up, l1, c, r2, e2)


def _final_norm_kernel(h_ref, pt_ref, g_ref, o_ref):
    x = h_ref[...] + pt_ref[...].T
    ms = jnp.mean(x * x, axis=-1, keepdims=True)
    o_ref[...] = x * lax.rsqrt(ms + EPS) * g_ref[...]


def final_norm(h, pt, g, *, tm=256):
    m, d = h.shape
    tm = min(tm, m)
    row = pl.BlockSpec((tm, d), lambda i: (i, 0))
    return pl.pallas_call(
        _final_norm_kernel,
        grid=(m // tm,),
        in_specs=[row, pl.BlockSpec((d, tm), lambda i: (0, i)), pl.BlockSpec((1, d), lambda i: (0, 0))],
        out_specs=row,
        out_shape=jax.ShapeDtypeStruct((m, d), F32),
        compiler_params=_cparams(("parallel",)),
        name="final_norm",
    )(h, pt, g.reshape(1, d).astype(F32))


def kernel(x, mem, mix_norm_g, w_in, da_lambda_q1, da_lambda_k1, da_lambda_q2, da_lambda_k2, da_subln_g, hgrn_gamma, hgrn_norm_g, w_out, cross_norm_g, mem_norm_g, w_cq, w_ckv, w_co, ffn_norm_g, peer_wq, peer_subkeys, peer_down, peer_up, final_norm_g):
    batch, seq, d = x.shape
    mem_len = mem.shape[1]
    depth = w_in.shape[0]
    assert depth == 1 and hgrn_gamma.shape[0] == depth + 1
    l = 0
    da_w = DA_HEADS * HEAD_W
    x2 = x.reshape(batch * seq, d)

    proj = norm_matmul(x2, mix_norm_g[l], w_in[l].astype(BF16))
    lam4 = jnp.stack([da_lambda_q1[l], da_lambda_k1[l], da_lambda_q2[l], da_lambda_k2[l]]).astype(F32)
    a_out = diff_attention(proj, lam4, da_subln_g[l], batch, seq, layer_idx=l)
    b_out = hgrn2(proj, hgrn_gamma, hgrn_norm_g[l], batch, seq, layer_idx=l)
    wo = w_out[l].astype(BF16)
    h1 = matmul_residual([a_out, b_out], [wo[:da_w], wo[da_w:]], x2)

    qc = norm_matmul(h1, cross_norm_g[l], w_cq[l].astype(BF16))
    kv = norm_matmul(mem.reshape(batch * mem_len, d), mem_norm_g[l], w_ckv[l].astype(BF16))
    oc = cross_attention(qc, kv, batch, seq, mem_len)
    h2 = matmul_residual([oc], [w_co[l].astype(BF16)], h1)

    qp, hn = norm_matmul(h2, ffn_norm_g[l], peer_wq[l].astype(BF16), tn=512, emit_xn=True)
    tables = peer_topk(qp, peer_subkeys[l].astype(BF16))
    peer_t = peer_dense(hn, peer_down[l].astype(BF16), peer_up[l].astype(BF16), tables)
    out = final_norm(h2, peer_t, final_norm_g)
    return out.reshape(batch, seq, d)
```

```python
import functools
import math

import numpy as np
import jax
import jax.numpy as jnp
from jax import lax
from jax.experimental import pallas as pl
from jax.experimental.pallas import tpu as pltpu

F32 = jnp.float32
BF16 = jnp.bfloat16
EPS = 1e-6
NEG_BIG = -1e30

DA_HEADS = 16
DA_HALF = 64
HEAD_W = 128
HG_HEADS = 16
HG_BLOCK = 256
HG_SUB = 8
XA_HEADS = 4
PEER_HEADS = 8
PEER_NKEYS = 128
PEER_TOPK = 16
NOT_SELECTED = 99.0
SUBLANES = 8

VMEM_LIMIT = 56 * 1024 * 1024


def _cparams(sem):
    return pltpu.CompilerParams(dimension_semantics=sem, vmem_limit_bytes=VMEM_LIMIT)


def _nt_dot(a, b):
    return lax.dot_general(a, b, (((1,), (1,)), ((), ())), preferred_element_type=F32)


def _tn_dot(a, b):
    return lax.dot_general(a, b, (((0,), (0,)), ((), ())), preferred_element_type=F32)


def _sigmoid(x):
    return 0.5 * jnp.tanh(0.5 * x) + 0.5


def _norm_matmul_kernel(x_ref, g_ref, w_ref, o_ref, xn_ref, *, rows):
    @pl.when(pl.program_id(1) == 0)
    def _():
        g = g_ref[...]

        def chunk(ci, _):
            r0 = pl.multiple_of(ci * rows, rows)
            x = x_ref[pl.ds(r0, rows), :]
            ms = jnp.mean(x * x, axis=-1, keepdims=True)
            xn_ref[pl.ds(r0, rows), :] = (x * lax.rsqrt(ms + EPS) * g).astype(BF16)
            return 0

        lax.fori_loop(0, x_ref.shape[0] // rows, chunk, 0)

    o_ref[...] = jnp.dot(xn_ref[...], w_ref[...], preferred_element_type=F32).astype(o_ref.dtype)


def norm_matmul(x, g, w, *, tm=512, tn=1024, emit_xn=False):
    m, k = x.shape
    n = w.shape[1]
    tm, tn = min(tm, m), min(tn, n)
    assert m % tm == 0 and n % tn == 0
    rows = min(128, tm)
    out_shape = [jax.ShapeDtypeStruct((m, n), BF16)]
    out_specs = [pl.BlockSpec((tm, tn), lambda i, j: (i, j))]
    scratch = [pltpu.VMEM((tm, k), BF16)]
    if emit_xn:
        out_shape.append(jax.ShapeDtypeStruct((m, k), BF16))
        out_specs.append(pl.BlockSpec((tm, k), lambda i, j: (i, 0)))
        scratch = []
    res = pl.pallas_call(
        functools.partial(_norm_matmul_kernel, rows=rows),
        grid=(m // tm, n // tn),
        in_specs=[pl.BlockSpec((tm, k), lambda i, j: (i, 0)),
                  pl.BlockSpec((1, k), lambda i, j: (0, 0)),
                  pl.BlockSpec((k, tn), lambda i, j: (0, j))],
        out_specs=out_specs,
        out_shape=out_shape,
        scratch_shapes=scratch,
        compiler_params=_cparams(("parallel", "arbitrary")),
        name="norm_matmul",
    )(x, g.reshape(1, k).astype(F32), w)
    return res if emit_xn else res[0]


def _matmul_residual_kernel(*refs, n_lhs):
    lhs = refs[:n_lhs]
    ws = refs[n_lhs:2 * n_lhs]
    res_ref, o_ref = refs[2 * n_lhs], refs[2 * n_lhs + 1]
    acc = res_ref[...]
    for a_ref, w_ref in zip(lhs, ws):
        acc = acc + jnp.dot(a_ref[...], w_ref[...], preferred_element_type=F32)
    o_ref[...] = acc


def matmul_residual(lhs_list, w_list, res, *, tm=512, tn=1024):
    m, n = res.shape
    tm, tn = min(tm, m), min(tn, n)
    assert m % tm == 0 and n % tn == 0
    n_lhs = len(lhs_list)
    in_specs = [pl.BlockSpec((tm, a.shape[1]), lambda i, j: (i, 0)) for a in lhs_list]
    in_specs += [pl.BlockSpec((w.shape[0], tn), lambda i, j: (0, j)) for w in w_list]
    in_specs += [pl.BlockSpec((tm, tn), lambda i, j: (i, j))]
    return pl.pallas_call(
        functools.partial(_matmul_residual_kernel, n_lhs=n_lhs),
        grid=(m // tm, n // tn),
        in_specs=in_specs,
        out_specs=pl.BlockSpec((tm, tn), lambda i, j: (i, j)),
        out_shape=jax.ShapeDtypeStruct((m, n), F32),
        compiler_params=_cparams(("parallel", "parallel")),
        name="matmul_residual",
    )(*lhs_list, *w_list, res)


def _diff_attn_kernel(lam_ref, subg_ref, slope_ref, q_ref, k_ref, vt_ref, o_ref, base_ref, acc_ref, *,
                      tq, tc, lam_init):
    qi = pl.program_id(2)
    ncol = 2 * tq
    per_q = tq // tc
    log2e = math.log2(math.e)
    q = (q_ref[...].astype(F32) * (DA_HALF ** -0.5 * log2e)).astype(BF16)
    lane = lax.broadcasted_iota(jnp.int32, q.shape, 1)
    zero = jnp.zeros_like(q)
    q2 = jnp.concatenate([jnp.where(lane < DA_HALF, q, zero), jnp.where(lane >= DA_HALF, q, zero)], axis=0)
    slope = slope_ref[0:1, 0:1] * log2e
    krow = lax.broadcasted_iota(jnp.int32, (tc, ncol), 0)
    @pl.when(qi == 0)
    def _():
        base_ref[...] = slope * krow.astype(F32)

    acc_ref[...] = jnp.zeros_like(acc_ref)

    def process(j, m, masked):
        rel = j * tc - qi * tq
        off = slope * rel.astype(F32)
        kj = k_ref[pl.ds(pl.multiple_of(j * tc, tc), tc), :]
        s = _nt_dot(kj, q2) + base_ref[...]
        if masked:
            qpos = lax.broadcasted_iota(jnp.int32, (tc, ncol), 1) % tq
            s = jnp.where(krow + rel <= qpos, s, NEG_BIG)
        m_new = jnp.maximum(m, jnp.max(s, axis=0, keepdims=True) + off)
        p = jnp.exp2((s - (m_new - off)).astype(BF16))
        alpha = jnp.exp2(m - m_new)
        acc_ref[...] = acc_ref[...] * alpha + jnp.dot(vt_ref[j], p, preferred_element_type=F32)
        return m_new

    m = jnp.full((1, ncol), NEG_BIG, F32)
    n_full = per_q * qi
    m = lax.fori_loop(0, n_full // 2, lambda it, mm: process(2 * it + 1, process(2 * it, mm, False), False), m)
    m = lax.cond(n_full % 2 == 1, lambda mm: process(n_full - 1, mm, False), lambda mm: mm, m)
    for u in range(per_q):
        m = process(per_q * qi + u, m, True)

    lv = lam_ref[...]
    lam = (jnp.exp(jnp.sum(lv[0:1] * lv[1:2], axis=-1, keepdims=True))
           - jnp.exp(jnp.sum(lv[2:3] * lv[3:4], axis=-1, keepdims=True)) + lam_init)
    on = acc_ref[0:HEAD_W, :] / acc_ref[HEAD_W:HEAD_W + 1, :]
    o = on[:, :tq] - lam * on[:, tq:]
    ms = jnp.mean(o * o, axis=0, keepdims=True)
    o = o * lax.rsqrt(ms + EPS) * subg_ref[...] * (1.0 - lam_init)
    o_ref[...] = o.T.astype(o_ref.dtype)


def diff_attention(proj, lam4, subln_g, batch, seq, *, tq=512, tc=512, layer_idx=0):
    n = proj.shape[0]
    tq = min(tq, seq)
    tc = min(tc, tq)
    nq, nc = seq // tq, seq // tc
    lam_init = 0.8 - 0.6 * math.exp(-0.3 * layer_idx)
    slopes = np.array([2.0 ** (-8.0 * (h + 1) / DA_HEADS) for h in range(DA_HEADS)], dtype=np.float32)
    slope_tab = jnp.asarray(np.broadcast_to(slopes[:, None, None], (DA_HEADS, 8, 128)).copy())
    da_w = DA_HEADS * HEAD_W
    vt5 = proj[:, 2 * da_w:3 * da_w].reshape(batch, nc, tc, DA_HEADS, HEAD_W).transpose(0, 3, 1, 4, 2)
    ones_tile = jnp.zeros((2 * SUBLANES, tc), BF16).at[0].set(1)
    vt5 = jnp.concatenate([vt5, jnp.broadcast_to(ones_tile, vt5.shape[:3] + ones_tile.shape)], axis=3)
    vrows = HEAD_W + 2 * SUBLANES
    return pl.pallas_call(
        functools.partial(_diff_attn_kernel, tq=tq, tc=tc, lam_init=lam_init),
        grid=(batch, DA_HEADS, nq),
        in_specs=[pl.BlockSpec((4, DA_HALF), lambda b, h, i: (0, 0)),
                  pl.BlockSpec((HEAD_W, 1), lambda b, h, i: (0, 0)),
                  pl.BlockSpec((None, 8, 128), lambda b, h, i: (h, 0, 0)),
                  pl.BlockSpec((tq, HEAD_W), lambda b, h, i: (b * nq + i, h)),
                  pl.BlockSpec((seq, HEAD_W), lambda b, h, i: (b, DA_HEADS + h)),
                  pl.BlockSpec((None, None, nc, vrows, tc), lambda b, h, i: (b, h, 0, 0, 0))],
        out_specs=pl.BlockSpec((tq, HEAD_W), lambda b, h, i: (b * nq + i, h)),
        out_shape=jax.ShapeDtypeStruct((n, da_w), BF16),
        scratch_shapes=[pltpu.VMEM((tc, 2 * tq), F32), pltpu.VMEM((vrows, 2 * tq), F32)],
        compiler_params=_cparams(("parallel", "parallel", "arbitrary")),
        name="diff_attention",
    )(lam4, subln_g.reshape(HEAD_W, 1).astype(F32), slope_tab, proj, proj, vt5)


def _hgrn_chunk(q, k, b, v):
    c = q.shape[0]
    row = lax.broadcasted_iota(jnp.int32, (c, 1), 0)
    arow = lax.broadcasted_iota(jnp.int32, (c, c), 0)
    acol = lax.broadcasted_iota(jnp.int32, (c, c), 1)

    def block_level(half):
        grp = 2 * half
        bg = b.reshape(c // grp, grp, HEAD_W)
        ref = jnp.broadcast_to(bg[:, half - 1:half, :], bg.shape).reshape(c, HEAD_W)
        upper = (row % grp) >= half
        qs = jnp.where(upper, q * jnp.exp2(jnp.minimum(b - ref, 0.0)), 0.0)
        ks = jnp.where(upper, 0.0, k * jnp.exp2(jnp.minimum(ref - b, 0.0)))
        a = _nt_dot(qs.astype(BF16), ks.astype(BF16))
        keep = ((arow // grp) == (acol // grp)) & ((arow % grp) >= half) & ((acol % grp) < half)
        return jnp.where(keep, a, 0.0)

    a_tot = None
    half = c // 2
    while half >= HG_SUB:
        lvl = block_level(half)
        a_tot = lvl if a_tot is None else a_tot + lvl
        half //= 2
    o = jnp.dot(a_tot.astype(BF16), v, preferred_element_type=F32)

    nsub = c // HG_SUB
    q3 = q.reshape(nsub, HG_SUB, HEAD_W)
    k3 = k.reshape(nsub, HG_SUB, HEAD_W)
    b3 = b.reshape(nsub, HG_SUB, HEAD_W)
    v3 = v.astype(F32).reshape(nsub, HG_SUB, HEAD_W)
    trow = lax.broadcasted_iota(jnp.int32, (nsub, HG_SUB, 1), 1)
    o3 = jnp.zeros((nsub, HG_SUB, HEAD_W), F32)
    for s in range(HG_SUB):
        e = jnp.exp2(jnp.minimum(b3 - b3[:, s:s + 1, :], 0.0))
        col = jnp.sum(q3 * (k3[:, s:s + 1, :] * e), axis=-1, keepdims=True)
        o3 = o3 + jnp.where(trow >= s, col, 0.0) * v3[:, s:s + 1, :]
    return o + o3.reshape(c, HEAD_W)


def _hgrn_kernel(gam_ref, ng_ref, q_ref, f_ref, i_ref, g_ref, o_ref, st_ref, *, n_chunks, layer_idx):
    @pl.when(pl.program_id(2) == 0)
    def _():
        st_ref[...] = jnp.zeros_like(st_ref)

    gam = gam_ref[...]
    ge = jnp.exp(gam - jnp.max(gam, axis=0, keepdims=True))
    lb = jnp.sum(ge[0:layer_idx + 1], axis=0, keepdims=True) / jnp.sum(ge, axis=0, keepdims=True)
    ng = ng_ref[...]
    c = q_ref.shape[0] // n_chunks
    tri = (lax.broadcasted_iota(jnp.int32, (c, c), 0) >= lax.broadcasted_iota(jnp.int32, (c, c), 1))
    tri = jnp.where(tri, 1.0, 0.0).astype(BF16)

    def chunk(ci, _):
        r0 = pl.multiple_of(ci * c, c)
        qr = q_ref[pl.ds(r0, c), :].astype(F32)
        fr = f_ref[pl.ds(r0, c), :].astype(F32)
        v = i_ref[pl.ds(r0, c), :]
        gr = g_ref[pl.ds(r0, c), :].astype(F32)
        q = qr * _sigmoid(qr)
        f = lb + (1.0 - lb) * _sigmoid(fr)
        logf = jnp.log2(f)
        k = 1.0 - f
        hi = logf.astype(BF16)
        r1 = logf - hi.astype(F32)
        mid = r1.astype(BF16)
        lo = (r1 - mid.astype(F32)).astype(BF16)
        b = (jnp.dot(tri, hi, preferred_element_type=F32) + jnp.dot(tri, mid, preferred_element_type=F32)
             + jnp.dot(tri, lo, preferred_element_type=F32))
        st = st_ref[...]
        o = _nt_dot((q * jnp.exp2(b)).astype(BF16), st.astype(BF16))
        o = o + _hgrn_chunk(q, k, b, v)
        bend = b[c - 1:c]
        kd = (k * jnp.exp2(bend - b)).astype(BF16)
        st_ref[...] = st * jnp.exp2(bend) + _tn_dot(v, kd)
        ms = jnp.mean(o * o, axis=-1, keepdims=True)
        on = o * lax.rsqrt(ms + EPS) * ng * (gr * _sigmoid(gr))
        o_ref[pl.ds(r0, c), :] = on.astype(o_ref.dtype)
        return 0

    if n_chunks % 2:
        lax.fori_loop(0, n_chunks, chunk, 0)
    else:
        def pair(i, _):
            chunk(2 * i, 0)
            chunk(2 * i + 1, 0)
            return 0

        lax.fori_loop(0, n_chunks // 2, pair, 0)


def hgrn2(proj, gamma, norm_g, batch, seq, *, tc=1024, layer_idx=0, col0=3 * DA_HEADS):
    n = proj.shape[0]
    tc = min(tc, seq)
    ng = seq // tc
    depth1 = gamma.shape[0]

    def spec(off):
        return pl.BlockSpec((tc, HEAD_W), lambda b, h, g: (b * ng + g, col0 + off * HG_HEADS + h))

    return pl.pallas_call(
        functools.partial(_hgrn_kernel, n_chunks=tc // min(HG_BLOCK, tc), layer_idx=layer_idx),
        grid=(batch, HG_HEADS, ng),
        in_specs=[pl.BlockSpec((depth1, HEAD_W), lambda b, h, g: (0, h)),
                  pl.BlockSpec((1, HEAD_W), lambda b, h, g: (0, 0)),
                  spec(0), spec(1), spec(2), spec(3)],
        out_specs=pl.BlockSpec((tc, HEAD_W), lambda b, h, g: (b * ng + g, h)),
        out_shape=jax.ShapeDtypeStruct((n, HG_HEADS * HEAD_W), BF16),
        scratch_shapes=[pltpu.VMEM((HEAD_W, HEAD_W), F32)],
        compiler_params=_cparams(("parallel", "parallel", "arbitrary")),
        name="hgrn2",
    )(gamma.astype(F32), norm_g.reshape(1, HEAD_W).astype(F32), proj, proj, proj, proj)


def _cross_attn_kernel(q_ref, k_ref, v_ref, o_ref, *, scale):
    q = q_ref[...] * jnp.asarray(scale, BF16)
    s = _nt_dot(q, k_ref[...])
    m = jnp.max(s, axis=-1, keepdims=True)
    p = jnp.exp(s - m)
    l = jnp.sum(p, axis=-1, keepdims=True)
    o = jnp.dot(p.astype(BF16), v_ref[...], preferred_element_type=F32) / l
    o_ref[...] = o.astype(o_ref.dtype)


def cross_attention(q, kv, batch, seq, mem_len, *, tq=512):
    n, d = q.shape
    hd = d // XA_HEADS
    tq = min(tq, seq)
    nq = seq // tq
    return pl.pallas_call(
        functools.partial(_cross_attn_kernel, scale=hd ** -0.5),
        grid=(batch, nq, XA_HEADS),
        in_specs=[pl.BlockSpec((tq, hd), lambda b, i, h: (b * nq + i, h)),
                  pl.BlockSpec((mem_len, hd), lambda b, i, h: (b, h)),
                  pl.BlockSpec((mem_len, hd), lambda b, i, h: (b, XA_HEADS + h))],
        out_specs=pl.BlockSpec((tq, hd), lambda b, i, h: (b * nq + i, h)),
        out_shape=jax.ShapeDtypeStruct((n, d), BF16),
        compiler_params=_cparams(("parallel", "parallel", "parallel")),
        name="cross_attention",
    )(q, kv, kv)


def _topk_desc(s, k):
    rows = s.shape[0]

    def extract(first_only):
        iota = lax.broadcasted_iota(jnp.int32, s.shape, 0)
        rank = jnp.full(s.shape, NOT_SELECTED, F32)
        vals = []
        cur = s
        for r in range(k):
            m = jnp.max(cur, axis=0, keepdims=True)
            sel = cur == m
            if first_only:
                sel = iota == jnp.min(jnp.where(sel, iota, rows), axis=0, keepdims=True)
            rank = jnp.where(sel, float(r), rank)
            cur = jnp.where(sel, -jnp.inf, cur)
            vals.append(m)
        return jnp.concatenate(vals, axis=0), rank, cur

    sv, rank, cur = extract(False)
    removed = jnp.sum(jnp.where(cur == -jnp.inf, 1.0, 0.0), axis=0, keepdims=True)
    return lax.cond(jnp.max(removed) == float(k), lambda: (sv, rank), lambda: extract(True)[:2])


def _peer_topk_kernel(q_ref, keys_ref, l1_ref, c_ref, r2_ref, e2_ref):
    kk = PEER_TOPK
    for h in range(PEER_HEADS):
        q1 = q_ref[:, (2 * h) * PEER_NKEYS:(2 * h + 1) * PEER_NKEYS]
        q2 = q_ref[:, (2 * h + 1) * PEER_NKEYS:(2 * h + 2) * PEER_NKEYS]
        s1 = _nt_dot(keys_ref[h, 0], q1)
        s2 = _nt_dot(keys_ref[h, 1], q2)
        t = s1.shape[1]
        sv12, rank12 = _topk_desc(jnp.concatenate([s1, s2], axis=1), kk)
        sv1, sv2, rank1, rank2 = sv12[:, :t], sv12[:, t:], rank12[:, :t], rank12[:, t:]
        segs, seg_rows = [], []
        for a in range(kk):
            nb = kk // (a + 1)
            if nb > 1:
                rows = -(-nb // SUBLANES) * SUBLANES
                seg = sv1[a:a + 1] + sv2[0:rows]
                if rows != nb:
                    seg = jnp.where(lax.broadcasted_iota(jnp.int32, seg.shape, 0) < nb, seg, -jnp.inf)
                segs.append(seg)
                seg_rows.append(rows)
        single0 = len(segs)
        n_single = kk - single0
        assert n_single % SUBLANES == 0
        segs.append(sv1[single0:kk] + sv2[0:1])
        cand = jnp.concatenate(segs, axis=0)
        best0 = sv1[0:1] + sv2[0:1]

        def pick(first_only, cand=cand, best0=best0):
            iota = lax.broadcasted_iota(jnp.int32, cand.shape, 0)
            picked = jnp.zeros(cand.shape, F32)
            z = jnp.zeros((1, t), F32)
            cur = cand
            for _ in range(kk):
                m = jnp.max(cur, axis=0, keepdims=True)
                sel = cur == m
                if first_only:
                    sel = iota == jnp.min(jnp.where(sel, iota, cand.shape[0]), axis=0, keepdims=True)
                cur = jnp.where(sel, -jnp.inf, cur)
                picked = jnp.where(sel, 1.0, picked)
                z = z + jnp.exp(m - best0)
            return picked, z

        picked, z = pick(False)
        n_picked = jnp.sum(picked, axis=0, keepdims=True)
        picked, z = lax.cond(jnp.max(n_picked) == float(kk), lambda p=picked, zz=z: (p, zz),
                             lambda: pick(True))
        count, r0 = [], 0
        for rows in seg_rows:
            count.append(jnp.sum(picked[r0:r0 + rows], axis=0, keepdims=True))
            r0 += rows
        count += [picked[r0 + i:r0 + i + 1] for i in range(n_single)]
        l1 = jnp.zeros_like(rank1)
        for a in range(kk):
            l1 = jnp.where(rank1 == float(a), count[a], l1)
        sel1 = rank1 < float(kk)
        l1_ref[h] = l1
        c_ref[h] = jnp.where(sel1, jnp.exp(jnp.minimum(s1 - sv1[0:1], 0.0)) / z, 0.0)
        r2_ref[h] = rank2.astype(r2_ref.dtype)
        e2 = jnp.where(rank2 < float(kk), jnp.exp(jnp.minimum(s2 - sv2[0:1], 0.0)), 0.0)
        e2_ref[h] = e2.astype(e2_ref.dtype)


def peer_topk(q, keys, *, tt=256):
    n = q.shape[0]
    tt = min(tt, n)
    tab_spec = pl.BlockSpec((PEER_HEADS, PEER_NKEYS, tt), lambda i: (0, 0, i))
    return pl.pallas_call(
        _peer_topk_kernel,
        grid=(n // tt,),
        in_specs=[pl.BlockSpec((tt, q.shape[1]), lambda i: (i, 0)),
                  pl.BlockSpec(keys.shape, lambda i: (0, 0, 0, 0))],
        out_specs=[tab_spec] * 4,
        out_shape=[jax.ShapeDtypeStruct((PEER_HEADS, PEER_NKEYS, n), dt) for dt in (F32, F32, BF16, BF16)],
        compiler_params=_cparams(("parallel",)),
        name="peer_topk",
    )(q, keys)


def _gelu_exact(x):
    return 0.5 * x * (1.0 + lax.erf(x * (2.0 ** -0.5)))


def _peer_dense_kernel(hn_ref, down_ref, up_ref, l1_ref, c_ref, r2_ref, e2_ref, o_ref, wg_ref, *, groups):
    s = pl.program_id(1)
    row0 = (s * groups) % SUBLANES

    @pl.when(s == 0)
    def _():
        o_ref[...] = jnp.zeros_like(o_ref)

    tt = o_ref.shape[1]
    lane_chunk = 256
    halves = 2
    gh = groups // halves
    wgs = []
    for hf in range(halves):
        er = slice(hf * gh * PEER_NKEYS, (hf + 1) * gh * PEER_NKEYS)
        act = _nt_dot(down_ref[er, :], hn_ref[...])
        for gl in range(gh):
            gi = hf * gh + gl
            rows = slice(gi * PEER_NKEYS, (gi + 1) * PEER_NKEYS)
            lrows = slice(gl * PEER_NKEYS, (gl + 1) * PEER_NKEYS)
            for t0 in range(0, tt, lane_chunk):
                lanes = slice(t0, t0 + lane_chunk)
                g = None
                for h in range(PEER_HEADS):
                    l1 = l1_ref[h, pl.ds(row0 + gi, 1), lanes].astype(BF16)
                    cc = c_ref[h, pl.ds(row0 + gi, 1), lanes].astype(BF16)
                    e2 = e2_ref[h, :, lanes]
                    term = jnp.where(r2_ref[h, :, lanes] < l1, e2 * cc, jnp.zeros_like(e2))
                    g = term if g is None else g + term
                wg_ref[rows, lanes] = g * _gelu_exact(act[lrows, lanes]).astype(BF16)
        wgs.append(wg_ref[er, :])
    acc = o_ref[...]
    for hf in range(halves):
        er = slice(hf * gh * PEER_NKEYS, (hf + 1) * gh * PEER_NKEYS)
        acc = acc + _tn_dot(up_ref[er, :], wgs[hf])
    o_ref[...] = acc


def peer_dense(hn, down, up, tables, *, tt=512, te=512):
    n, d = hn.shape
    n_exp = down.shape[0]
    tt = min(tt, n)
    groups = te // PEER_NKEYS
    nblk = n_exp // te
    l1, c, r2, e2 = tables
    assert SUBLANES % groups == 0 and groups % 2 == 0
    row_spec = pl.BlockSpec((PEER_HEADS, SUBLANES, tt), lambda i, s: (0, s * groups // SUBLANES, i))
    tab_spec = pl.BlockSpec((PEER_HEADS, PEER_NKEYS, tt), lambda i, s: (0, 0, i))
    return pl.pallas_call(
        functools.partial(_peer_dense_kernel, groups=groups),
        grid=(n // tt, nblk),
        in_specs=[pl.BlockSpec((tt, d), lambda i, s: (i, 0)),
                  pl.BlockSpec((te, d), lambda i, s: (s, 0)),
                  pl.BlockSpec((te, d), lambda i, s: (s, 0)),
                  row_spec, row_spec, tab_spec, tab_spec],
        out_specs=pl.BlockSpec((d, tt), lambda i, s: (0, i)),
        out_shape=jax.ShapeDtypeStruct((d, n), F32),
        scratch_shapes=[pltpu.VMEM((te, tt), BF16)],
        compiler_params=_cparams(("parallel", "arbitrary")),
        name="peer_dense",
    )(hn, down, up, l1, c, r2, e2)


def _final_norm_kernel(h_ref, pt_ref, g_ref, o_ref):
    x = h_ref[...] + pt_ref[...].T
    ms = jnp.mean(x * x, axis=-1, keepdims=True)
    o_ref[...] = x * lax.rsqrt(ms + EPS) * g_ref[...]


def final_norm(h, pt, g, *, tm=256):
    m, d = h.shape
    tm = min(tm, m)
    row = pl.BlockSpec((tm, d), lambda i: (i, 0))
    return pl.pallas_call(
        _final_norm_kernel,
        grid=(m // tm,),
        in_specs=[row, pl.BlockSpec((d, tm), lambda i: (0, i)), pl.BlockSpec((1, d), lambda i: (0, 0))],
        out_specs=row,
        out_shape=jax.ShapeDtypeStruct((m, d), F32),
        compiler_params=_cparams(("parallel",)),
        name="final_norm",
    )(h, pt, g.reshape(1, d).astype(F32))


def kernel(x, mem, mix_norm_g, w_in, da_lambda_q1, da_lambda_k1, da_lambda_q2, da_lambda_k2, da_subln_g, hgrn_gamma, hgrn_norm_g, w_out, cross_norm_g, mem_norm_g, w_cq, w_ckv, w_co, ffn_norm_g, peer_wq, peer_subkeys, peer_down, peer_up, final_norm_g):
    batch, seq, d = x.shape
    mem_len = mem.shape[1]
    depth = w_in.shape[0]
    assert depth == 1 and hgrn_gamma.shape[0] == depth + 1
    l = 0
    da_w = DA_HEADS * HEAD_W
    x2 = x.reshape(batch * seq, d)

    proj = norm_matmul(x2, mix_norm_g[l], w_in[l].astype(BF16))
    lam4 = jnp.stack([da_lambda_q1[l], da_lambda_k1[l], da_lambda_q2[l], da_lambda_k2[l]]).astype(F32)
    a_out = diff_attention(proj, lam4, da_subln_g[l], batch, seq, layer_idx=l)
    b_out = hgrn2(proj, hgrn_gamma, hgrn_norm_g[l], batch, seq, layer_idx=l)
    wo = w_out[l].astype(BF16)
    h1 = matmul_residual([a_out, b_out], [wo[:da_w], wo[da_w:]], x2)

    qc = norm_matmul(h1, cross_norm_g[l], w_cq[l].astype(BF16))
    kv = norm_matmul(mem.reshape(batch * mem_len, d), mem_norm_g[l], w_ckv[l].astype(BF16))
    oc = cross_attention(qc, kv, batch, seq, mem_len)
    h2 = matmul_residual([oc], [w_co[l].astype(BF16)], h1)

    qp, hn = norm_matmul(h2, ffn_norm_g[l], peer_wq[l].astype(BF16), tn=512, emit_xn=True)
    tables = peer_topk(qp, peer_subkeys[l].astype(BF16))
    peer_t = peer_dense(hn, peer_down[l].astype(BF16), peer_up[l].astype(BF16), tables)
    out = final_norm(h2, peer_t, final_norm_g)
    return out.reshape(batch, seq, d)
```

```python
import functools
import math

import numpy as np
import jax
import jax.numpy as jnp
from jax import lax
from jax.experimental import pallas as pl
from jax.experimental.pallas import tpu as pltpu

F32 = jnp.float32
BF16 = jnp.bfloat16
EPS = 1e-6
NEG_BIG = -1e30

DA_HEADS = 16
DA_HALF = 64
HEAD_W = 128
HG_HEADS = 16
HG_BLOCK = 256
HG_SUB = 8
XA_HEADS = 4
PEER_HEADS = 8
PEER_NKEYS = 128
PEER_TOPK = 16
NOT_SELECTED = 99.0
SUBLANES = 8

VMEM_LIMIT = 56 * 1024 * 1024


def _cparams(sem):
    return pltpu.CompilerParams(dimension_semantics=sem, vmem_limit_bytes=VMEM_LIMIT)


def _nt_dot(a, b):
    return lax.dot_general(a, b, (((1,), (1,)), ((), ())), preferred_element_type=F32)


def _tn_dot(a, b):
    return lax.dot_general(a, b, (((0,), (0,)), ((), ())), preferred_element_type=F32)


def _sigmoid(x):
    return 0.5 * jnp.tanh(0.5 * x) + 0.5


def _norm_matmul_kernel(x_ref, g_ref, w_ref, o_ref, xn_ref, *, rows):
    @pl.when(pl.program_id(1) == 0)
    def _():
        g = g_ref[...]

        def chunk(ci, _):
            r0 = pl.multiple_of(ci * rows, rows)
            x = x_ref[pl.ds(r0, rows), :]
            ms = jnp.mean(x * x, axis=-1, keepdims=True)
            xn_ref[pl.ds(r0, rows), :] = (x * lax.rsqrt(ms + EPS) * g).astype(BF16)
            return 0

        lax.fori_loop(0, x_ref.shape[0] // rows, chunk, 0)

    o_ref[...] = jnp.dot(xn_ref[...], w_ref[...], preferred_element_type=F32).astype(o_ref.dtype)


def norm_matmul(x, g, w, *, tm=512, tn=1024, emit_xn=False):
    m, k = x.shape
    n = w.shape[1]
    tm, tn = min(tm, m), min(tn, n)
    assert m % tm == 0 and n % tn == 0
    rows = min(128, tm)
    out_shape = [jax.ShapeDtypeStruct((m, n), BF16)]
    out_specs = [pl.BlockSpec((tm, tn), lambda i, j: (i, j))]
    scratch = [pltpu.VMEM((tm, k), BF16)]
    if emit_xn:
        out_shape.append(jax.ShapeDtypeStruct((m, k), BF16))
        out_specs.append(pl.BlockSpec((tm, k), lambda i, j: (i, 0)))
        scratch = []
    res = pl.pallas_call(
        functools.partial(_norm_matmul_kernel, rows=rows),
        grid=(m // tm, n // tn),
        in_specs=[pl.BlockSpec((tm, k), lambda i, j: (i, 0)),
                  pl.BlockSpec((1, k), lambda i, j: (0, 0)),
                  pl.BlockSpec((k, tn), lambda i, j: (0, j))],
        out_specs=out_specs,
        out_shape=out_shape,
        scratch_shapes=scratch,
        compiler_params=_cparams(("parallel", "arbitrary")),
        name="norm_matmul",
    )(x, g.reshape(1, k).astype(F32), w)
    return res if emit_xn else res[0]


def _matmul_residual_kernel(*refs, n_lhs):
    lhs = refs[:n_lhs]
    ws = refs[n_lhs:2 * n_lhs]
    res_ref, o_ref = refs[2 * n_lhs], refs[2 * n_lhs + 1]
    acc = res_ref[...]
    for a_ref, w_ref in zip(lhs, ws):
        acc = acc + jnp.dot(a_ref[...], w_ref[...], preferred_element_type=F32)
    o_ref[...] = acc


def matmul_residual(lhs_list, w_list, res, *, tm=512, tn=1024):
    m, n = res.shape
    tm, tn = min(tm, m), min(tn, n)
    assert m % tm == 0 and n % tn == 0
    n_lhs = len(lhs_list)
    in_specs = [pl.BlockSpec((tm, a.shape[1]), lambda i, j: (i, 0)) for a in lhs_list]
    in_specs += [pl.BlockSpec((w.shape[0], tn), lambda i, j: (0, j)) for w in w_list]
    in_specs += [pl.BlockSpec((tm, tn), lambda i, j: (i, j))]
    return pl.pallas_call(
        functools.partial(_matmul_residual_kernel, n_lhs=n_lhs),
        grid=(m // tm, n // tn),
        in_specs=in_specs,
        out_specs=pl.BlockSpec((tm, tn), lambda i, j: (i, j)),
        out_shape=jax.ShapeDtypeStruct((m, n), F32),
        compiler_params=_cparams(("parallel", "parallel")),
        name="matmul_residual",
    )(*lhs_list, *w_list, res)


def _diff_attn_kernel(lam_ref, subg_ref, slope_ref, q_ref, k_ref, vt_ref, o_ref, base_ref, acc_ref, *,
                      tq, tc, lam_init):
    qi = pl.program_id(2)
    ncol = 2 * tq
    per_q = tq // tc
    log2e = math.log2(math.e)
    q = (q_ref[...].astype(F32) * (DA_HALF ** -0.5 * log2e)).astype(BF16)
    lane = lax.broadcasted_iota(jnp.int32, q.shape, 1)
    zero = jnp.zeros_like(q)
    q2 = jnp.concatenate([jnp.where(lane < DA_HALF, q, zero), jnp.where(lane >= DA_HALF, q, zero)], axis=0)
    slope = slope_ref[0:1, 0:1] * log2e
    krow = lax.broadcasted_iota(jnp.int32, (tc, ncol), 0)

    @pl.when(qi == 0)
    def _():
        base_ref[...] = slope * krow.astype(F32)

    acc_ref[...] = jnp.zeros_like(acc_ref)

    def process(j, m, masked):
        rel = j * tc - qi * tq
        off = slope * rel.astype(F32)
        kj = k_ref[pl.ds(pl.multiple_of(j * tc, tc), tc), :]
        s = _nt_dot(kj, q2) + base_ref[...]
        if masked:
            qpos = lax.broadcasted_iota(jnp.int32, (tc, ncol), 1) % tq
            s = jnp.where(krow + rel <= qpos, s, NEG_BIG)
        m_new = jnp.maximum(m, jnp.max(s, axis=0, keepdims=True) + off)
        p = jnp.exp2((s - (m_new - off)).astype(BF16))
        alpha = jnp.exp2(m - m_new)
        acc_ref[...] = acc_ref[...] * alpha + jnp.dot(vt_ref[j], p, preferred_element_type=F32)
        return m_new

    m = jnp.full((1, ncol), NEG_BIG, F32)
    n_full = per_q * qi
    m = lax.fori_loop(0, n_full // 2, lambda it, mm: process(2 * it + 1, process(2 * it, mm, False), False), m)
    m = lax.cond(n_full % 2 == 1, lambda mm: process(n_full - 1, mm, False), lambda mm: mm, m)
    for u in range(per_q):
        m = process(per_q * qi + u, m, True)

    lv = lam_ref[...]
    lam = (jnp.exp(jnp.sum(lv[0:1] * lv[1:2], axis=-1, keepdims=True))
           - jnp.exp(jnp.sum(lv[2:3] * lv[3:4], axis=-1, keepdims=True)) + lam_init)
    on = acc_ref[0:HEAD_W, :] / acc_ref[HEAD_W:HEAD_W + 1, :]
    o = on[:, :tq] - lam * on[:, tq:]
    ms = jnp.mean(o * o, axis=0, keepdims=True)
    o = o * lax.rsqrt(ms + EPS) * subg_ref[...] * (1.0 - lam_init)
    o_ref[...] = o.T.astype(o_ref.dtype)


def diff_attention(proj, lam4, subln_g, batch, seq, *, tq=512, tc=512, layer_idx=0):
    n = proj.shape[0]
    tq = min(tq, seq)
    tc = min(tc, tq)
    nq, nc = seq // tq, seq // tc
    lam_init = 0.8 - 0.6 * math.exp(-0.3 * layer_idx)
    slopes = np.array([2.0 ** (-8.0 * (h + 1) / DA_HEADS) for h in range(DA_HEADS)], dtype=np.float32)
    slope_tab = jnp.asarray(np.broadcast_to(slopes[:, None, None], (DA_HEADS, SUBLANES, HEAD_W)).copy())
    da_w = DA_HEADS * HEAD_W
    vt5 = proj[:, 2 * da_w:3 * da_w].reshape(batch, nc, tc, DA_HEADS, HEAD_W).transpose(0, 3, 1, 4, 2)
    ones_tile = jnp.zeros((2 * SUBLANES, tc), BF16).at[0].set(1)
    vt5 = jnp.concatenate([vt5, jnp.broadcast_to(ones_tile, vt5.shape[:3] + ones_tile.shape)], axis=3)
    vrows = HEAD_W + 2 * SUBLANES
    return pl.pallas_call(
        functools.partial(_diff_attn_kernel, tq=tq, tc=tc, lam_init=lam_init),
        grid=(batch, DA_HEADS, nq),
        in_specs=[pl.BlockSpec((4, DA_HALF), lambda b, h, i: (0, 0)),
                  pl.BlockSpec((HEAD_W, 1), lambda b, h, i: (0, 0)),
                  pl.BlockSpec((None, SUBLANES, HEAD_W), lambda b, h, i: (h, 0, 0)),
                  pl.BlockSpec((tq, HEAD_W), lambda b, h, i: (b * nq + i, h)),
                  pl.BlockSpec((seq, HEAD_W), lambda b, h, i: (b, DA_HEADS + h)),
                  pl.BlockSpec((None, None, nc, vrows, tc), lambda b, h, i: (b, h, 0, 0, 0))],
        out_specs=pl.BlockSpec((tq, HEAD_W), lambda b, h, i: (b * nq + i, h)),
        out_shape=jax.ShapeDtypeStruct((n, da_w), BF16),
        scratch_shapes=[pltpu.VMEM((tc, 2 * tq), F32), pltpu.VMEM((vrows, 2 * tq), F32)],
        compiler_params=_cparams(("parallel", "parallel", "arbitrary")),
        name="diff_attention",
    )(lam4, subln_g.reshape(HEAD_W, 1).astype(F32), slope_tab, proj, proj, vt5)


def _hgrn_chunk(q, k, b, v):
    c = q.shape[0]
    row = lax.broadcasted_iota(jnp.int32, (c, 1), 0)
    arow = lax.broadcasted_iota(jnp.int32, (c, c), 0)
    acol = lax.broadcasted_iota(jnp.int32, (c, c), 1)

    def block_level(half):
        grp = 2 * half
        bg = b.reshape(c // grp, grp, HEAD_W)
        ref = jnp.broadcast_to(bg[:, half - 1:half, :], bg.shape).reshape(c, HEAD_W)
        upper = (row % grp) >= half
        qs = jnp.where(upper, q * jnp.exp2(jnp.minimum(b - ref, 0.0)), 0.0)
        ks = jnp.where(upper, 0.0, k * jnp.exp2(jnp.minimum(ref - b, 0.0)))
        a = _nt_dot(qs.astype(BF16), ks.astype(BF16))
        keep = ((arow // grp) == (acol // grp)) & ((arow % grp) >= half) & ((acol % grp) < half)
        return jnp.where(keep, a, 0.0)

    a_tot = None
    half = c // 2
    while half >= HG_SUB:
        lvl = block_level(half)
        a_tot = lvl if a_tot is None else a_tot + lvl
        half //= 2
    o = jnp.dot(a_tot.astype(BF16), v, preferred_element_type=F32)

    nsub = c // HG_SUB
    q3 = q.reshape(nsub, HG_SUB, HEAD_W)
    k3 = k.reshape(nsub, HG_SUB, HEAD_W)
    b3 = b.reshape(nsub, HG_SUB, HEAD_W)
    v3 = v.astype(F32).reshape(nsub, HG_SUB, HEAD_W)
    trow = lax.broadcasted_iota(jnp.int32, (nsub, HG_SUB, 1), 1)
    o3 = jnp.zeros((nsub, HG_SUB, HEAD_W), F32)
    for s in range(HG_SUB):
        e = jnp.exp2(jnp.minimum(b3 - b3[:, s:s + 1, :], 0.0))
        col = jnp.sum(q3 * (k3[:, s:s + 1, :] * e), axis=-1, keepdims=True)
        o3 = o3 + jnp.where(trow >= s, col, 0.0) * v3[:, s:s + 1, :]
    return o + o3.reshape(c, HEAD_W)


def _hgrn_kernel(gam_ref, ng_ref, q_ref, f_ref, i_ref, g_ref, o_ref, st_ref, *, n_chunks, layer_idx):
    @pl.when(pl.program_id(2) == 0)
    def _():
        st_ref[...] = jnp.zeros_like(st_ref)

    gam = gam_ref[...]
    ge = jnp.exp(gam - jnp.max(gam, axis=0, keepdims=True))
    lb = jnp.sum(ge[0:layer_idx + 1], axis=0, keepdims=True) / jnp.sum(ge, axis=0, keepdims=True)
    ng = ng_ref[...]
    c = q_ref.shape[0] // n_chunks
    tri = (lax.broadcasted_iota(jnp.int32, (c, c), 0) >= lax.broadcasted_iota(jnp.int32, (c, c), 1))
    tri = jnp.where(tri, 1.0, 0.0).astype(BF16)

    def chunk(ci, _):
        r0 = pl.multiple_of(ci * c, c)
        qr = q_ref[pl.ds(r0, c), :].astype(F32)
        fr = f_ref[pl.ds(r0, c), :].astype(F32)
        v = i_ref[pl.ds(r0, c), :]
        gr = g_ref[pl.ds(r0, c), :].astype(F32)
        q = qr * _sigmoid(qr)
        f = lb + (1.0 - lb) * _sigmoid(fr)
        logf = jnp.log2(f)
        k = 1.0 - f
        hi = logf.astype(BF16)
        r1 = logf - hi.astype(F32)
        mid = r1.astype(BF16)
        lo = (r1 - mid.astype(F32)).astype(BF16)
        b = (jnp.dot(tri, hi, preferred_element_type=F32) + jnp.dot(tri, mid, preferred_element_type=F32)
             + jnp.dot(tri, lo, preferred_element_type=F32))
        st = st_ref[...]
        o = _nt_dot((q * jnp.exp2(b)).astype(BF16), st.astype(BF16))
        o = o + _hgrn_chunk(q, k, b, v)
        bend = b[c - 1:c]
        kd = (k * jnp.exp2(bend - b)).astype(BF16)
        st_ref[...] = st * jnp.exp2(bend) + _tn_dot(v, kd)
        ms = jnp.mean(o * o, axis=-1, keepdims=True)
        on = o * lax.rsqrt(ms + EPS) * ng * (gr * _sigmoid(gr))
        o_ref[pl.ds(r0, c), :] = on.astype(o_ref.dtype)
        return 0

    per_trip = max(g for g in (4, 2, 1) if n_chunks % g == 0)

    def trip(i, _):
        for dj in range(per_trip):
            chunk(per_trip * i + dj, 0)
        return 0

    lax.fori_loop(0, n_chunks // per_trip, trip, 0)


def hgrn2(proj, gamma, norm_g, batch, seq, *, tc=1024, layer_idx=0, col0=3 * DA_HEADS):
    n = proj.shape[0]
    tc = min(tc, seq)
    ng = seq // tc
    depth1 = gamma.shape[0]

    def spec(off):
        return pl.BlockSpec((tc, HEAD_W), lambda b, h, g: (b * ng + g, col0 + off * HG_HEADS + h))

    return pl.pallas_call(
        functools.partial(_hgrn_kernel, n_chunks=tc // min(HG_BLOCK, tc), layer_idx=layer_idx),
        grid=(batch, HG_HEADS, ng),
        in_specs=[pl.BlockSpec((depth1, HEAD_W), lambda b, h, g: (0, h)),
                  pl.BlockSpec((1, HEAD_W), lambda b, h, g: (0, 0)),
                  spec(0), spec(1), spec(2), spec(3)],
        out_specs=pl.BlockSpec((tc, HEAD_W), lambda b, h, g: (b * ng + g, h)),
        out_shape=jax.ShapeDtypeStruct((n, HG_HEADS * HEAD_W), BF16),
        scratch_shapes=[pltpu.VMEM((HEAD_W, HEAD_W), F32)],
        compiler_params=_cparams(("parallel", "parallel", "arbitrary")),
        name="hgrn2",
    )(gamma.astype(F32), norm_g.reshape(1, HEAD_W).astype(F32), proj, proj, proj, proj)


def _cross_attn_kernel(q_ref, k_ref, v_ref, o_ref, *, scale):
    q = q_ref[...] * jnp.asarray(scale, BF16)
    s = _nt_dot(q, k_ref[...])
    m = jnp.max(s, axis=-1, keepdims=True)
    p = jnp.exp(s - m)
    l = jnp.sum(p, axis=-1, keepdims=True)
    o = jnp.dot(p.astype(BF16), v_ref[...], preferred_element_type=F32) / l
    o_ref[...] = o.astype(o_ref.dtype)


def cross_attention(q, kv, batch, seq, mem_len, *, tq=512):
    n, d = q.shape
    hd = d // XA_HEADS
    tq = min(tq, seq)
    nq = seq // tq
    return pl.pallas_call(
        functools.partial(_cross_attn_kernel, scale=hd ** -0.5),
        grid=(batch, nq, XA_HEADS),
        in_specs=[pl.BlockSpec((tq, hd), lambda b, i, h: (b * nq + i, h)),
                  pl.BlockSpec((mem_len, hd), lambda b, i, h: (b, h)),
                  pl.BlockSpec((mem_len, hd), lambda b, i, h: (b, XA_HEADS + h))],
        out_specs=pl.BlockSpec((tq, hd), lambda b, i, h: (b * nq + i, h)),
        out_shape=jax.ShapeDtypeStruct((n, d), BF16),
        compiler_params=_cparams(("parallel", "parallel", "parallel")),
        name="cross_attention",
    )(q, kv, kv)


def _topk_desc(s, k):
    rows = s.shape[0]

    def extract(first_only):
        iota = lax.broadcasted_iota(jnp.int32, s.shape, 0)
        rank = jnp.full(s.shape, NOT_SELECTED, F32)
        vals = []
        cur = s
        for r in range(k):
            m = jnp.max(cur, axis=0, keepdims=True)
            sel = cur == m
            if first_only:
                sel = iota == jnp.min(jnp.where(sel, iota, rows), axis=0, keepdims=True)
            rank = jnp.where(sel, float(r), rank)
            cur = jnp.where(sel, -jnp.inf, cur)
            vals.append(m)
        return jnp.concatenate(vals, axis=0), rank, cur

    sv, rank, cur = extract(False)
    removed = jnp.sum(jnp.where(cur == -jnp.inf, 1.0, 0.0), axis=0, keepdims=True)
    return lax.cond(jnp.max(removed) == float(k), lambda: (sv, rank), lambda: extract(True)[:2])


def _peer_topk_kernel(q_ref, keys_ref, l1_ref, c_ref, r2_ref, e2_ref):
    kk = PEER_TOPK
    for h in range(PEER_HEADS):
        q1 = q_ref[:, (2 * h) * PEER_NKEYS:(2 * h + 1) * PEER_NKEYS]
        q2 = q_ref[:, (2 * h + 1) * PEER_NKEYS:(2 * h + 2) * PEER_NKEYS]
        s1 = _nt_dot(keys_ref[h, 0], q1)
        s2 = _nt_dot(keys_ref[h, 1], q2)
        t = s1.shape[1]
        sv12, rank12 = _topk_desc(jnp.concatenate([s1, s2], axis=1), kk)
        sv1, sv2, rank1, rank2 = sv12[:, :t], sv12[:, t:], rank12[:, :t], rank12[:, t:]
        segs, seg_rows = [], []
        for a in range(kk):
            nb = kk // (a + 1)
            if nb > 1:
                rows = -(-nb // SUBLANES) * SUBLANES
                seg = sv1[a:a + 1] + sv2[0:rows]
                if rows != nb:
                    seg = jnp.where(lax.broadcasted_iota(jnp.int32, seg.shape, 0) < nb, seg, -jnp.inf)
                segs.append(seg)
                seg_rows.append(rows)
        single0 = len(segs)
        n_single = kk - single0
        assert n_single % SUBLANES == 0
        segs.append(sv1[single0:kk] + sv2[0:1])
        cand = jnp.concatenate(segs, axis=0)
        best0 = sv1[0:1] + sv2[0:1]

        def pick(first_only, cand=cand, best0=best0):
            iota = lax.broadcasted_iota(jnp.int32, cand.shape, 0)
            picked = jnp.zeros(cand.shape, F32)
            z = jnp.zeros((1, t), F32)
            cur = cand
            for _ in range(kk):
                m = jnp.max(cur, axis=0, keepdims=True)
                sel = cur == m
                if first_only:
                    sel = iota == jnp.min(jnp.where(sel, iota, cand.shape[0]), axis=0, keepdims=True)
                cur = jnp.where(sel, -jnp.inf, cur)
                picked = jnp.where(sel, 1.0, picked)
                z = z + jnp.exp(m - best0)
            return picked, z

        picked, z = pick(False)
        n_picked = jnp.sum(picked, axis=0, keepdims=True)
        picked, z = lax.cond(jnp.max(n_picked) == float(kk), lambda p=picked, zz=z: (p, zz),
                             lambda: pick(True))
        count, r0 = [], 0
        for rows in seg_rows:
            count.append(jnp.sum(picked[r0:r0 + rows], axis=0, keepdims=True))
            r0 += rows
        count += [picked[r0 + i:r0 + i + 1] for i in range(n_single)]
        l1 = jnp.zeros_like(rank1)
        for a in range(kk):
            l1 = jnp.where(rank1 == float(a), count[a], l1)
        sel1 = rank1 < float(kk)
        l1_ref[h] = l1
        c_ref[h] = jnp.where(sel1, jnp.exp(jnp.minimum(s1 - sv1[0:1], 0.0)) / z, 0.0)
        r2_ref[h] = rank2.astype(r2_ref.dtype)
        e2 = jnp.where(rank2 < float(kk), jnp.exp(jnp.minimum(s2 - sv2[0:1], 0.0)), 0.0)
        e2_ref[h] = e2.astype(e2_ref.dtype)


def peer_topk(q, keys, *, tt=256):
    n = q.shape[0]
    tt = min(tt, n)
    tab_spec = pl.BlockSpec((PEER_HEADS, PEER_NKEYS, tt), lambda i: (0, 0, i))
    return pl.pallas_call(
        _peer_topk_kernel,
        grid=(n // tt,),
        in_specs=[pl.BlockSpec((tt, q.shape[1]), lambda i: (i, 0)),
                  pl.BlockSpec(keys.shape, lambda i: (0, 0, 0, 0))],
        out_specs=[tab_spec] * 4,
        out_shape=[jax.ShapeDtypeStruct((PEER_HEADS, PEER_NKEYS, n), dt) for dt in (F32, F32, BF16, BF16)],
        compiler_params=_cparams(("parallel",)),
        name="peer_topk",
    )(q, keys)


def _gelu_exact(x):
    return 0.5 * x * (1.0 + lax.erf(x * (2.0 ** -0.5)))


def _peer_dense_kernel(hn_ref, down_ref, up_ref, l1_ref, c_ref, r2_ref, e2_ref, o_ref, wg_ref, *, groups):
    s = pl.program_id(1)
    row0 = (s * groups) % SUBLANES

    @pl.when(s == 0)
    def _():
        o_ref[...] = jnp.zeros_like(o_ref)

    tt = o_ref.shape[1]
    lane_chunk = 256
    halves = 2
    gh = groups // halves
    wgs = []
    for hf in range(halves):
        er = slice(hf * gh * PEER_NKEYS, (hf + 1) * gh * PEER_NKEYS)
        act = _nt_dot(down_ref[er, :], hn_ref[...])
        for gl in range(gh):
            gi = hf * gh + gl
            rows = slice(gi * PEER_NKEYS, (gi + 1) * PEER_NKEYS)
            lrows = slice(gl * PEER_NKEYS, (gl + 1) * PEER_NKEYS)
            for t0 in range(0, tt, lane_chunk):
                lanes = slice(t0, t0 + lane_chunk)
                g = None
                for h in range(PEER_HEADS):
                    l1 = l1_ref[h, pl.ds(row0 + gi, 1), lanes].astype(BF16)
                    cc = c_ref[h, pl.ds(row0 + gi, 1), lanes].astype(BF16)
                    e2 = e2_ref[h, :, lanes]
                    term = jnp.where(r2_ref[h, :, lanes] < l1, e2 * cc, jnp.zeros_like(e2))
                    g = term if g is None else g + term
                wg_ref[rows, lanes] = g * _gelu_exact(act[lrows, lanes]).astype(BF16)
        wgs.append(wg_ref[er, :])
    acc = o_ref[...]
    for hf in range(halves):
        er = slice(hf * gh * PEER_NKEYS, (hf + 1) * gh * PEER_NKEYS)
        acc = acc + _tn_dot(up_ref[er, :], wgs[hf])
    o_ref[...] = acc


def peer_dense(hn, down, up, tables, *, tt=512, te=512):
    n, d = hn.shape
    n_exp = down.shape[0]
    tt = min(tt, n)
    groups = te // PEER_NKEYS
    nblk = n_exp // te
    l1, c, r2, e2 = tables
    assert SUBLANES % groups == 0 and groups % 2 == 0
    row_spec = pl.BlockSpec((PEER_HEADS, SUBLANES, tt), lambda i, s: (0, s * groups // SUBLANES, i))
    tab_spec = pl.BlockSpec((PEER_HEADS, PEER_NKEYS, tt), lambda i, s: (0, 0, i))
    return pl.pallas_call(
        functools.partial(_peer_dense_kernel, groups=groups),
        grid=(n // tt, nblk),
        in_specs=[pl.BlockSpec((tt, d), lambda i, s: (i, 0)),
                  pl.BlockSpec((te, d), lambda i, s: (s, 0)),
                  pl.BlockSpec((te, d), lambda i, s: (s, 0)),
                  row_spec, row_spec, tab_spec, tab_spec],
        out_specs=pl.BlockSpec((d, tt), lambda i, s: (0, i)),
        out_shape=jax.ShapeDtypeStruct((d, n), F32),
        scratch_shapes=[pltpu.VMEM((te, tt), BF16)],
        compiler_params=_cparams(("parallel", "arbitrary")),
        name="peer_dense",
    )(hn, down, up, l1, c, r2, e2)


def _final_norm_kernel(h_ref, pt_ref, g_ref, o_ref):
    x = h_ref[...] + pt_ref[...].T
    ms = jnp.mean(x * x, axis=-1, keepdims=True)
    o_ref[...] = x * lax.rsqrt(ms + EPS) * g_ref[...]


def final_norm(h, pt, g, *, tm=256):
    m, d = h.shape
    tm = min(tm, m)
    row = pl.BlockSpec((tm, d), lambda i: (i, 0))
    return pl.pallas_call(
        _final_norm_kernel,
        grid=(m // tm,),
        in_specs=[row, pl.BlockSpec((d, tm), lambda i: (0, i)), pl.BlockSpec((1, d), lambda i: (0, 0))],
        out_specs=row,
        out_shape=jax.ShapeDtypeStruct((m, d), F32),
        compiler_params=_cparams(("parallel",)),
        name="final_norm",
    )(h, pt, g.reshape(1, d).astype(F32))


def kernel(x, mem, mix_norm_g, w_in, da_lambda_q1, da_lambda_k1, da_lambda_q2, da_lambda_k2, da_subln_g, hgrn_gamma, hgrn_norm_g, w_out, cross_norm_g, mem_norm_g, w_cq, w_ckv, w_co, ffn_norm_g, peer_wq, peer_subkeys, peer_down, peer_up, final_norm_g):
    batch, seq, d = x.shape
    mem_len = mem.shape[1]
    depth = w_in.shape[0]
    assert depth == 1 and hgrn_gamma.shape[0] == depth + 1
    l = 0
    da_w = DA_HEADS * HEAD_W
    x2 = x.reshape(batch * seq, d)

    proj = norm_matmul(x2, mix_norm_g[l], w_in[l].astype(BF16))
    lam4 = jnp.stack([da_lambda_q1[l], da_lambda_k1[l], da_lambda_q2[l], da_lambda_k2[l]]).astype(F32)
    a_out = diff_attention(proj, lam4, da_subln_g[l], batch, seq, layer_idx=l)
    b_out = hgrn2(proj, hgrn_gamma, hgrn_norm_g[l], batch, seq, layer_idx=l)
    wo = w_out[l].astype(BF16)
    h1 = matmul_residual([a_out, b_out], [wo[:da_w], wo[da_w:]], x2)

    qc = norm_matmul(h1, cross_norm_g[l], w_cq[l].astype(BF16))
    kv = norm_matmul(mem.reshape(batch * mem_len, d), mem_norm_g[l], w_ckv[l].astype(BF16))
    oc = cross_attention(qc, kv, batch, seq, mem_len)
    h2 = matmul_residual([oc], [w_co[l].astype(BF16)], h1)

    qp, hn = norm_matmul(h2, ffn_norm_g[l], peer_wq[l].astype(BF16), tn=512, emit_xn=True)
    tables = peer_topk(qp, peer_subkeys[l].astype(BF16))
    peer_t = peer_dense(hn, peer_down[l].astype(BF16), peer_up[l].astype(BF16), tables)
    out = final_norm(h2, peer_t, final_norm_g)
    return out.reshape(batch, seq, d)
```

```python
import functools
import math

import numpy as np
import jax
import jax.numpy as jnp
from jax import lax
from jax.experimental import pallas as pl
from jax.experimental.pallas import tpu as pltpu

F32 = jnp.float32
BF16 = jnp.bfloat16
EPS = 1e-6
NEG_BIG = -1e30

DA_HEADS = 16
DA_HALF = 64
HEAD_W = 128
HG_HEADS = 16
HG_BLOCK = 256
HG_SUB = 8
XA_HEADS = 4
PEER_HEADS = 8
PEER_NKEYS = 128
PEER_TOPK = 16
NOT_SELECTED = 99.0
SUBLANES = 8

VMEM_LIMIT = 56 * 1024 * 1024


def _cparams(sem):
    return pltpu.CompilerParams(dimension_semantics=sem, vmem_limit_bytes=VMEM_LIMIT)


def _nt_dot(a, b):
    return lax.dot_general(a, b, (((1,), (1,)), ((), ())), preferred_element_type=F32)


def _tn_dot(a, b):
    return lax.dot_general(a, b, (((0,), (0,)), ((), ())), preferred_element_type=F32)


def _sigmoid(x):
    return 0.5 * jnp.tanh(0.5 * x) + 0.5


def _norm_matmul_kernel(x_ref, g_ref, w_ref, o_ref, xn_ref, *, rows):
    @pl.when(pl.program_id(1) == 0)
    def _():
        g = g_ref[...]

        def chunk(ci, _):
            r0 = pl.multiple_of(ci * rows, rows)
            x = x_ref[pl.ds(r0, rows), :]
            ms = jnp.mean(x * x, axis=-1, keepdims=True)
            xn_ref[pl.ds(r0, rows), :] = (x * lax.rsqrt(ms + EPS) * g).astype(BF16)
            return 0

        lax.fori_loop(0, x_ref.shape[0] // rows, chunk, 0)

    o_ref[...] = jnp.dot(xn_ref[...], w_ref[...], preferred_element_type=F32).astype(o_ref.dtype)


def norm_matmul(x, g, w, *, tm=512, tn=1024, emit_xn=False):
    m, k = x.shape
    n = w.shape[1]
    tm, tn = min(tm, m), min(tn, n)
    assert m % tm == 0 and n % tn == 0
    rows = min(128, tm)
    out_shape = [jax.ShapeDtypeStruct((m, n), BF16)]
    out_specs = [pl.BlockSpec((tm, tn), lambda i, j: (i, j))]
    scratch = [pltpu.VMEM((tm, k), BF16)]
    if emit_xn:
        out_shape.append(jax.ShapeDtypeStruct((m, k), BF16))
        out_specs.append(pl.BlockSpec((tm, k), lambda i, j: (i, 0)))
        scratch = []
    res = pl.pallas_call(
        functools.partial(_norm_matmul_kernel, rows=rows),
        grid=(m // tm, n // tn),
        in_specs=[pl.BlockSpec((tm, k), lambda i, j: (i, 0)),
                  pl.BlockSpec((1, k), lambda i, j: (0, 0)),
                  pl.BlockSpec((k, tn), lambda i, j: (0, j))],
        out_specs=out_specs,
        out_shape=out_shape,
        scratch_shapes=scratch,
        compiler_params=_cparams(("parallel", "arbitrary")),
        name="norm_matmul",
    )(x, g.reshape(1, k).astype(F32), w)
    return res if emit_xn else res[0]


def _matmul_residual_kernel(*refs, n_lhs):
    lhs = refs[:n_lhs]
    ws = refs[n_lhs:2 * n_lhs]
    res_ref, o_ref = refs[2 * n_lhs], refs[2 * n_lhs + 1]
    acc = res_ref[...]
    for a_ref, w_ref in zip(lhs, ws):
        acc = acc + jnp.dot(a_ref[...], w_ref[...], preferred_element_type=F32)
    o_ref[...] = acc


def matmul_residual(lhs_list, w_list, res, *, tm=512, tn=1024):
    m, n = res.shape
    tm, tn = min(tm, m), min(tn, n)
    assert m % tm == 0 and n % tn == 0
    n_lhs = len(lhs_list)
    in_specs = [pl.BlockSpec((tm, a.shape[1]), lambda i, j: (i, 0)) for a in lhs_list]
    in_specs += [pl.BlockSpec((w.shape[0], tn), lambda i, j: (0, j)) for w in w_list]
    in_specs += [pl.BlockSpec((tm, tn), lambda i, j: (i, j))]
    return pl.pallas_call(
        functools.partial(_matmul_residual_kernel, n_lhs=n_lhs),
        grid=(m // tm, n // tn),
        in_specs=in_specs,
        out_specs=pl.BlockSpec((tm, tn), lambda i, j: (i, j)),
        out_shape=jax.ShapeDtypeStruct((m, n), F32),
        compiler_params=_cparams(("parallel", "parallel")),
        name="matmul_residual",
    )(*lhs_list, *w_list, res)


def _diff_attn_kernel(lam_ref, subg_ref, slope_ref, q_ref, k_ref, vt_ref, o_ref, base_ref, acc_ref, *,
                      tq, tc, lam_init):
    qi = pl.program_id(2)
    ncol = 2 * tq
    per_q = tq // tc
    log2e = math.log2(math.e)
    q = (q_ref[...].astype(F32) * (DA_HALF ** -0.5 * log2e)).astype(BF16)
    lane = lax.broadcasted_iota(jnp.int32, q.shape, 1)
    zero = jnp.zeros_like(q)
    q2 = jnp.concatenate([jnp.where(lane < DA_HALF, q, zero), jnp.where(lane >= DA_HALF, q, zero)], axis=0)
    slope = slope_ref[0:1, 0:1] * log2e
    krow = lax.broadcasted_iota(jnp.int32, (tc, ncol), 0)

    @pl.when(qi == 0)
    def _():
        base_ref[...] = slope * krow.astype(F32)

    acc_ref[...] = jnp.zeros_like(acc_ref)

    def process(j, m, masked):
        rel = j * tc - qi * tq
        off = slope * rel.astype(F32)
        kj = k_ref[pl.ds(pl.multiple_of(j * tc, tc), tc), :]
        s = _nt_dot(kj, q2) + base_ref[...]
        if masked:
            qpos = lax.broadcasted_iota(jnp.int32, (tc, ncol), 1) % tq
            s = jnp.where(krow + rel <= qpos, s, NEG_BIG)
        m_new = jnp.maximum(m, jnp.max(s, axis=0, keepdims=True) + off)
        p = jnp.exp2((s - (m_new - off)).astype(BF16))
        alpha = jnp.exp2(m - m_new)
        acc_ref[...] = acc_ref[...] * alpha + jnp.dot(vt_ref[j], p, preferred_element_type=F32)
        return m_new

    m = jnp.full((1, ncol), NEG_BIG, F32)
    n_full = per_q * qi
    m = lax.fori_loop(0, n_full // 2, lambda it, mm: process(2 * it + 1, process(2 * it, mm, False), False), m)
    m = lax.cond(n_full % 2 == 1, lambda mm: process(n_full - 1, mm, False), lambda mm: mm, m)
    for u in range(per_q):
        m = process(per_q * qi + u, m, True)

    lv = lam_ref[...]
    lam = (jnp.exp(jnp.sum(lv[0:1] * lv[1:2], axis=-1, keepdims=True))
           - jnp.exp(jnp.sum(lv[2:3] * lv[3:4], axis=-1, keepdims=True)) + lam_init)
    on = acc_ref[0:HEAD_W, :] / acc_ref[HEAD_W:HEAD_W + 1, :]
    o = on[:, :tq] - lam * on[:, tq:]
    ms = jnp.mean(o * o, axis=0, keepdims=True)
    o = o * lax.rsqrt(ms + EPS) * subg_ref[...] * (1.0 - lam_init)
    o_ref[...] = o.T.astype(o_ref.dtype)


def diff_attention(proj, lam4, subln_g, batch, seq, *, tq=512, tc=512, layer_idx=0):
    n = proj.shape[0]
    tq = min(tq, seq)
    tc = min(tc, tq)
    nq, nc = seq // tq, seq // tc
    lam_init = 0.8 - 0.6 * math.exp(-0.3 * layer_idx)
    slopes = np.array([2.0 ** (-8.0 * (h + 1) / DA_HEADS) for h in range(DA_HEADS)], dtype=np.float32)
    slope_tab = jnp.asarray(np.broadcast_to(slopes[:, None, None], (DA_HEADS, SUBLANES, HEAD_W)).copy())
    da_w = DA_HEADS * HEAD_W
    vt5 = proj[:, 2 * da_w:3 * da_w].reshape(batch, nc, tc, DA_HEADS, HEAD_W).transpose(0, 3, 1, 4, 2)
    ones_tile = jnp.zeros((2 * SUBLANES, tc), BF16).at[0].set(1)
    vt5 = jnp.concatenate([vt5, jnp.broadcast_to(ones_tile, vt5.shape[:3] + ones_tile.shape)], axis=3)
    vrows = HEAD_W + 2 * SUBLANES
    return pl.pallas_call(
        functools.partial(_diff_attn_kernel, tq=tq, tc=tc, lam_init=lam_init),
        grid=(batch, DA_HEADS, nq),
        in_specs=[pl.BlockSpec((4, DA_HALF), lambda b, h, i: (0, 0)),
                  pl.BlockSpec((HEAD_W, 1), lambda b, h, i: (0, 0)),
                  pl.BlockSpec((None, SUBLANES, HEAD_W), lambda b, h, i: (h, 0, 0)),
                  pl.BlockSpec((tq, HEAD_W), lambda b, h, i: (b * nq + i, h)),
                  pl.BlockSpec((seq, HEAD_W), lambda b, h, i: (b, DA_HEADS + h)),
                  pl.BlockSpec((None, None, nc, vrows, tc), lambda b, h, i: (b, h, 0, 0, 0))],
        out_specs=pl.BlockSpec((tq, HEAD_W), lambda b, h, i: (b * nq + i, h)),
        out_shape=jax.ShapeDtypeStruct((n, da_w), BF16),
        scratch_shapes=[pltpu.VMEM((tc, 2 * tq), F32), pltpu.VMEM((vrows, 2 * tq), F32)],
        compiler_params=_cparams(("parallel", "parallel", "arbitrary")),
        name="diff_attention",
    )(lam4, subln_g.reshape(HEAD_W, 1).astype(F32), slope_tab, proj, proj, vt5)


def _hgrn_chunk(q, k, b, v):
    c = q.shape[0]
    row = lax.broadcasted_iota(jnp.int32, (c, 1), 0)
    arow = lax.broadcasted_iota(jnp.int32, (c, c), 0)
    acol = lax.broadcasted_iota(jnp.int32, (c, c), 1)

    def block_level(half):
        grp = 2 * half
        bg = b.reshape(c // grp, grp, HEAD_W)
        ref = jnp.broadcast_to(bg[:, half - 1:half, :], bg.shape).reshape(c, HEAD_W)
        upper = (row % grp) >= half
        qs = jnp.where(upper, q * jnp.exp2(jnp.minimum(b - ref, 0.0)), 0.0)
        ks = jnp.where(upper, 0.0, k * jnp.exp2(jnp.minimum(ref - b, 0.0)))
        a = _nt_dot(qs.astype(BF16), ks.astype(BF16))
        keep = ((arow // grp) == (acol // grp)) & ((arow % grp) >= half) & ((acol % grp) < half)
        return jnp.where(keep, a, 0.0)

    a_tot = None
    half = c // 2
    while half >= HG_SUB:
        lvl = block_level(half)
        a_tot = lvl if a_tot is None else a_tot + lvl
        half //= 2
    o = jnp.dot(a_tot.astype(BF16), v, preferred_element_type=F32)

    nsub = c // HG_SUB
    q3 = q.reshape(nsub, HG_SUB, HEAD_W)
    k3 = k.reshape(nsub, HG_SUB, HEAD_W)
    b3 = b.reshape(nsub, HG_SUB, HEAD_W)
    v3 = v.astype(F32).reshape(nsub, HG_SUB, HEAD_W)
    trow = lax.broadcasted_iota(jnp.int32, (nsub, HG_SUB, 1), 1)
    o3 = jnp.zeros((nsub, HG_SUB, HEAD_W), F32)
    for s in range(HG_SUB):
        e = jnp.exp2(jnp.minimum(b3 - b3[:, s:s + 1, :], 0.0))
        col = jnp.sum(q3 * (k3[:, s:s + 1, :] * e), axis=-1, keepdims=True)
        o3 = o3 + jnp.where(trow >= s, col, 0.0) * v3[:, s:s + 1, :]
    return o + o3.reshape(c, HEAD_W)


def _hgrn_kernel(gam_ref, ng_ref, q_ref, f_ref, i_ref, g_ref, o_ref, st_ref, *, n_chunks, layer_idx):
    @pl.when(pl.program_id(2) == 0)
    def _():
        st_ref[...] = jnp.zeros_like(st_ref)

    gam = gam_ref[...]
    ge = jnp.exp(gam - jnp.max(gam, axis=0, keepdims=True))
    lb = jnp.sum(ge[0:layer_idx + 1], axis=0, keepdims=True) / jnp.sum(ge, axis=0, keepdims=True)
    ng = ng_ref[...]
    c = q_ref.shape[0] // n_chunks
    tri = (lax.broadcasted_iota(jnp.int32, (c, c), 0) >= lax.broadcasted_iota(jnp.int32, (c, c), 1))
    tri = jnp.where(tri, 1.0, 0.0).astype(BF16)

    def chunk(ci, _):
        r0 = pl.multiple_of(ci * c, c)
        qr = q_ref[pl.ds(r0, c), :].astype(F32)
        fr = f_ref[pl.ds(r0, c), :].astype(F32)
        v = i_ref[pl.ds(r0, c), :]
        gr = g_ref[pl.ds(r0, c), :].astype(F32)
        q = qr * _sigmoid(qr)
        f = lb + (1.0 - lb) * _sigmoid(fr)
        logf = jnp.log2(f)
        k = 1.0 - f
        hi = logf.astype(BF16)
        r1 = logf - hi.astype(F32)
        mid = r1.astype(BF16)
        lo = (r1 - mid.astype(F32)).astype(BF16)
        b = (jnp.dot(tri, hi, preferred_element_type=F32) + jnp.dot(tri, mid, preferred_element_type=F32)
             + jnp.dot(tri, lo, preferred_element_type=F32))
        st = st_ref[...]
        o = _nt_dot((q * jnp.exp2(b)).astype(BF16), st.astype(BF16))
        o = o + _hgrn_chunk(q, k, b, v)
        bend = b[c - 1:c]
        kd = (k * jnp.exp2(bend - b)).astype(BF16)
        st_ref[...] = st * jnp.exp2(bend) + _tn_dot(v, kd)
        ms = jnp.mean(o * o, axis=-1, keepdims=True)
        on = o * lax.rsqrt(ms + EPS) * ng * (gr * _sigmoid(gr))
        o_ref[pl.ds(r0, c), :] = on.astype(o_ref.dtype)
        return 0

    per_trip = max(g for g in (4, 2, 1) if n_chunks % g == 0)

    def trip(i, _):
        for dj in range(per_trip):
            chunk(per_trip * i + dj, 0)
        return 0

    lax.fori_loop(0, n_chunks // per_trip, trip, 0)


def hgrn2(proj, gamma, norm_g, batch, seq, *, tc=1024, layer_idx=0, col0=3 * DA_HEADS):
    n = proj.shape[0]
    tc = min(tc, seq)
    ng = seq // tc
    depth1 = gamma.shape[0]

    def spec(off):
        return pl.BlockSpec((tc, HEAD_W), lambda b, h, g: (b * ng + g, col0 + off * HG_HEADS + h))

    return pl.pallas_call(
        functools.partial(_hgrn_kernel, n_chunks=tc // min(HG_BLOCK, tc), layer_idx=layer_idx),
        grid=(batch, HG_HEADS, ng),
        in_specs=[pl.BlockSpec((depth1, HEAD_W), lambda b, h, g: (0, h)),
                  pl.BlockSpec((1, HEAD_W), lambda b, h, g: (0, 0)),
                  spec(0), spec(1), spec(2), spec(3)],
        out_specs=pl.BlockSpec((tc, HEAD_W), lambda b, h, g: (b * ng + g, h)),
        out_shape=jax.ShapeDtypeStruct((n, HG_HEADS * HEAD_W), BF16),
        scratch_shapes=[pltpu.VMEM((HEAD_W, HEAD_W), F32)],
        compiler_params=_cparams(("parallel", "parallel", "arbitrary")),
        name="hgrn2",
    )(gamma.astype(F32), norm_g.reshape(1, HEAD_W).astype(F32), proj, proj, proj, proj)


def _cross_attn_kernel(q_ref, kv_ref, o_ref, *, scale):
    d = q_ref.shape[1]
    hd = d // XA_HEADS
    for h in range(XA_HEADS):
        cols = slice(h * hd, (h + 1) * hd)
        q = q_ref[:, cols] * jnp.asarray(scale, BF16)
        s = _nt_dot(q, kv_ref[:, cols])
        m = jnp.max(s, axis=-1, keepdims=True)
        p = jnp.exp(s - m)
        l = jnp.sum(p, axis=-1, keepdims=True)
        o = jnp.dot(p.astype(BF16), kv_ref[:, d + h * hd:d + (h + 1) * hd], preferred_element_type=F32) / l
        o_ref[:, cols] = o.astype(o_ref.dtype)


def cross_attention(q, kv, batch, seq, mem_len, *, tq=512):
    n, d = q.shape
    tq = min(tq, seq)
    nq = seq // tq
    return pl.pallas_call(
        functools.partial(_cross_attn_kernel, scale=(d // XA_HEADS) ** -0.5),
        grid=(batch, nq),
        in_specs=[pl.BlockSpec((tq, d), lambda b, i: (b * nq + i, 0)),
                  pl.BlockSpec((mem_len, 2 * d), lambda b, i: (b, 0))],
        out_specs=pl.BlockSpec((tq, d), lambda b, i: (b * nq + i, 0)),
        out_shape=jax.ShapeDtypeStruct((n, d), BF16),
        compiler_params=_cparams(("parallel", "parallel")),
        name="cross_attention",
    )(q, kv)


def _topk_desc(s, k):
    rows = s.shape[0]

    def extract(first_only):
        iota = lax.broadcasted_iota(jnp.int32, s.shape, 0)
        rank = jnp.full(s.shape, NOT_SELECTED, F32)
        vals = []
        cur = s
        for r in range(k):
            m = jnp.max(cur, axis=0, keepdims=True)
            sel = cur == m
            if first_only:
                sel = iota == jnp.min(jnp.where(sel, iota, rows), axis=0, keepdims=True)
            rank = jnp.where(sel, float(r), rank)
            cur = jnp.where(sel, -jnp.inf, cur)
            vals.append(m)
        return jnp.concatenate(vals, axis=0), rank, cur

    sv, rank, cur = extract(False)
    removed = jnp.sum(jnp.where(cur == -jnp.inf, 1.0, 0.0), axis=0, keepdims=True)
    return lax.cond(jnp.max(removed) == float(k), lambda: (sv, rank), lambda: extract(True)[:2])


def _peer_topk_kernel(q_ref, keys_ref, l1_ref, c_ref, r2_ref, e2_ref):
    kk = PEER_TOPK
    for h in range(PEER_HEADS):
        q1 = q_ref[:, (2 * h) * PEER_NKEYS:(2 * h + 1) * PEER_NKEYS]
        q2 = q_ref[:, (2 * h + 1) * PEER_NKEYS:(2 * h + 2) * PEER_NKEYS]
        s1 = _nt_dot(keys_ref[h, 0], q1)
        s2 = _nt_dot(keys_ref[h, 1], q2)
        t = s1.shape[1]
        sv12, rank12 = _topk_desc(jnp.concatenate([s1, s2], axis=1), kk)
        sv1, sv2, rank1, rank2 = sv12[:, :t], sv12[:, t:], rank12[:, :t], rank12[:, t:]
        segs, seg_rows = [], []
        for a in range(kk):
            nb = kk // (a + 1)
            if nb > 1:
                rows = -(-nb // SUBLANES) * SUBLANES
                seg = sv1[a:a + 1] + sv2[0:rows]
                if rows != nb:
                    seg = jnp.where(lax.broadcasted_iota(jnp.int32, seg.shape, 0) < nb, seg, -jnp.inf)
                segs.append(seg)
                seg_rows.append(rows)
        single0 = len(segs)
        n_single = kk - single0
        assert n_single % SUBLANES == 0
        segs.append(sv1[single0:kk] + sv2[0:1])
        cand = jnp.concatenate(segs, axis=0)
        best0 = sv1[0:1] + sv2[0:1]

        def pick(first_only, cand=cand, best0=best0):
            iota = lax.broadcasted_iota(jnp.int32, cand.shape, 0)
            picked = jnp.zeros(cand.shape, F32)
            z = jnp.zeros((1, t), F32)
            cur = cand
            for _ in range(kk):
                m = jnp.max(cur, axis=0, keepdims=True)
                sel = cur == m
                if first_only:
                    sel = iota == jnp.min(jnp.where(sel, iota, cand.shape[0]), axis=0, keepdims=True)
                cur = jnp.where(sel, -jnp.inf, cur)
                picked = jnp.where(sel, 1.0, picked)
                z = z + jnp.exp(m - best0)
            return picked, z

        picked, z = pick(False)
        n_picked = jnp.sum(picked, axis=0, keepdims=True)
        picked, z = lax.cond(jnp.max(n_picked) == float(kk), lambda p=picked, zz=z: (p, zz),
                             lambda: pick(True))
        count, r0 = [], 0
        for rows in seg_rows:
            count.append(jnp.sum(picked[r0:r0 + rows], axis=0, keepdims=True))
            r0 += rows
        count += [picked[r0 + i:r0 + i + 1] for i in range(n_single)]
        l1 = jnp.zeros_like(rank1)
        for a in range(kk):
            l1 = jnp.where(rank1 == float(a), count[a], l1)
        sel1 = rank1 < float(kk)
        l1_ref[h] = l1
        c_ref[h] = jnp.where(sel1, jnp.exp(jnp.minimum(s1 - sv1[0:1], 0.0)) / z, 0.0)
        r2_ref[h] = rank2.astype(r2_ref.dtype)
        e2 = jnp.where(rank2 < float(kk), jnp.exp(jnp.minimum(s2 - sv2[0:1], 0.0)), 0.0)
        e2_ref[h] = e2.astype(e2_ref.dtype)


def peer_topk(q, keys, *, tt=256):
    n = q.shape[0]
    tt = min(tt, n)
    tab_spec = pl.BlockSpec((PEER_HEADS, PEER_NKEYS, tt), lambda i: (0, 0, i))
    return pl.pallas_call(
        _peer_topk_kernel,
        grid=(n // tt,),
        in_specs=[pl.BlockSpec((tt, q.shape[1]), lambda i: (i, 0)),
                  pl.BlockSpec(keys.shape, lambda i: (0, 0, 0, 0))],
        out_specs=[tab_spec] * 4,
        out_shape=[jax.ShapeDtypeStruct((PEER_HEADS, PEER_NKEYS, n), dt) for dt in (F32, F32, BF16, BF16)],
        compiler_params=_cparams(("parallel",)),
        name="peer_topk",
    )(q, keys)


def _gelu_exact(x):
    return 0.5 * x * (1.0 + lax.erf(x * (2.0 ** -0.5)))


def _peer_dense_kernel(hn_ref, down_ref, up_ref, l1_ref, c_ref, r2_ref, e2_ref, o_ref, wg_ref, *, groups):
    s = pl.program_id(1)
    row0 = (s * groups) % SUBLANES

    @pl.when(s == 0)
    def _():
        o_ref[...] = jnp.zeros_like(o_ref)

    tt = o_ref.shape[1]
    lane_chunk = 256
    halves = 2
    gh = groups // halves
    wgs = []
    for hf in range(halves):
        er = slice(hf * gh * PEER_NKEYS, (hf + 1) * gh * PEER_NKEYS)
        act = _nt_dot(down_ref[er, :], hn_ref[...])
        for gl in range(gh):
            gi = hf * gh + gl
            rows = slice(gi * PEER_NKEYS, (gi + 1) * PEER_NKEYS)
            lrows = slice(gl * PEER_NKEYS, (gl + 1) * PEER_NKEYS)
            for t0 in range(0, tt, lane_chunk):
                lanes = slice(t0, t0 + lane_chunk)
                g = None
                for h in range(PEER_HEADS):
                    l1 = l1_ref[h, pl.ds(row0 + gi, 1), lanes].astype(BF16)
                    cc = c_ref[h, pl.ds(row0 + gi, 1), lanes].astype(BF16)
                    e2 = e2_ref[h, :, lanes]
                    term = jnp.where(r2_ref[h, :, lanes] < l1, e2 * cc, jnp.zeros_like(e2))
                    g = term if g is None else g + term
                wg_ref[rows, lanes] = g * _gelu_exact(act[lrows, lanes]).astype(BF16)
        wgs.append(wg_ref[er, :])
    acc = o_ref[...]
    for hf in range(halves):
        er = slice(hf * gh * PEER_NKEYS, (hf + 1) * gh * PEER_NKEYS)
        acc = acc + _tn_dot(up_ref[er, :], wgs[hf])
    o_ref[...] = acc


def peer_dense(hn, down, up, tables, *, tt=512, te=512):
    n, d = hn.shape
    n_exp = down.shape[0]
    tt = min(tt, n)
    groups = te // PEER_NKEYS
    nblk = n_exp // te
    l1, c, r2, e2 = tables
    assert SUBLANES % groups == 0 and groups % 2 == 0
    row_spec = pl.BlockSpec((PEER_HEADS, SUBLANES, tt), lambda i, s: (0, s * groups // SUBLANES, i))
    tab_spec = pl.BlockSpec((PEER_HEADS, PEER_NKEYS, tt), lambda i, s: (0, 0, i))
    return pl.pallas_call(
        functools.partial(_peer_dense_kernel, groups=groups),
        grid=(n // tt, nblk),
        in_specs=[pl.BlockSpec((tt, d), lambda i, s: (i, 0)),
                  pl.BlockSpec((te, d), lambda i, s: (s, 0)),
                  pl.BlockSpec((te, d), lambda i, s: (s, 0)),
                  row_spec, row_spec, tab_spec, tab_spec],
        out_specs=pl.BlockSpec((d, tt), lambda i, s: (0, i)),
        out_shape=jax.ShapeDtypeStruct((d, n), F32),
        scratch_shapes=[pltpu.VMEM((te, tt), BF16)],
        compiler_params=_cparams(("parallel", "arbitrary")),
        name="peer_dense",
    )(hn, down, up, l1, c, r2, e2)


def _final_norm_kernel(h_ref, pt_ref, g_ref, o_ref):
    x = h_ref[...] + pt_ref[...].T
    ms = jnp.mean(x * x, axis=-1, keepdims=True)
    o_ref[...] = x * lax.rsqrt(ms + EPS) * g_ref[...]


def final_norm(h, pt, g, *, tm=256):
    m, d = h.shape
    tm = min(tm, m)
    row = pl.BlockSpec((tm, d), lambda i: (i, 0))
    return pl.pallas_call(
        _final_norm_kernel,
        grid=(m // tm,),
        in_specs=[row, pl.BlockSpec((d, tm), lambda i: (0, i)), pl.BlockSpec((1, d), lambda i: (0, 0))],
        out_specs=row,
        out_shape=jax.ShapeDtypeStruct((m, d), F32),
        compiler_params=_cparams(("parallel",)),
        name="final_norm",
    )(h, pt, g.reshape(1, d).astype(F32))


def kernel(x, mem, mix_norm_g, w_in, da_lambda_q1, da_lambda_k1, da_lambda_q2, da_lambda_k2, da_subln_g, hgrn_gamma, hgrn_norm_g, w_out, cross_norm_g, mem_norm_g, w_cq, w_ckv, w_co, ffn_norm_g, peer_wq, peer_subkeys, peer_down, peer_up, final_norm_g):
    batch, seq, d = x.shape
    mem_len = mem.shape[1]
    depth = w_in.shape[0]
    assert depth == 1 and hgrn_gamma.shape[0] == depth + 1
    l = 0
    da_w = DA_HEADS * HEAD_W
    x2 = x.reshape(batch * seq, d)

    proj = norm_matmul(x2, mix_norm_g[l], w_in[l].astype(BF16))
    lam4 = jnp.stack([da_lambda_q1[l], da_lambda_k1[l], da_lambda_q2[l], da_lambda_k2[l]]).astype(F32)
    a_out = diff_attention(proj, lam4, da_subln_g[l], batch, seq, layer_idx=l)
    b_out = hgrn2(proj, hgrn_gamma, hgrn_norm_g[l], batch, seq, layer_idx=l)
    wo = w_out[l].astype(BF16)
    h1 = matmul_residual([a_out, b_out], [wo[:da_w], wo[da_w:]], x2)

    qc = norm_matmul(h1, cross_norm_g[l], w_cq[l].astype(BF16))
    kv = norm_matmul(mem.reshape(batch * mem_len, d), mem_norm_g[l], w_ckv[l].astype(BF16))
    oc = cross_attention(qc, kv, batch, seq, mem_len)
    h2 = matmul_residual([oc], [w_co[l].astype(BF16)], h1)

    qp, hn = norm_matmul(h2, ffn_norm_g[l], peer_wq[l].astype(BF16), emit_xn=True)
    tables = peer_topk(qp, peer_subkeys[l].astype(BF16))
    peer_t = peer_dense(hn, peer_down[l].astype(BF16), peer_up[l].astype(BF16), tables)
    out = final_norm(h2, peer_t, final_norm_g)
    return out.reshape(batch, seq, d)
```

```python
import functools
import math

import numpy as np
import jax
import jax.numpy as jnp
from jax import lax
from jax.experimental import pallas as pl
from jax.experimental.pallas import tpu as pltpu

F32 = jnp.float32
BF16 = jnp.bfloat16
EPS = 1e-6
NEG_BIG = -1e30

DA_HEADS = 16
DA_HALF = 64
HEAD_W = 128
HG_HEADS = 16
HG_BLOCK = 256
HG_SUB = 8
XA_HEADS = 4
PEER_HEADS = 8
PEER_NKEYS = 128
PEER_TOPK = 16
NOT_SELECTED = 99.0
SUBLANES = 8

VMEM_LIMIT = 56 * 1024 * 1024


def _cparams(sem):
    return pltpu.CompilerParams(dimension_semantics=sem, vmem_limit_bytes=VMEM_LIMIT)


def _nt_dot(a, b):
    return lax.dot_general(a, b, (((1,), (1,)), ((), ())), preferred_element_type=F32)


def _tn_dot(a, b):
    return lax.dot_general(a, b, (((0,), (0,)), ((), ())), preferred_element_type=F32)


def _sigmoid(x):
    return 0.5 * jnp.tanh(0.5 * x) + 0.5


def _norm_matmul_kernel(x_ref, g_ref, w_ref, o_ref, xn_ref, *, rows):
    @pl.when(pl.program_id(1) == 0)
    def _():
        g = g_ref[...]

        def chunk(ci, _):
            r0 = pl.multiple_of(ci * rows, rows)
            x = x_ref[pl.ds(r0, rows), :]
            ms = jnp.mean(x * x, axis=-1, keepdims=True)
            xn_ref[pl.ds(r0, rows), :] = (x * lax.rsqrt(ms + EPS) * g).astype(BF16)
            return 0

        lax.fori_loop(0, x_ref.shape[0] // rows, chunk, 0)

    o_ref[...] = jnp.dot(xn_ref[...], w_ref[...], preferred_element_type=F32).astype(o_ref.dtype)


def norm_matmul(x, g, w, *, tm=512, tn=1024, emit_xn=False):
    m, k = x.shape
    n = w.shape[1]
    tm, tn = min(tm, m), min(tn, n)
    assert m % tm == 0 and n % tn == 0
    rows = min(128, tm)
    out_shape = [jax.ShapeDtypeStruct((m, n), BF16)]
    out_specs = [pl.BlockSpec((tm, tn), lambda i, j: (i, j))]
    scratch = [pltpu.VMEM((tm, k), BF16)]
    if emit_xn:
        out_shape.append(jax.ShapeDtypeStruct((m, k), BF16))
        out_specs.append(pl.BlockSpec((tm, k), lambda i, j: (i, 0)))
        scratch = []
    res = pl.pallas_call(
        functools.partial(_norm_matmul_kernel, rows=rows),
        grid=(m // tm, n // tn),
        in_specs=[pl.BlockSpec((tm, k), lambda i, j: (i, 0)),
                  pl.BlockSpec((1, k), lambda i, j: (0, 0)),
                  pl.BlockSpec((k, tn), lambda i, j: (0, j))],
        out_specs=out_specs,
        out_shape=out_shape,
        scratch_shapes=scratch,
        compiler_params=_cparams(("parallel", "arbitrary")),
        name="norm_matmul",
    )(x, g.reshape(1, k).astype(F32), w)
    return res if emit_xn else res[0]


def _matmul_residual_kernel(*refs, n_lhs):
    lhs = refs[:n_lhs]
    ws = refs[n_lhs:2 * n_lhs]
    res_ref, o_ref = refs[2 * n_lhs], refs[2 * n_lhs + 1]
    acc = res_ref[...]
    for a_ref, w_ref in zip(lhs, ws):
        acc = acc + jnp.dot(a_ref[...], w_ref[...], preferred_element_type=F32)
    o_ref[...] = acc


def matmul_residual(lhs_list, w_list, res, *, tm=1024, tn=1024):
    m, n = res.shape
    tm, tn = min(tm, m), min(tn, n)
    assert m % tm == 0 and n % tn == 0
    n_lhs = len(lhs_list)
    in_specs = [pl.BlockSpec((tm, a.shape[1]), lambda i, j: (i, 0)) for a in lhs_list]
    in_specs += [pl.BlockSpec((w.shape[0], tn), lambda i, j: (0, j)) for w in w_list]
    in_specs += [pl.BlockSpec((tm, tn), lambda i, j: (i, j))]
    return pl.pallas_call(
        functools.partial(_matmul_residual_kernel, n_lhs=n_lhs),
        grid=(m // tm, n // tn),
        in_specs=in_specs,
        out_specs=pl.BlockSpec((tm, tn), lambda i, j: (i, j)),
        out_shape=jax.ShapeDtypeStruct((m, n), F32),
        compiler_params=_cparams(("parallel", "parallel")),
        name="matmul_residual",
    )(*lhs_list, *w_list, res)


def _diff_attn_kernel(lam_ref, subg_ref, slope_ref, q_ref, k_ref, vt_ref, o_ref, base_ref, acc_ref, *,
                      tq, tc, lam_init):
    qi = pl.program_id(2)
    ncol = 2 * tq
    per_q = tq // tc
    log2e = math.log2(math.e)
    q = (q_ref[...].astype(F32) * (DA_HALF ** -0.5 * log2e)).astype(BF16)
    lane = lax.broadcasted_iota(jnp.int32, q.shape, 1)
    zero = jnp.zeros_like(q)
    q2 = jnp.concatenate([jnp.where(lane < DA_HALF, q, zero), jnp.where(lane >= DA_HALF, q, zero)], axis=0)
    slope = slope_ref[0:1, 0:1] * log2e
    krow = lax.broadcasted_iota(jnp.int32, (tc, ncol), 0)

    @pl.when(qi == 0)
    def _():
        base_ref[...] = slope * krow.astype(F32)

    acc_ref[...] = jnp.zeros_like(acc_ref)

    def process(j, m, masked):
        rel = j * tc - qi * tq
        off = slope * rel.astype(F32)
        kj = k_ref[pl.ds(pl.multiple_of(j * tc, tc), tc), :]
        s = _nt_dot(kj, q2) + base_ref[...]
        if masked:
            qpos = lax.broadcasted_iota(jnp.int32, (tc, ncol), 1) % tq
            s = jnp.where(krow + rel <= qpos, s, NEG_BIG)
        m_new = jnp.maximum(m, jnp.max(s, axis=0, keepdims=True) + off)
        p = jnp.exp2((s - (m_new - off)).astype(BF16))
        alpha = jnp.exp2(m - m_new)
        acc_ref[...] = acc_ref[...] * alpha + jnp.dot(vt_ref[j], p, preferred_element_type=F32)
        return m_new

    m = jnp.full((1, ncol), NEG_BIG, F32)
    n_full = per_q * qi
    m = lax.fori_loop(0, n_full // 2, lambda it, mm: process(2 * it + 1, process(2 * it, mm, False), False), m)
    m = lax.cond(n_full % 2 == 1, lambda mm: process(n_full - 1, mm, False), lambda mm: mm, m)
    for u in range(per_q):
        m = process(per_q * qi + u, m, True)

    lv = lam_ref[...]
    lam = (jnp.exp(jnp.sum(lv[0:1] * lv[1:2], axis=-1, keepdims=True))
           - jnp.exp(jnp.sum(lv[2:3] * lv[3:4], axis=-1, keepdims=True)) + lam_init)
    on = acc_ref[0:HEAD_W, :] / acc_ref[HEAD_W:HEAD_W + 1, :]
    o = on[:, :tq] - lam * on[:, tq:]
    ms = jnp.mean(o * o, axis=0, keepdims=True)
    o = o * lax.rsqrt(ms + EPS) * subg_ref[...] * (1.0 - lam_init)
    o_ref[...] = o.T.astype(o_ref.dtype)


def diff_attention(proj, lam4, subln_g, batch, seq, *, tq=512, tc=512, layer_idx=0):
    n = proj.shape[0]
    tq = min(tq, seq)
    tc = min(tc, tq)
    nq, nc = seq // tq, seq // tc
    lam_init = 0.8 - 0.6 * math.exp(-0.3 * layer_idx)
    slopes = np.array([2.0 ** (-8.0 * (h + 1) / DA_HEADS) for h in range(DA_HEADS)], dtype=np.float32)
    slope_tab = jnp.asarray(np.broadcast_to(slopes[:, None, None], (DA_HEADS, SUBLANES, HEAD_W)).copy())
    da_w = DA_HEADS * HEAD_W
    vt5 = proj[:, 2 * da_w:3 * da_w].reshape(batch, nc, tc, DA_HEADS, HEAD_W).transpose(0, 3, 1, 4, 2)
    ones_tile = jnp.zeros((2 * SUBLANES, tc), BF16).at[0].set(1)
    vt5 = jnp.concatenate([vt5, jnp.broadcast_to(ones_tile, vt5.shape[:3] + ones_tile.shape)], axis=3)
    vrows = HEAD_W + 2 * SUBLANES
    return pl.pallas_call(
        functools.partial(_diff_attn_kernel, tq=tq, tc=tc, lam_init=lam_init),
        grid=(batch, DA_HEADS, nq),
        in_specs=[pl.BlockSpec((4, DA_HALF), lambda b, h, i: (0, 0)),
                  pl.BlockSpec((HEAD_W, 1), lambda b, h, i: (0, 0)),
                  pl.BlockSpec((None, SUBLANES, HEAD_W), lambda b, h, i: (h, 0, 0)),
                  pl.BlockSpec((tq, HEAD_W), lambda b, h, i: (b * nq + i, h)),
                  pl.BlockSpec((seq, HEAD_W), lambda b, h, i: (b, DA_HEADS + h)),
                  pl.BlockSpec((None, None, nc, vrows, tc), lambda b, h, i: (b, h, 0, 0, 0))],
        out_specs=pl.BlockSpec((tq, HEAD_W), lambda b, h, i: (b * nq + i, h)),
        out_shape=jax.ShapeDtypeStruct((n, da_w), BF16),
        scratch_shapes=[pltpu.VMEM((tc, 2 * tq), F32), pltpu.VMEM((vrows, 2 * tq), F32)],
        compiler_params=_cparams(("parallel", "parallel", "arbitrary")),
        name="diff_attention",
    )(lam4, subln_g.reshape(HEAD_W, 1).astype(F32), slope_tab, proj, proj, vt5)


def _hgrn_chunk(q, k, b, v):
    c = q.shape[0]
    row = lax.broadcasted_iota(jnp.int32, (c, 1), 0)
    arow = lax.broadcasted_iota(jnp.int32, (c, c), 0)
    acol = lax.broadcasted_iota(jnp.int32, (c, c), 1)

    def block_level(half):
        grp = 2 * half
        bg = b.reshape(c // grp, grp, HEAD_W)
        ref = jnp.broadcast_to(bg[:, half - 1:half, :], bg.shape).reshape(c, HEAD_W)
        upper = (row % grp) >= half
        qs = jnp.where(upper, q * jnp.exp2(jnp.minimum(b - ref, 0.0)), 0.0)
        ks = jnp.where(upper, 0.0, k * jnp.exp2(jnp.minimum(ref - b, 0.0)))
        a = _nt_dot(qs.astype(BF16), ks.astype(BF16))
        keep = ((arow // grp) == (acol // grp)) & ((arow % grp) >= half) & ((acol % grp) < half)
        return jnp.where(keep, a, 0.0)

    a_tot = None
    half = c // 2
    while half >= HG_SUB:
        lvl = block_level(half)
        a_tot = lvl if a_tot is None else a_tot + lvl
        half //= 2
    o = jnp.dot(a_tot.astype(BF16), v, preferred_element_type=F32)

    nsub = c // HG_SUB
    q3 = q.reshape(nsub, HG_SUB, HEAD_W)
    k3 = k.reshape(nsub, HG_SUB, HEAD_W)
    b3 = b.reshape(nsub, HG_SUB, HEAD_W)
    v3 = v.astype(F32).reshape(nsub, HG_SUB, HEAD_W)
    trow = lax.broadcasted_iota(jnp.int32, (nsub, HG_SUB, 1), 1)
    o3 = jnp.zeros((nsub, HG_SUB, HEAD_W), F32)
    for s in range(HG_SUB):
        e = jnp.exp2(jnp.minimum(b3 - b3[:, s:s + 1, :], 0.0))
        col = jnp.sum(q3 * (k3[:, s:s + 1, :] * e), axis=-1, keepdims=True)
        o3 = o3 + jnp.where(trow >= s, col, 0.0) * v3[:, s:s + 1, :]
    return o + o3.reshape(c, HEAD_W)


def _hgrn_kernel(gam_ref, ng_ref, q_ref, f_ref, i_ref, g_ref, o_ref, st_ref, *, n_chunks, layer_idx):
    @pl.when(pl.program_id(2) == 0)
    def _():
        st_ref[...] = jnp.zeros_like(st_ref)

    gam = gam_ref[...]
    ge = jnp.exp(gam - jnp.max(gam, axis=0, keepdims=True))
    lb = jnp.sum(ge[0:layer_idx + 1], axis=0, keepdims=True) / jnp.sum(ge, axis=0, keepdims=True)
    ng = ng_ref[...]
    c = q_ref.shape[0] // n_chunks
    tri = (lax.broadcasted_iota(jnp.int32, (c, c), 0) >= lax.broadcasted_iota(jnp.int32, (c, c), 1))
    tri = jnp.where(tri, 1.0, 0.0).astype(BF16)

    def chunk(ci, _):
        r0 = pl.multiple_of(ci * c, c)
        qr = q_ref[pl.ds(r0, c), :].astype(F32)
        fr = f_ref[pl.ds(r0, c), :].astype(F32)
        v = i_ref[pl.ds(r0, c), :]
        gr = g_ref[pl.ds(r0, c), :].astype(F32)
        q = qr * _sigmoid(qr)
        f = lb + (1.0 - lb) * _sigmoid(fr)
        logf = jnp.log2(f)
        k = 1.0 - f
        hi = logf.astype(BF16)
        r1 = logf - hi.astype(F32)
        mid = r1.astype(BF16)
        lo = (r1 - mid.astype(F32)).astype(BF16)
        b = (jnp.dot(tri, hi, preferred_element_type=F32) + jnp.dot(tri, mid, preferred_element_type=F32)
             + jnp.dot(tri, lo, preferred_element_type=F32))
        st = st_ref[...]
        o = _nt_dot((q * jnp.exp2(b)).astype(BF16), st.astype(BF16))
        o = o + _hgrn_chunk(q, k, b, v)
        bend = b[c - 1:c]
        kd = (k * jnp.exp2(bend - b)).astype(BF16)
        st_ref[...] = st * jnp.exp2(bend) + _tn_dot(v, kd)
        ms = jnp.mean(o * o, axis=-1, keepdims=True)
        on = o * lax.rsqrt(ms + EPS) * ng * (gr * _sigmoid(gr))
        o_ref[pl.ds(r0, c), :] = on.astype(o_ref.dtype)
        return 0

    per_trip = max(g for g in (4, 2, 1) if n_chunks % g == 0)

    def trip(i, _):
        for dj in range(per_trip):
            chunk(per_trip * i + dj, 0)
        return 0

    lax.fori_loop(0, n_chunks // per_trip, trip, 0)


def hgrn2(proj, gamma, norm_g, batch, seq, *, tc=1024, layer_idx=0, col0=3 * DA_HEADS):
    n = proj.shape[0]
    tc = min(tc, seq)
    ng = seq // tc
    depth1 = gamma.shape[0]

    def spec(off):
        return pl.BlockSpec((tc, HEAD_W), lambda b, h, g: (b * ng + g, col0 + off * HG_HEADS + h))

    return pl.pallas_call(
        functools.partial(_hgrn_kernel, n_chunks=tc // min(HG_BLOCK, tc), layer_idx=layer_idx),
        grid=(batch, HG_HEADS, ng),
        in_specs=[pl.BlockSpec((depth1, HEAD_W), lambda b, h, g: (0, h)),
                  pl.BlockSpec((1, HEAD_W), lambda b, h, g: (0, 0)),
                  spec(0), spec(1), spec(2), spec(3)],
        out_specs=pl.BlockSpec((tc, HEAD_W), lambda b, h, g: (b * ng + g, h)),
        out_shape=jax.ShapeDtypeStruct((n, HG_HEADS * HEAD_W), BF16),
        scratch_shapes=[pltpu.VMEM((HEAD_W, HEAD_W), F32)],
        compiler_params=_cparams(("parallel", "parallel", "arbitrary")),
        name="hgrn2",
    )(gamma.astype(F32), norm_g.reshape(1, HEAD_W).astype(F32), proj, proj, proj, proj)


def _cross_attn_kernel(q_ref, kv_ref, o_ref, *, scale):
    d = q_ref.shape[1]
    hd = d // XA_HEADS
    for h in range(XA_HEADS):
        cols = slice(h * hd, (h + 1) * hd)
        q = q_ref[:, cols] * jnp.asarray(scale, BF16)
        s = _nt_dot(q, kv_ref[:, cols])
        m = jnp.max(s, axis=-1, keepdims=True)
        p = jnp.exp(s - m)
        l = jnp.sum(p, axis=-1, keepdims=True)
        o = jnp.dot(p.astype(BF16), kv_ref[:, d + h * hd:d + (h + 1) * hd], preferred_element_type=F32) / l
        o_ref[:, cols] = o.astype(o_ref.dtype)


def cross_attention(q, kv, batch, seq, mem_len, *, tq=512):
    n, d = q.shape
    tq = min(tq, seq)
    nq = seq // tq
    return pl.pallas_call(
        functools.partial(_cross_attn_kernel, scale=(d // XA_HEADS) ** -0.5),
        grid=(batch, nq),
        in_specs=[pl.BlockSpec((tq, d), lambda b, i: (b * nq + i, 0)),
                  pl.BlockSpec((mem_len, 2 * d), lambda b, i: (b, 0))],
        out_specs=pl.BlockSpec((tq, d), lambda b, i: (b * nq + i, 0)),
        out_shape=jax.ShapeDtypeStruct((n, d), BF16),
        compiler_params=_cparams(("parallel", "parallel")),
        name="cross_attention",
    )(q, kv)


def _topk_desc(s, k):
    rows = s.shape[0]

    def extract(first_only):
        iota = lax.broadcasted_iota(jnp.int32, s.shape, 0)
        rank = jnp.full(s.shape, NOT_SELECTED, F32)
        vals = []
        cur = s
        for r in range(k):
            m = jnp.max(cur, axis=0, keepdims=True)
            sel = cur == m
            if first_only:
                sel = iota == jnp.min(jnp.where(sel, iota, rows), axis=0, keepdims=True)
            rank = jnp.where(sel, float(r), rank)
            cur = jnp.where(sel, -jnp.inf, cur)
            vals.append(m)
        return jnp.concatenate(vals, axis=0), rank, cur

    sv, rank, cur = extract(False)
    removed = jnp.sum(jnp.where(cur == -jnp.inf, 1.0, 0.0), axis=0, keepdims=True)
    return lax.cond(jnp.max(removed) == float(k), lambda: (sv, rank), lambda: extract(True)[:2])


def _peer_topk_kernel(q_ref, keys_ref, l1_ref, c_ref, r2_ref, e2_ref):
    kk = PEER_TOPK
    for h in range(PEER_HEADS):
        q1 = q_ref[:, (2 * h) * PEER_NKEYS:(2 * h + 1) * PEER_NKEYS]
        q2 = q_ref[:, (2 * h + 1) * PEER_NKEYS:(2 * h + 2) * PEER_NKEYS]
        s1 = _nt_dot(keys_ref[h, 0], q1)
        s2 = _nt_dot(keys_ref[h, 1], q2)
        t = s1.shape[1]
        sv12, rank12 = _topk_desc(jnp.concatenate([s1, s2], axis=1), kk)
        sv1, sv2, rank1, rank2 = sv12[:, :t], sv12[:, t:], rank12[:, :t], rank12[:, t:]
        segs, seg_rows = [], []
        for a in range(kk):
            nb = kk // (a + 1)
            if nb > 1:
                rows = -(-nb // SUBLANES) * SUBLANES
                seg = sv1[a:a + 1] + sv2[0:rows]
                if rows != nb:
                    seg = jnp.where(lax.broadcasted_iota(jnp.int32, seg.shape, 0) < nb, seg, -jnp.inf)
                segs.append(seg)
                seg_rows.append(rows)
        single0 = len(segs)
        n_single = kk - single0
        assert n_single % SUBLANES == 0
        segs.append(sv1[single0:kk] + sv2[0:1])
        cand = jnp.concatenate(segs, axis=0)
        best0 = sv1[0:1] + sv2[0:1]

        def pick(first_only, cand=cand, best0=best0):
            iota = lax.broadcasted_iota(jnp.int32, cand.shape, 0)
            picked = jnp.zeros(cand.shape, F32)
            z = jnp.zeros((1, t), F32)
            cur = cand
            for _ in range(kk):
                m = jnp.max(cur, axis=0, keepdims=True)
                sel = cur == m
                if first_only:
                    sel = iota == jnp.min(jnp.where(sel, iota, cand.shape[0]), axis=0, keepdims=True)
                cur = jnp.where(sel, -jnp.inf, cur)
                picked = jnp.where(sel, 1.0, picked)
                z = z + jnp.exp(m - best0)
            return picked, z

        picked, z = pick(False)
        n_picked = jnp.sum(picked, axis=0, keepdims=True)
        picked, z = lax.cond(jnp.max(n_picked) == float(kk), lambda p=picked, zz=z: (p, zz),
                             lambda: pick(True))
        count, r0 = [], 0
        for rows in seg_rows:
            count.append(jnp.sum(picked[r0:r0 + rows], axis=0, keepdims=True))
            r0 += rows
        count += [picked[r0 + i:r0 + i + 1] for i in range(n_single)]
        l1 = jnp.zeros_like(rank1)
        for a in range(kk):
            l1 = jnp.where(rank1 == float(a), count[a], l1)
        sel1 = rank1 < float(kk)
        l1_ref[h] = l1
        c_ref[h] = jnp.where(sel1, jnp.exp(jnp.minimum(s1 - sv1[0:1], 0.0)) / z, 0.0)
        r2_ref[h] = rank2.astype(r2_ref.dtype)
        e2 = jnp.where(rank2 < float(kk), jnp.exp(jnp.minimum(s2 - sv2[0:1], 0.0)), 0.0)
        e2_ref[h] = e2.astype(e2_ref.dtype)


def peer_topk(q, keys, *, tt=256):
    n = q.shape[0]
    tt = min(tt, n)
    tab_spec = pl.BlockSpec((PEER_HEADS, PEER_NKEYS, tt), lambda i: (0, 0, i))
    return pl.pallas_call(
        _peer_topk_kernel,
        grid=(n // tt,),
        in_specs=[pl.BlockSpec((tt, q.shape[1]), lambda i: (i, 0)),
                  pl.BlockSpec(keys.shape, lambda i: (0, 0, 0, 0))],
        out_specs=[tab_spec] * 4,
        out_shape=[jax.ShapeDtypeStruct((PEER_HEADS, PEER_NKEYS, n), dt) for dt in (F32, F32, BF16, BF16)],
        compiler_params=_cparams(("parallel",)),
        name="peer_topk",
    )(q, keys)


def _gelu_exact(x):
    return 0.5 * x * (1.0 + lax.erf(x * (2.0 ** -0.5)))


def _peer_dense_kernel(hn_ref, down_ref, up_ref, l1_ref, c_ref, r2_ref, e2_ref, o_ref, wg_ref, *, groups):
    s = pl.program_id(1)
    row0 = (s * groups) % SUBLANES

    @pl.when(s == 0)
    def _():
        o_ref[...] = jnp.zeros_like(o_ref)

    tt = o_ref.shape[1]
    lane_chunk = 256
    halves = 2
    gh = groups // halves
    wgs = []
    for hf in range(halves):
        er = slice(hf * gh * PEER_NKEYS, (hf + 1) * gh * PEER_NKEYS)
        act = _nt_dot(down_ref[er, :], hn_ref[...])
        for gl in range(gh):
            gi = hf * gh + gl
            rows = slice(gi * PEER_NKEYS, (gi + 1) * PEER_NKEYS)
            lrows = slice(gl * PEER_NKEYS, (gl + 1) * PEER_NKEYS)
            for t0 in range(0, tt, lane_chunk):
                lanes = slice(t0, t0 + lane_chunk)
                g = None
                for h in range(PEER_HEADS):
                    l1 = l1_ref[h, pl.ds(row0 + gi, 1), lanes].astype(BF16)
                    cc = c_ref[h, pl.ds(row0 + gi, 1), lanes].astype(BF16)
                    e2 = e2_ref[h, :, lanes]
                    term = jnp.where(r2_ref[h, :, lanes] < l1, e2 * cc, jnp.zeros_like(e2))
                    g = term if g is None else g + term
                wg_ref[rows, lanes] = g * _gelu_exact(act[lrows, lanes]).astype(BF16)
        wgs.append(wg_ref[er, :])
    acc = o_ref[...]
    for hf in range(halves):
        er = slice(hf * gh * PEER_NKEYS, (hf + 1) * gh * PEER_NKEYS)
        acc = acc + _tn_dot(up_ref[er, :], wgs[hf])
    o_ref[...] = acc


def peer_dense(hn, down, up, tables, *, tt=512, te=512):
    n, d = hn.shape
    n_exp = down.shape[0]
    tt = min(tt, n)
    groups = te // PEER_NKEYS
    nblk = n_exp // te
    l1, c, r2, e2 = tables
    assert SUBLANES % groups == 0 and groups % 2 == 0
    row_spec = pl.BlockSpec((PEER_HEADS, SUBLANES, tt), lambda i, s: (0, s * groups // SUBLANES, i))
    tab_spec = pl.BlockSpec((PEER_HEADS, PEER_NKEYS, tt), lambda i, s: (0, 0, i))
    return pl.pallas_call(
        functools.partial(_peer_dense_kernel, groups=groups),
        grid=(n // tt, nblk),
        in_specs=[pl.BlockSpec((tt, d), lambda i, s: (i, 0)),
                  pl.BlockSpec((te, d), lambda i, s: (s, 0)),
                  pl.BlockSpec((te, d), lambda i, s: (s, 0)),
                  row_spec, row_spec, tab_spec, tab_spec],
        out_specs=pl.BlockSpec((d, tt), lambda i, s: (0, i)),
        out_shape=jax.ShapeDtypeStruct((d, n), F32),
        scratch_shapes=[pltpu.VMEM((te, tt), BF16)],
        compiler_params=_cparams(("parallel", "arbitrary")),
        name="peer_dense",
    )(hn, down, up, l1, c, r2, e2)


def _final_norm_kernel(h_ref, pt_ref, g_ref, o_ref):
    x = h_ref[...] + pt_ref[...].T
    ms = jnp.mean(x * x, axis=-1, keepdims=True)
    o_ref[...] = x * lax.rsqrt(ms + EPS) * g_ref[...]


def final_norm(h, pt, g, *, tm=256):
    m, d = h.shape
    tm = min(tm, m)
    row = pl.BlockSpec((tm, d), lambda i: (i, 0))
    return pl.pallas_call(
        _final_norm_kernel,
        grid=(m // tm,),
        in_specs=[row, pl.BlockSpec((d, tm), lambda i: (0, i)), pl.BlockSpec((1, d), lambda i: (0, 0))],
        out_specs=row,
        out_shape=jax.ShapeDtypeStruct((m, d), F32),
        compiler_params=_cparams(("parallel",)),
        name="final_norm",
    )(h, pt, g.reshape(1, d).astype(F32))


def kernel(x, mem, mix_norm_g, w_in, da_lambda_q1, da_lambda_k1, da_lambda_q2, da_lambda_k2, da_subln_g, hgrn_gamma, hgrn_norm_g, w_out, cross_norm_g, mem_norm_g, w_cq, w_ckv, w_co, ffn_norm_g, peer_wq, peer_subkeys, peer_down, peer_up, final_norm_g):
    batch, seq, d = x.shape
    mem_len = mem.shape[1]
    depth = w_in.shape[0]
    assert depth == 1 and hgrn_gamma.shape[0] == depth + 1
    l = 0
    da_w = DA_HEADS * HEAD_W
    x2 = x.reshape(batch * seq, d)

    proj = norm_matmul(x2, mix_norm_g[l], w_in[l].astype(BF16))
    lam4 = jnp.stack([da_lambda_q1[l], da_lambda_k1[l], da_lambda_q2[l], da_lambda_k2[l]]).astype(F32)
    a_out = diff_attention(proj, lam4, da_subln_g[l], batch, seq, layer_idx=l)
    b_out = hgrn2(proj, hgrn_gamma, hgrn_norm_g[l], batch, seq, layer_idx=l)
    wo = w_out[l].astype(BF16)
    h1 = matmul_residual([a_out, b_out], [wo[:da_w], wo[da_w:]], x2)

    qc = norm_matmul(h1, cross_norm_g[l], w_cq[l].astype(BF16))
    kv = norm_matmul(mem.reshape(batch * mem_len, d), mem_norm_g[l], w_ckv[l].astype(BF16))
    oc = cross_attention(qc, kv, batch, seq, mem_len)
    h2 = matmul_residual([oc], [w_co[l].astype(BF16)], h1)

    qp, hn = norm_matmul(h2, ffn_norm_g[l], peer_wq[l].astype(BF16), emit_xn=True)
    tables = peer_topk(qp, peer_subkeys[l].astype(BF16))
    peer_t = peer_dense(hn, peer_down[l].astype(BF16), peer_up[l].astype(BF16), tables)
    out = final_norm(h2, peer_t, final_norm_g)
    return out.reshape(batch, seq, d)
```

```python
import functools
import math

import numpy as np
import jax
import jax.numpy as jnp
from jax import lax
from jax.experimental import pallas as pl
from jax.experimental.pallas import tpu as pltpu

F32 = jnp.float32
BF16 = jnp.bfloat16
EPS = 1e-6
NEG_BIG = -1e30

DA_HEADS = 16
DA_HALF = 64
HEAD_W = 128
HG_HEADS = 16
HG_BLOCK = 256
HG_SUB = 8
XA_HEADS = 4
PEER_HEADS = 8
PEER_NKEYS = 128
PEER_TOPK = 16
NOT_SELECTED = 99.0
SUBLANES = 8

VMEM_LIMIT = 56 * 1024 * 1024


def _cparams(sem):
    return pltpu.CompilerParams(dimension_semantics=sem, vmem_limit_bytes=VMEM_LIMIT)


def _nt_dot(a, b):
    return lax.dot_general(a, b, (((1,), (1,)), ((), ())), preferred_element_type=F32)


def _tn_dot(a, b):
    return lax.dot_general(a, b, (((0,), (0,)), ((), ())), preferred_element_type=F32)


def _sigmoid(x):
    return 0.5 * jnp.tanh(0.5 * x) + 0.5


def _norm_matmul_kernel(x_ref, g_ref, w_ref, o_ref, xn_ref, *, rows):
    @pl.when(pl.program_id(1) == 0)
    def _():
        g = g_ref[...]

        def chunk(ci, _):
            r0 = pl.multiple_of(ci * rows, rows)
            x = x_ref[pl.ds(r0, rows), :]
            ms = jnp.mean(x * x, axis=-1, keepdims=True)
            xn_ref[pl.ds(r0, rows), :] = (x * lax.rsqrt(ms + EPS) * g).astype(BF16)
            return 0

        lax.fori_loop(0, x_ref.shape[0] // rows, chunk, 0)

    o_ref[...] = jnp.dot(xn_ref[...], w_ref[...], preferred_element_type=F32).astype(o_ref.dtype)


def norm_matmul(x, g, w, *, tm=1024, tn=1024, emit_xn=False):
    m, k = x.shape
    n = w.shape[1]
    tm, tn = min(tm, m), min(tn, n)
    assert m % tm == 0 and n % tn == 0
    rows = min(128, tm)
    out_shape = [jax.ShapeDtypeStruct((m, n), BF16)]
    out_specs = [pl.BlockSpec((tm, tn), lambda i, j: (i, j))]
    scratch = [pltpu.VMEM((tm, k), BF16)]
    if emit_xn:
        out_shape.append(jax.ShapeDtypeStruct((m, k), BF16))
        out_specs.append(pl.BlockSpec((tm, k), lambda i, j: (i, 0)))
        scratch = []
    res = pl.pallas_call(
        functools.partial(_norm_matmul_kernel, rows=rows),
        grid=(m // tm, n // tn),
        in_specs=[pl.BlockSpec((tm, k), lambda i, j: (i, 0), pipeline_mode=pl.Buffered(1)),
                  pl.BlockSpec((1, k), lambda i, j: (0, 0)),
                  pl.BlockSpec((k, tn), lambda i, j: (0, j))],
        out_specs=out_specs,
        out_shape=out_shape,
        scratch_shapes=scratch,
        compiler_params=_cparams(("parallel", "arbitrary")),
        name="norm_matmul",
    )(x, g.reshape(1, k).astype(F32), w)
    return res if emit_xn else res[0]


def _matmul_residual_kernel(*refs, n_lhs):
    lhs = refs[:n_lhs]
    ws = refs[n_lhs:2 * n_lhs]
    res_ref, o_ref = refs[2 * n_lhs], refs[2 * n_lhs + 1]
    acc = res_ref[...]
    for a_ref, w_ref in zip(lhs, ws):
        acc = acc + jnp.dot(a_ref[...], w_ref[...], preferred_element_type=F32)
    o_ref[...] = acc


def matmul_residual(lhs_list, w_list, res, *, tm=1024, tn=1024):
    m, n = res.shape
    tm, tn = min(tm, m), min(tn, n)
    assert m % tm == 0 and n % tn == 0
    n_lhs = len(lhs_list)
    in_specs = [pl.BlockSpec((tm, a.shape[1]), lambda i, j: (i, 0)) for a in lhs_list]
    in_specs += [pl.BlockSpec((w.shape[0], tn), lambda i, j: (0, j)) for w in w_list]
    in_specs += [pl.BlockSpec((tm, tn), lambda i, j: (i, j))]
    return pl.pallas_call(
        functools.partial(_matmul_residual_kernel, n_lhs=n_lhs),
        grid=(m // tm, n // tn),
        in_specs=in_specs,
        out_specs=pl.BlockSpec((tm, tn), lambda i, j: (i, j)),
        out_shape=jax.ShapeDtypeStruct((m, n), F32),
        compiler_params=_cparams(("parallel", "parallel")),
        name="matmul_residual",
    )(*lhs_list, *w_list, res)


def _diff_attn_kernel(lam_ref, subg_ref, slope_ref, q_ref, k_ref, vt_ref, o_ref, base_ref, acc_ref, *,
                      tq, tc, lam_init):
    qi = pl.program_id(2)
    ncol = 2 * tq
    per_q = tq // tc
    log2e = math.log2(math.e)
    q = (q_ref[...].astype(F32) * (DA_HALF ** -0.5 * log2e)).astype(BF16)
    lane = lax.broadcasted_iota(jnp.int32, q.shape, 1)
    zero = jnp.zeros_like(q)
    q2 = jnp.concatenate([jnp.where(lane < DA_HALF, q, zero), jnp.where(lane >= DA_HALF, q, zero)], axis=0)
    slope = slope_ref[0:1, 0:1] * log2e
    krow = lax.broadcasted_iota(jnp.int32, (tc, ncol), 0)

    @pl.when(qi == 0)
    def _():
        base_ref[...] = slope * krow.astype(F32)

    acc_ref[...] = jnp.zeros_like(acc_ref)

    def process(j, m, masked):
        rel = j * tc - qi * tq
        off = slope * rel.astype(F32)
        kj = k_ref[pl.ds(pl.multiple_of(j * tc, tc), tc), :]
        s = _nt_dot(kj, q2) + base_ref[...]
        if masked:
            qpos = lax.broadcasted_iota(jnp.int32, (tc, ncol), 1) % tq
            s = jnp.where(krow + rel <= qpos, s, NEG_BIG)
        m_new = jnp.maximum(m, jnp.max(s, axis=0, keepdims=True) + off)
        p = jnp.exp2((s - (m_new - off)).astype(BF16))
        alpha = jnp.exp2(m - m_new)
        acc_ref[...] = acc_ref[...] * alpha + jnp.dot(vt_ref[j], p, preferred_element_type=F32)
        return m_new

    m = jnp.full((1, ncol), NEG_BIG, F32)
    n_full = per_q * qi
    m = lax.fori_loop(0, n_full // 2, lambda it, mm: process(2 * it + 1, process(2 * it, mm, False), False), m)
    m = lax.cond(n_full % 2 == 1, lambda mm: process(n_full - 1, mm, False), lambda mm: mm, m)
    for u in range(per_q):
        m = process(per_q * qi + u, m, True)

    lv = lam_ref[...]
    lam = (jnp.exp(jnp.sum(lv[0:1] * lv[1:2], axis=-1, keepdims=True))
           - jnp.exp(jnp.sum(lv[2:3] * lv[3:4], axis=-1, keepdims=True)) + lam_init)
    on = acc_ref[0:HEAD_W, :] / acc_ref[HEAD_W:HEAD_W + 1, :]
    o = on[:, :tq] - lam * on[:, tq:]
    ms = jnp.mean(o * o, axis=0, keepdims=True)
    o = o * lax.rsqrt(ms + EPS) * subg_ref[...] * (1.0 - lam_init)
    o_ref[...] = o.T.astype(o_ref.dtype)


def diff_attention(proj, lam4, subln_g, batch, seq, *, tq=512, tc=512, layer_idx=0):
    n = proj.shape[0]
    tq = min(tq, seq)
    tc = min(tc, tq)
    nq, nc = seq // tq, seq // tc
    lam_init = 0.8 - 0.6 * math.exp(-0.3 * layer_idx)
    slopes = np.array([2.0 ** (-8.0 * (h + 1) / DA_HEADS) for h in range(DA_HEADS)], dtype=np.float32)
    slope_tab = jnp.asarray(np.broadcast_to(slopes[:, None, None], (DA_HEADS, SUBLANES, HEAD_W)).copy())
    da_w = DA_HEADS * HEAD_W
    vt5 = proj[:, 2 * da_w:3 * da_w].reshape(batch, nc, tc, DA_HEADS, HEAD_W).transpose(0, 3, 1, 4, 2)
    ones_tile = jnp.zeros((2 * SUBLANES, tc), BF16).at[0].set(1)
    vt5 = jnp.concatenate([vt5, jnp.broadcast_to(ones_tile, vt5.shape[:3] + ones_tile.shape)], axis=3)
    vrows = HEAD_W + 2 * SUBLANES
    return pl.pallas_call(
        functools.partial(_diff_attn_kernel, tq=tq, tc=tc, lam_init=lam_init),
        grid=(batch, DA_HEADS, nq),
        in_specs=[pl.BlockSpec((4, DA_HALF), lambda b, h, i: (0, 0)),
                  pl.BlockSpec((HEAD_W, 1), lambda b, h, i: (0, 0)),
                  pl.BlockSpec((None, SUBLANES, HEAD_W), lambda b, h, i: (h, 0, 0)),
                  pl.BlockSpec((tq, HEAD_W), lambda b, h, i: (b * nq + i, h)),
                  pl.BlockSpec((seq, HEAD_W), lambda b, h, i: (b, DA_HEADS + h)),
                  pl.BlockSpec((None, None, nc, vrows, tc), lambda b, h, i: (b, h, 0, 0, 0))],
        out_specs=pl.BlockSpec((tq, HEAD_W), lambda b, h, i: (b * nq + i, h)),
        out_shape=jax.ShapeDtypeStruct((n, da_w), BF16),
        scratch_shapes=[pltpu.VMEM((tc, 2 * tq), F32), pltpu.VMEM((vrows, 2 * tq), F32)],
        compiler_params=_cparams(("parallel", "parallel", "arbitrary")),
        name="diff_attention",
    )(lam4, subln_g.reshape(HEAD_W, 1).astype(F32), slope_tab, proj, proj, vt5)


def _hgrn_chunk(q, k, b, v):
    c = q.shape[0]
    row = lax.broadcasted_iota(jnp.int32, (c, 1), 0)
    arow = lax.broadcasted_iota(jnp.int32, (c, c), 0)
    acol = lax.broadcasted_iota(jnp.int32, (c, c), 1)

    def block_level(half):
        grp = 2 * half
        bg = b.reshape(c // grp, grp, HEAD_W)
        ref = jnp.broadcast_to(bg[:, half - 1:half, :], bg.shape).reshape(c, HEAD_W)
        upper = (row % grp) >= half
        qs = jnp.where(upper, q * jnp.exp2(jnp.minimum(b - ref, 0.0)), 0.0)
        ks = jnp.where(upper, 0.0, k * jnp.exp2(jnp.minimum(ref - b, 0.0)))
        a = _nt_dot(qs.astype(BF16), ks.astype(BF16))
        keep = ((arow // grp) == (acol // grp)) & ((arow % grp) >= half) & ((acol % grp) < half)
        return jnp.where(keep, a, 0.0)

    a_tot = None
    half = c // 2
    while half >= HG_SUB:
        lvl = block_level(half)
        a_tot = lvl if a_tot is None else a_tot + lvl
        half //= 2
    o = jnp.dot(a_tot.astype(BF16), v, preferred_element_type=F32)

    nsub = c // HG_SUB
    q3 = q.reshape(nsub, HG_SUB, HEAD_W)
    k3 = k.reshape(nsub, HG_SUB, HEAD_W)
    b3 = b.reshape(nsub, HG_SUB, HEAD_W)
    v3 = v.astype(F32).reshape(nsub, HG_SUB, HEAD_W)
    trow = lax.broadcasted_iota(jnp.int32, (nsub, HG_SUB, 1), 1)
    o3 = jnp.zeros((nsub, HG_SUB, HEAD_W), F32)
    for s in range(HG_SUB):
        e = jnp.exp2(jnp.minimum(b3 - b3[:, s:s + 1, :], 0.0))
        col = jnp.sum(q3 * (k3[:, s:s + 1, :] * e), axis=-1, keepdims=True)
        o3 = o3 + jnp.where(trow >= s, col, 0.0) * v3[:, s:s + 1, :]
    return o + o3.reshape(c, HEAD_W)


def _hgrn_kernel(gam_ref, ng_ref, q_ref, f_ref, i_ref, g_ref, o_ref, st_ref, *, n_chunks, layer_idx):
    @pl.when(pl.program_id(2) == 0)
    def _():
        st_ref[...] = jnp.zeros_like(st_ref)

    gam = gam_ref[...]
    ge = jnp.exp(gam - jnp.max(gam, axis=0, keepdims=True))
    lb = jnp.sum(ge[0:layer_idx + 1], axis=0, keepdims=True) / jnp.sum(ge, axis=0, keepdims=True)
    ng = ng_ref[...]
    c = q_ref.shape[0] // n_chunks
    tri = (lax.broadcasted_iota(jnp.int32, (c, c), 0) >= lax.broadcasted_iota(jnp.int32, (c, c), 1))
    tri = jnp.where(tri, 1.0, 0.0).astype(BF16)

    def chunk(ci, _):
        r0 = pl.multiple_of(ci * c, c)
        qr = q_ref[pl.ds(r0, c), :].astype(F32)
        fr = f_ref[pl.ds(r0, c), :].astype(F32)
        v = i_ref[pl.ds(r0, c), :]
        gr = g_ref[pl.ds(r0, c), :].astype(F32)
        q = qr * _sigmoid(qr)
        f = lb + (1.0 - lb) * _sigmoid(fr)
        logf = jnp.log2(f)
        k = 1.0 - f
        hi = logf.astype(BF16)
        r1 = logf - hi.astype(F32)
        mid = r1.astype(BF16)
        lo = (r1 - mid.astype(F32)).astype(BF16)
        b = (jnp.dot(tri, hi, preferred_element_type=F32) + jnp.dot(tri, mid, preferred_element_type=F32)
             + jnp.dot(tri, lo, preferred_element_type=F32))
        st = st_ref[...]
        o = _nt_dot((q * jnp.exp2(b)).astype(BF16), st.astype(BF16))
        o = o + _hgrn_chunk(q, k, b, v)
        bend = b[c - 1:c]
        kd = (k * jnp.exp2(bend - b)).astype(BF16)
        st_ref[...] = st * jnp.exp2(bend) + _tn_dot(v, kd)
        ms = jnp.mean(o * o, axis=-1, keepdims=True)
        on = o * lax.rsqrt(ms + EPS) * ng * (gr * _sigmoid(gr))
        o_ref[pl.ds(r0, c), :] = on.astype(o_ref.dtype)
        return 0

    per_trip = max(g for g in (4, 2, 1) if n_chunks % g == 0)

    def trip(i, _):
        for dj in range(per_trip):
            chunk(per_trip * i + dj, 0)
        return 0

    lax.fori_loop(0, n_chunks // per_trip, trip, 0)


def hgrn2(proj, gamma, norm_g, batch, seq, *, tc=1024, layer_idx=0, col0=3 * DA_HEADS):
    n = proj.shape[0]
    tc = min(tc, seq)
    ng = seq // tc
    depth1 = gamma.shape[0]

    def spec(off):
        return pl.BlockSpec((tc, HEAD_W), lambda b, h, g: (b * ng + g, col0 + off * HG_HEADS + h))

    return pl.pallas_call(
        functools.partial(_hgrn_kernel, n_chunks=tc // min(HG_BLOCK, tc), layer_idx=layer_idx),
        grid=(batch, HG_HEADS, ng),
        in_specs=[pl.BlockSpec((depth1, HEAD_W), lambda b, h, g: (0, h)),
                  pl.BlockSpec((1, HEAD_W), lambda b, h, g: (0, 0)),
                  spec(0), spec(1), spec(2), spec(3)],
        out_specs=pl.BlockSpec((tc, HEAD_W), lambda b, h, g: (b * ng + g, h)),
        out_shape=jax.ShapeDtypeStruct((n, HG_HEADS * HEAD_W), BF16),
        scratch_shapes=[pltpu.VMEM((HEAD_W, HEAD_W), F32)],
        compiler_params=_cparams(("parallel", "parallel", "arbitrary")),
        name="hgrn2",
    )(gamma.astype(F32), norm_g.reshape(1, HEAD_W).astype(F32), proj, proj, proj, proj)


def _cross_attn_kernel(q_ref, kv_ref, o_ref, *, scale):
    d = q_ref.shape[1]
    hd = d // XA_HEADS
    for h in range(XA_HEADS):
        cols = slice(h * hd, (h + 1) * hd)
        q = q_ref[:, cols] * jnp.asarray(scale, BF16)
        s = _nt_dot(q, kv_ref[:, cols])
        m = jnp.max(s, axis=-1, keepdims=True)
        p = jnp.exp(s - m)
        l = jnp.sum(p, axis=-1, keepdims=True)
        o = jnp.dot(p.astype(BF16), kv_ref[:, d + h * hd:d + (h + 1) * hd], preferred_element_type=F32) / l
        o_ref[:, cols] = o.astype(o_ref.dtype)


def cross_attention(q, kv, batch, seq, mem_len, *, tq=512):
    n, d = q.shape
    tq = min(tq, seq)
    nq = seq // tq
    return pl.pallas_call(
        functools.partial(_cross_attn_kernel, scale=(d // XA_HEADS) ** -0.5),
        grid=(batch, nq),
        in_specs=[pl.BlockSpec((tq, d), lambda b, i: (b * nq + i, 0)),
                  pl.BlockSpec((mem_len, 2 * d), lambda b, i: (b, 0))],
        out_specs=pl.BlockSpec((tq, d), lambda b, i: (b * nq + i, 0)),
        out_shape=jax.ShapeDtypeStruct((n, d), BF16),
        compiler_params=_cparams(("parallel", "parallel")),
        name="cross_attention",
    )(q, kv)


def _topk_desc(s, k):
    rows = s.shape[0]

    def extract(first_only):
        iota = lax.broadcasted_iota(jnp.int32, s.shape, 0)
        rank = jnp.full(s.shape, NOT_SELECTED, F32)
        vals = []
        cur = s
        for r in range(k):
            m = jnp.max(cur, axis=0, keepdims=True)
            sel = cur == m
            if first_only:
                sel = iota == jnp.min(jnp.where(sel, iota, rows), axis=0, keepdims=True)
            rank = jnp.where(sel, float(r), rank)
            cur = jnp.where(sel, -jnp.inf, cur)
            vals.append(m)
        return jnp.concatenate(vals, axis=0), rank, cur

    sv, rank, cur = extract(False)
    removed = jnp.sum(jnp.where(cur == -jnp.inf, 1.0, 0.0), axis=0, keepdims=True)
    return lax.cond(jnp.max(removed) == float(k), lambda: (sv, rank), lambda: extract(True)[:2])


def _peer_topk_kernel(q_ref, keys_ref, l1_ref, c_ref, r2_ref, e2_ref):
    kk = PEER_TOPK
    for h in range(PEER_HEADS):
        q1 = q_ref[:, (2 * h) * PEER_NKEYS:(2 * h + 1) * PEER_NKEYS]
        q2 = q_ref[:, (2 * h + 1) * PEER_NKEYS:(2 * h + 2) * PEER_NKEYS]
        s1 = _nt_dot(keys_ref[h, 0], q1)
        s2 = _nt_dot(keys_ref[h, 1], q2)
        t = s1.shape[1]
        sv12, rank12 = _topk_desc(jnp.concatenate([s1, s2], axis=1), kk)
        sv1, sv2, rank1, rank2 = sv12[:, :t], sv12[:, t:], rank12[:, :t], rank12[:, t:]
        segs, seg_rows = [], []
        for a in range(kk):
            nb = kk // (a + 1)
            if nb > 1:
                rows = -(-nb // SUBLANES) * SUBLANES
                seg = sv1[a:a + 1] + sv2[0:rows]
                if rows != nb:
                    seg = jnp.where(lax.broadcasted_iota(jnp.int32, seg.shape, 0) < nb, seg, -jnp.inf)
                segs.append(seg)
                seg_rows.append(rows)
        single0 = len(segs)
        n_single = kk - single0
        assert n_single % SUBLANES == 0
        segs.append(sv1[single0:kk] + sv2[0:1])
        cand = jnp.concatenate(segs, axis=0)
        best0 = sv1[0:1] + sv2[0:1]

        def pick(first_only, cand=cand, best0=best0):
            iota = lax.broadcasted_iota(jnp.int32, cand.shape, 0)
            picked = jnp.zeros(cand.shape, F32)
            z = jnp.zeros((1, t), F32)
            cur = cand
            for _ in range(kk):
                m = jnp.max(cur, axis=0, keepdims=True)
                sel = cur == m
                if first_only:
                    sel = iota == jnp.min(jnp.where(sel, iota, cand.shape[0]), axis=0, keepdims=True)
                cur = jnp.where(sel, -jnp.inf, cur)
                picked = jnp.where(sel, 1.0, picked)
                z = z + jnp.exp(m - best0)
            return picked, z

        picked, z = pick(False)
        n_picked = jnp.sum(picked, axis=0, keepdims=True)
        picked, z = lax.cond(jnp.max(n_picked) == float(kk), lambda p=picked, zz=z: (p, zz),
                             lambda: pick(True))
        count, r0 = [], 0
        for rows in seg_rows:
            count.append(jnp.sum(picked[r0:r0 + rows], axis=0, keepdims=True))
            r0 += rows
        count += [picked[r0 + i:r0 + i + 1] for i in range(n_single)]
        l1 = jnp.zeros_like(rank1)
        for a in range(kk):
            l1 = jnp.where(rank1 == float(a), count[a], l1)
        sel1 = rank1 < float(kk)
        l1_ref[h] = l1
        c_ref[h] = jnp.where(sel1, jnp.exp(jnp.minimum(s1 - sv1[0:1], 0.0)) / z, 0.0)
        r2_ref[h] = rank2.astype(r2_ref.dtype)
        e2 = jnp.where(rank2 < float(kk), jnp.exp(jnp.minimum(s2 - sv2[0:1], 0.0)), 0.0)
        e2_ref[h] = e2.astype(e2_ref.dtype)


def peer_topk(q, keys, *, tt=256):
    n = q.shape[0]
    tt = min(tt, n)
    tab_spec = pl.BlockSpec((PEER_HEADS, PEER_NKEYS, tt), lambda i: (0, 0, i))
    return pl.pallas_call(
        _peer_topk_kernel,
        grid=(n // tt,),
        in_specs=[pl.BlockSpec((tt, q.shape[1]), lambda i: (i, 0)),
                  pl.BlockSpec(keys.shape, lambda i: (0, 0, 0, 0))],
        out_specs=[tab_spec] * 4,
        out_shape=[jax.ShapeDtypeStruct((PEER_HEADS, PEER_NKEYS, n), dt) for dt in (F32, F32, BF16, BF16)],
        compiler_params=_cparams(("parallel",)),
        name="peer_topk",
    )(q, keys)


def _gelu_exact(x):
    return 0.5 * x * (1.0 + lax.erf(x * (2.0 ** -0.5)))


def _peer_dense_kernel(hn_ref, down_ref, up_ref, l1_ref, c_ref, r2_ref, e2_ref, o_ref, wg_ref, *, groups):
    s = pl.program_id(1)
    row0 = (s * groups) % SUBLANES

    @pl.when(s == 0)
    def _():
        o_ref[...] = jnp.zeros_like(o_ref)

    tt = o_ref.shape[1]
    lane_chunk = 256
    halves = 2
    gh = groups // halves
    wgs = []
    for hf in range(halves):
        er = slice(hf * gh * PEER_NKEYS, (hf + 1) * gh * PEER_NKEYS)
        act = _nt_dot(down_ref[er, :], hn_ref[...])
        for gl in range(gh):
            gi = hf * gh + gl
            rows = slice(gi * PEER_NKEYS, (gi + 1) * PEER_NKEYS)
            lrows = slice(gl * PEER_NKEYS, (gl + 1) * PEER_NKEYS)
            for t0 in range(0, tt, lane_chunk):
                lanes = slice(t0, t0 + lane_chunk)
                g = None
                for h in range(PEER_HEADS):
                    l1 = l1_ref[h, pl.ds(row0 + gi, 1), lanes].astype(BF16)
                    cc = c_ref[h, pl.ds(row0 + gi, 1), lanes].astype(BF16)
                    e2 = e2_ref[h, :, lanes]
                    term = jnp.where(r2_ref[h, :, lanes] < l1, e2 * cc, jnp.zeros_like(e2))
                    g = term if g is None else g + term
                wg_ref[rows, lanes] = g * _gelu_exact(act[lrows, lanes]).astype(BF16)
        wgs.append(wg_ref[er, :])
    acc = o_ref[...]
    for hf in range(halves):
        er = slice(hf * gh * PEER_NKEYS, (hf + 1) * gh * PEER_NKEYS)
        acc = acc + _tn_dot(up_ref[er, :], wgs[hf])
    o_ref[...] = acc


def peer_dense(hn, down, up, tables, *, tt=512, te=512):
    n, d = hn.shape
    n_exp = down.shape[0]
    tt = min(tt, n)
    groups = te // PEER_NKEYS
    nblk = n_exp // te
    l1, c, r2, e2 = tables
    assert SUBLANES % groups == 0 and groups % 2 == 0
    row_spec = pl.BlockSpec((PEER_HEADS, SUBLANES, tt), lambda i, s: (0, s * groups // SUBLANES, i))
    tab_spec = pl.BlockSpec((PEER_HEADS, PEER_NKEYS, tt), lambda i, s: (0, 0, i))
    return pl.pallas_call(
        functools.partial(_peer_dense_kernel, groups=groups),
        grid=(n // tt, nblk),
        in_specs=[pl.BlockSpec((tt, d), lambda i, s: (i, 0)),
                  pl.BlockSpec((te, d), lambda i, s: (s, 0)),
                  pl.BlockSpec((te, d), lambda i, s: (s, 0)),
                  row_spec, row_spec, tab_spec, tab_spec],
        out_specs=pl.BlockSpec((d, tt), lambda i, s: (0, i)),
        out_shape=jax.ShapeDtypeStruct((d, n), F32),
        scratch_shapes=[pltpu.VMEM((te, tt), BF16)],
        compiler_params=_cparams(("parallel", "arbitrary")),
        name="peer_dense",
    )(hn, down, up, l1, c, r2, e2)


def _final_norm_kernel(h_ref, pt_ref, g_ref, o_ref):
    x = h_ref[...] + pt_ref[...].T
    ms = jnp.mean(x * x, axis=-1, keepdims=True)
    o_ref[...] = x * lax.rsqrt(ms + EPS) * g_ref[...]


def final_norm(h, pt, g, *, tm=256):
    m, d = h.shape
    tm = min(tm, m)
    row = pl.BlockSpec((tm, d), lambda i: (i, 0))
    return pl.pallas_call(
        _final_norm_kernel,
        grid=(m // tm,),
        in_specs=[row, pl.BlockSpec((d, tm), lambda i: (0, i)), pl.BlockSpec((1, d), lambda i: (0, 0))],
        out_specs=row,
        out_shape=jax.ShapeDtypeStruct((m, d), F32),
        compiler_params=_cparams(("parallel",)),
        name="final_norm",
    )(h, pt, g.reshape(1, d).astype(F32))


def kernel(x, mem, mix_norm_g, w_in, da_lambda_q1, da_lambda_k1, da_lambda_q2, da_lambda_k2, da_subln_g, hgrn_gamma, hgrn_norm_g, w_out, cross_norm_g, mem_norm_g, w_cq, w_ckv, w_co, ffn_norm_g, peer_wq, peer_subkeys, peer_down, peer_up, final_norm_g):
    batch, seq, d = x.shape
    mem_len = mem.shape[1]
    depth = w_in.shape[0]
    assert depth == 1 and hgrn_gamma.shape[0] == depth + 1
    l = 0
    da_w = DA_HEADS * HEAD_W
    x2 = x.reshape(batch * seq, d)

    proj = norm_matmul(x2, mix_norm_g[l], w_in[l].astype(BF16))
    lam4 = jnp.stack([da_lambda_q1[l], da_lambda_k1[l], da_lambda_q2[l], da_lambda_k2[l]]).astype(F32)
    a_out = diff_attention(proj, lam4, da_subln_g[l], batch, seq, layer_idx=l)
    b_out = hgrn2(proj, hgrn_gamma, hgrn_norm_g[l], batch, seq, layer_idx=l)
    wo = w_out[l].astype(BF16)
    h1 = matmul_residual([a_out, b_out], [wo[:da_w], wo[da_w:]], x2)

    qc = norm_matmul(h1, cross_norm_g[l], w_cq[l].astype(BF16))
    kv = norm_matmul(mem.reshape(batch * mem_len, d), mem_norm_g[l], w_ckv[l].astype(BF16))
    oc = cross_attention(qc, kv, batch, seq, mem_len)
    h2 = matmul_residual([oc], [w_co[l].astype(BF16)], h1)

    qp, hn = norm_matmul(h2, ffn_norm_g[l], peer_wq[l].astype(BF16), tm=512, emit_xn=True)
    tables = peer_topk(qp, peer_subkeys[l].astype(BF16))
    peer_t = peer_dense(hn, peer_down[l].astype(BF16), peer_up[l].astype(BF16), tables)
    out = final_norm(h2, peer_t, final_norm_g)
    return out.reshape(batch, seq, d)
```

```python
import functools
import math

import numpy as np
import jax
import jax.numpy as jnp
from jax import lax
from jax.experimental import pallas as pl
from jax.experimental.pallas import tpu as pltpu

F32 = jnp.float32
BF16 = jnp.bfloat16
EPS = 1e-6
NEG_BIG = -1e30

DA_HEADS = 16
DA_HALF = 64
HEAD_W = 128
HG_HEADS = 16
HG_BLOCK = 256
HG_SUB = 8
XA_HEADS = 4
PEER_HEADS = 8
PEER_NKEYS = 128
PEER_TOPK = 16
NOT_SELECTED = 99.0
SUBLANES = 8

VMEM_LIMIT = 58 * 1024 * 1024


def _cparams(sem):
    return pltpu.CompilerParams(dimension_semantics=sem, vmem_limit_bytes=VMEM_LIMIT)


def _nt_dot(a, b):
    return lax.dot_general(a, b, (((1,), (1,)), ((), ())), preferred_element_type=F32)


def _tn_dot(a, b):
    return lax.dot_general(a, b, (((0,), (0,)), ((), ())), preferred_element_type=F32)


def _sigmoid(x):
    return 0.5 * jnp.tanh(0.5 * x) + 0.5


def _norm_matmul_kernel(x_ref, g_ref, w_ref, o_ref, xn_ref, *, rows):
    @pl.when(pl.program_id(1) == 0)
    def _():
        g = g_ref[...]

        def chunk(ci, _):
            r0 = pl.multiple_of(ci * rows, rows)
            x = x_ref[pl.ds(r0, rows), :]
            ms = jnp.mean(x * x, axis=-1, keepdims=True)
            xn_ref[pl.ds(r0, rows), :] = (x * lax.rsqrt(ms + EPS) * g).astype(BF16)
            return 0

        lax.fori_loop(0, x_ref.shape[0] // rows, chunk, 0)

    o_ref[...] = jnp.dot(xn_ref[...], w_ref[...], preferred_element_type=F32).astype(o_ref.dtype)


def norm_matmul(x, g, w, *, tm=512, tn=1024, emit_xn=False):
    m, k = x.shape
    n = w.shape[1]
    tm, tn = min(tm, m), min(tn, n)
    assert m % tm == 0 and n % tn == 0
    rows = min(128, tm)
    out_shape = [jax.ShapeDtypeStruct((m, n), BF16)]
    out_specs = [pl.BlockSpec((tm, tn), lambda i, j: (i, j))]
    scratch = [pltpu.VMEM((tm, k), BF16)]
    if emit_xn:
        out_shape.append(jax.ShapeDtypeStruct((m, k), BF16))
        out_specs.append(pl.BlockSpec((tm, k), lambda i, j: (i, 0)))
        scratch = []
    res = pl.pallas_call(
        functools.partial(_norm_matmul_kernel, rows=rows),
        grid=(m // tm, n // tn),
        in_specs=[pl.BlockSpec((tm, k), lambda i, j: (i, 0)),
                  pl.BlockSpec((1, k), lambda i, j: (0, 0)),
                  pl.BlockSpec((k, tn), lambda i, j: (0, j))],
        out_specs=out_specs,
        out_shape=out_shape,
        scratch_shapes=scratch,
        compiler_params=_cparams(("parallel", "arbitrary")),
        name="norm_matmul",
    )(x, g.reshape(1, k).astype(F32), w)
    return res if emit_xn else res[0]


def _matmul_residual_kernel(*refs, n_lhs):
    lhs = refs[:n_lhs]
    ws = refs[n_lhs:2 * n_lhs]
    res_ref, o_ref = refs[2 * n_lhs], refs[2 * n_lhs + 1]
    acc = res_ref[...]
    for a_ref, w_ref in zip(lhs, ws):
        acc = acc + jnp.dot(a_ref[...], w_ref[...], preferred_element_type=F32)
    o_ref[...] = acc


def matmul_residual(lhs_list, w_list, res, *, tm=1024, tn=1024):
    m, n = res.shape
    tm, tn = min(tm, m), min(tn, n)
    assert m % tm == 0 and n % tn == 0
    n_lhs = len(lhs_list)
    in_specs = [pl.BlockSpec((tm, a.shape[1]), lambda i, j: (i, 0)) for a in lhs_list]
    in_specs += [pl.BlockSpec((w.shape[0], tn), lambda i, j: (0, j)) for w in w_list]
    in_specs += [pl.BlockSpec((tm, tn), lambda i, j: (i, j))]
    return pl.pallas_call(
        functools.partial(_matmul_residual_kernel, n_lhs=n_lhs),
        grid=(m // tm, n // tn),
        in_specs=in_specs,
        out_specs=pl.BlockSpec((tm, tn), lambda i, j: (i, j)),
        out_shape=jax.ShapeDtypeStruct((m, n), F32),
        compiler_params=_cparams(("parallel", "parallel")),
        name="matmul_residual",
    )(*lhs_list, *w_list, res)


def _diff_attn_kernel(lam_ref, subg_ref, slope_ref, q_ref, k_ref, vt_ref, o_ref, base_ref, acc_ref, *,
                      tq, tc, lam_init):
    qi = pl.program_id(2)
    ncol = 2 * tq
    per_q = tq // tc
    log2e = math.log2(math.e)
    q = (q_ref[...].astype(F32) * (DA_HALF ** -0.5 * log2e)).astype(BF16)
    lane = lax.broadcasted_iota(jnp.int32, q.shape, 1)
    zero = jnp.zeros_like(q)
    q2 = jnp.concatenate([jnp.where(lane < DA_HALF, q, zero), jnp.where(lane >= DA_HALF, q, zero)], axis=0)
    slope = slope_ref[0:1, 0:1] * log2e
    krow = lax.broadcasted_iota(jnp.int32, (tc, ncol), 0)

    @pl.when(qi == 0)
    def _():
        base_ref[...] = slope * krow.astype(F32)

    acc_ref[...] = jnp.zeros_like(acc_ref)

    def process(j, m, masked):
        rel = j * tc - qi * tq
        off = slope * rel.astype(F32)
        kj = k_ref[pl.ds(pl.multiple_of(j * tc, tc), tc), :]
        s = _nt_dot(kj, q2) + base_ref[...]
        if masked:
            qpos = lax.broadcasted_iota(jnp.int32, (tc, ncol), 1) % tq
            s = jnp.where(krow + rel <= qpos, s, NEG_BIG)
        m_new = jnp.maximum(m, jnp.max(s, axis=0, keepdims=True) + off)
        p = jnp.exp2((s - (m_new - off)).astype(BF16))
        alpha = jnp.exp2(m - m_new)
        acc_ref[...] = acc_ref[...] * alpha + jnp.dot(vt_ref[j], p, preferred_element_type=F32)
        return m_new

    m = jnp.full((1, ncol), NEG_BIG, F32)
    n_full = per_q * qi
    m = lax.fori_loop(0, n_full // 2, lambda it, mm: process(2 * it + 1, process(2 * it, mm, False), False), m)
    m = lax.cond(n_full % 2 == 1, lambda mm: process(n_full - 1, mm, False), lambda mm: mm, m)
    for u in range(per_q):
        m = process(per_q * qi + u, m, True)

    lv = lam_ref[...]
    lam = (jnp.exp(jnp.sum(lv[0:1] * lv[1:2], axis=-1, keepdims=True))
           - jnp.exp(jnp.sum(lv[2:3] * lv[3:4], axis=-1, keepdims=True)) + lam_init)
    on = acc_ref[0:HEAD_W, :] / acc_ref[HEAD_W:HEAD_W + 1, :]
    o = on[:, :tq] - lam * on[:, tq:]
    ms = jnp.mean(o * o, axis=0, keepdims=True)
    o = o * lax.rsqrt(ms + EPS) * subg_ref[...] * (1.0 - lam_init)
    o_ref[...] = o.T.astype(o_ref.dtype)


def diff_attention(proj, lam4, subln_g, batch, seq, *, tq=512, tc=512, layer_idx=0):
    n = proj.shape[0]
    tq = min(tq, seq)
    tc = min(tc, tq)
    nq, nc = seq // tq, seq // tc
    lam_init = 0.8 - 0.6 * math.exp(-0.3 * layer_idx)
    slopes = np.array([2.0 ** (-8.0 * (h + 1) / DA_HEADS) for h in range(DA_HEADS)], dtype=np.float32)
    slope_tab = jnp.asarray(np.broadcast_to(slopes[:, None, None], (DA_HEADS, SUBLANES, HEAD_W)).copy())
    da_w = DA_HEADS * HEAD_W
    vt5 = proj[:, 2 * da_w:3 * da_w].reshape(batch, nc, tc, DA_HEADS, HEAD_W).transpose(0, 3, 1, 4, 2)
    ones_tile = jnp.zeros((2 * SUBLANES, tc), BF16).at[0].set(1)
    vt5 = jnp.concatenate([vt5, jnp.broadcast_to(ones_tile, vt5.shape[:3] + ones_tile.shape)], axis=3)
    vrows = HEAD_W + 2 * SUBLANES
    return pl.pallas_call(
        functools.partial(_diff_attn_kernel, tq=tq, tc=tc, lam_init=lam_init),
        grid=(batch, DA_HEADS, nq),
        in_specs=[pl.BlockSpec((4, DA_HALF), lambda b, h, i: (0, 0)),
                  pl.BlockSpec((HEAD_W, 1), lambda b, h, i: (0, 0)),
                  pl.BlockSpec((None, SUBLANES, HEAD_W), lambda b, h, i: (h, 0, 0)),
                  pl.BlockSpec((tq, HEAD_W), lambda b, h, i: (b * nq + i, h)),
                  pl.BlockSpec((seq, HEAD_W), lambda b, h, i: (b, DA_HEADS + h)),
                  pl.BlockSpec((None, None, nc, vrows, tc), lambda b, h, i: (b, h, 0, 0, 0))],
        out_specs=pl.BlockSpec((tq, HEAD_W), lambda b, h, i: (b * nq + i, h)),
        out_shape=jax.ShapeDtypeStruct((n, da_w), BF16),
        scratch_shapes=[pltpu.VMEM((tc, 2 * tq), F32), pltpu.VMEM((vrows, 2 * tq), F32)],
        compiler_params=_cparams(("parallel", "parallel", "arbitrary")),
        name="diff_attention",
    )(lam4, subln_g.reshape(HEAD_W, 1).astype(F32), slope_tab, proj, proj, vt5)


def _hgrn_chunk(q, k, b, v):
    c = q.shape[0]
    row = lax.broadcasted_iota(jnp.int32, (c, 1), 0)
    arow = lax.broadcasted_iota(jnp.int32, (c, c), 0)
    acol = lax.broadcasted_iota(jnp.int32, (c, c), 1)

    def block_level(half):
        grp = 2 * half
        bg = b.reshape(c // grp, grp, HEAD_W)
        ref = jnp.broadcast_to(bg[:, half - 1:half, :], bg.shape).reshape(c, HEAD_W)
        upper = (row % grp) >= half
        qs = jnp.where(upper, q * jnp.exp2(jnp.minimum(b - ref, 0.0)), 0.0)
        ks = jnp.where(upper, 0.0, k * jnp.exp2(jnp.minimum(ref - b, 0.0)))
        a = _nt_dot(qs.astype(BF16), ks.astype(BF16))
        keep = ((arow // grp) == (acol // grp)) & ((arow % grp) >= half) & ((acol % grp) < half)
        return jnp.where(keep, a, 0.0)

    a_tot = None
    half = c // 2
    while half >= HG_SUB:
        lvl = block_level(half)
        a_tot = lvl if a_tot is None else a_tot + lvl
        half //= 2
    o = jnp.dot(a_tot.astype(BF16), v, preferred_element_type=F32)

    nsub = c // HG_SUB
    q3 = q.reshape(nsub, HG_SUB, HEAD_W)
    k3 = k.reshape(nsub, HG_SUB, HEAD_W)
    b3 = b.reshape(nsub, HG_SUB, HEAD_W)
    v3 = v.astype(F32).reshape(nsub, HG_SUB, HEAD_W)
    trow = lax.broadcasted_iota(jnp.int32, (nsub, HG_SUB, 1), 1)
    o3 = jnp.zeros((nsub, HG_SUB, HEAD_W), F32)
    for s in range(HG_SUB):
        e = jnp.exp2(jnp.minimum(b3 - b3[:, s:s + 1, :], 0.0))
        col = jnp.sum(q3 * (k3[:, s:s + 1, :] * e), axis=-1, keepdims=True)
        o3 = o3 + jnp.where(trow >= s, col, 0.0) * v3[:, s:s + 1, :]
    return o + o3.reshape(c, HEAD_W)


def _hgrn_kernel(gam_ref, ng_ref, q_ref, f_ref, i_ref, g_ref, o_ref, st_ref, *, n_chunks, layer_idx):
    @pl.when(pl.program_id(2) == 0)
    def _():
        st_ref[...] = jnp.zeros_like(st_ref)

    gam = gam_ref[...]
    ge = jnp.exp(gam - jnp.max(gam, axis=0, keepdims=True))
    lb = jnp.sum(ge[0:layer_idx + 1], axis=0, keepdims=True) / jnp.sum(ge, axis=0, keepdims=True)
    ng = ng_ref[...]
    c = q_ref.shape[0] // n_chunks
    tri = (lax.broadcasted_iota(jnp.int32, (c, c), 0) >= lax.broadcasted_iota(jnp.int32, (c, c), 1))
    tri = jnp.where(tri, 1.0, 0.0).astype(BF16)

    def chunk(ci, _):
        r0 = pl.multiple_of(ci * c, c)
        qr = q_ref[pl.ds(r0, c), :].astype(F32)
        fr = f_ref[pl.ds(r0, c), :].astype(F32)
        v = i_ref[pl.ds(r0, c), :]
        gr = g_ref[pl.ds(r0, c), :].astype(F32)
        q = qr * _sigmoid(qr)
        f = lb + (1.0 - lb) * _sigmoid(fr)
        logf = jnp.log2(f)
        k = 1.0 - f
        hi = logf.astype(BF16)
        r1 = logf - hi.astype(F32)
        mid = r1.astype(BF16)
        lo = (r1 - mid.astype(F32)).astype(BF16)
        b = (jnp.dot(tri, hi, preferred_element_type=F32) + jnp.dot(tri, mid, preferred_element_type=F32)
             + jnp.dot(tri, lo, preferred_element_type=F32))
        st = st_ref[...]
        o = _nt_dot((q * jnp.exp2(b)).astype(BF16), st.astype(BF16))
        o = o + _hgrn_chunk(q, k, b, v)
        bend = b[c - 1:c]
        kd = (k * jnp.exp2(bend - b)).astype(BF16)
        st_ref[...] = st * jnp.exp2(bend) + _tn_dot(v, kd)
        ms = jnp.mean(o * o, axis=-1, keepdims=True)
        on = o * lax.rsqrt(ms + EPS) * ng * (gr * _sigmoid(gr))
        o_ref[pl.ds(r0, c), :] = on.astype(o_ref.dtype)
        return 0

    per_trip = max(g for g in (4, 2, 1) if n_chunks % g == 0)

    def trip(i, _):
        for dj in range(per_trip):
            chunk(per_trip * i + dj, 0)
        return 0

    lax.fori_loop(0, n_chunks // per_trip, trip, 0)


def hgrn2(proj, gamma, norm_g, batch, seq, *, tc=1024, layer_idx=0, col0=3 * DA_HEADS):
    n = proj.shape[0]
    tc = min(tc, seq)
    ng = seq // tc
    depth1 = gamma.shape[0]

    def spec(off):
        return pl.BlockSpec((tc, HEAD_W), lambda b, h, g: (b * ng + g, col0 + off * HG_HEADS + h))

    return pl.pallas_call(
        functools.partial(_hgrn_kernel, n_chunks=tc // min(HG_BLOCK, tc), layer_idx=layer_idx),
        grid=(batch, HG_HEADS, ng),
        in_specs=[pl.BlockSpec((depth1, HEAD_W), lambda b, h, g: (0, h)),
                  pl.BlockSpec((1, HEAD_W), lambda b, h, g: (0, 0)),
                  spec(0), spec(1), spec(2), spec(3)],
        out_specs=pl.BlockSpec((tc, HEAD_W), lambda b, h, g: (b * ng + g, h)),
        out_shape=jax.ShapeDtypeStruct((n, HG_HEADS * HEAD_W), BF16),
        scratch_shapes=[pltpu.VMEM((HEAD_W, HEAD_W), F32)],
        compiler_params=_cparams(("parallel", "parallel", "arbitrary")),
        name="hgrn2",
    )(gamma.astype(F32), norm_g.reshape(1, HEAD_W).astype(F32), proj, proj, proj, proj)


def _cross_attn_kernel(q_ref, kv_ref, o_ref, *, scale):
    d = q_ref.shape[1]
    hd = d // XA_HEADS
    for h in range(XA_HEADS):
        cols = slice(h * hd, (h + 1) * hd)
        q = q_ref[:, cols] * jnp.asarray(scale, BF16)
        s = _nt_dot(q, kv_ref[:, cols])
        m = jnp.max(s, axis=-1, keepdims=True)
        p = jnp.exp(s - m)
        l = jnp.sum(p, axis=-1, keepdims=True)
        o = jnp.dot(p.astype(BF16), kv_ref[:, d + h * hd:d + (h + 1) * hd], preferred_element_type=F32) / l
        o_ref[:, cols] = o.astype(o_ref.dtype)


def cross_attention(q, kv, batch, seq, mem_len, *, tq=512):
    n, d = q.shape
    tq = min(tq, seq)
    nq = seq // tq
    return pl.pallas_call(
        functools.partial(_cross_attn_kernel, scale=(d // XA_HEADS) ** -0.5),
        grid=(batch, nq),
        in_specs=[pl.BlockSpec((tq, d), lambda b, i: (b * nq + i, 0)),
                  pl.BlockSpec((mem_len, 2 * d), lambda b, i: (b, 0))],
        out_specs=pl.BlockSpec((tq, d), lambda b, i: (b * nq + i, 0)),
        out_shape=jax.ShapeDtypeStruct((n, d), BF16),
        compiler_params=_cparams(("parallel", "parallel")),
        name="cross_attention",
    )(q, kv)


def _topk_desc(s, k):
    rows = s.shape[0]

    def extract(first_only):
        iota = lax.broadcasted_iota(jnp.int32, s.shape, 0)
        rank = jnp.full(s.shape, NOT_SELECTED, F32)
        vals = []
        cur = s
        for r in range(k):
            m = jnp.max(cur, axis=0, keepdims=True)
            sel = cur == m
            if first_only:
                sel = iota == jnp.min(jnp.where(sel, iota, rows), axis=0, keepdims=True)
            rank = jnp.where(sel, float(r), rank)
            cur = jnp.where(sel, -jnp.inf, cur)
            vals.append(m)
        return jnp.concatenate(vals, axis=0), rank, cur

    sv, rank, cur = extract(False)
    removed = jnp.sum(jnp.where(cur == -jnp.inf, 1.0, 0.0), axis=0, keepdims=True)
    return lax.cond(jnp.max(removed) == float(k), lambda: (sv, rank), lambda: extract(True)[:2])


def _peer_topk_kernel(q_ref, keys_ref, l1_ref, c_ref, r2_ref, e2_ref):
    kk = PEER_TOPK
    for h in range(PEER_HEADS):
        q1 = q_ref[:, (2 * h) * PEER_NKEYS:(2 * h + 1) * PEER_NKEYS]
        q2 = q_ref[:, (2 * h + 1) * PEER_NKEYS:(2 * h + 2) * PEER_NKEYS]
        s1 = _nt_dot(keys_ref[h, 0], q1)
        s2 = _nt_dot(keys_ref[h, 1], q2)
        t = s1.shape[1]
        sv12, rank12 = _topk_desc(jnp.concatenate([s1, s2], axis=1), kk)
        sv1, sv2, rank1, rank2 = sv12[:, :t], sv12[:, t:], rank12[:, :t], rank12[:, t:]
        segs, seg_rows = [], []
        for a in range(kk):
            nb = kk // (a + 1)
            if nb > 1:
                rows = -(-nb // SUBLANES) * SUBLANES
                seg = sv1[a:a + 1] + sv2[0:rows]
                if rows != nb:
                    seg = jnp.where(lax.broadcasted_iota(jnp.int32, seg.shape, 0) < nb, seg, -jnp.inf)
                segs.append(seg)
                seg_rows.append(rows)
        single0 = len(segs)
        n_single = kk - single0
        assert n_single % SUBLANES == 0
        segs.append(sv1[single0:kk] + sv2[0:1])
        cand = jnp.concatenate(segs, axis=0)
        best0 = sv1[0:1] + sv2[0:1]

        def pick(first_only, cand=cand, best0=best0):
            iota = lax.broadcasted_iota(jnp.int32, cand.shape, 0)
            picked = jnp.zeros(cand.shape, F32)
            z = jnp.zeros((1, t), F32)
            cur = cand
            for _ in range(kk):
                m = jnp.max(cur, axis=0, keepdims=True)
                sel = cur == m
                if first_only:
                    sel = iota == jnp.min(jnp.where(sel, iota, cand.shape[0]), axis=0, keepdims=True)
                cur = jnp.where(sel, -jnp.inf, cur)
                picked = jnp.where(sel, 1.0, picked)
                z = z + jnp.exp(m - best0)
            return picked, z

        picked, z = pick(False)
        n_picked = jnp.sum(picked, axis=0, keepdims=True)
        picked, z = lax.cond(jnp.max(n_picked) == float(kk), lambda p=picked, zz=z: (p, zz),
                             lambda: pick(True))
        count, r0 = [], 0
        for rows in seg_rows:
            count.append(jnp.sum(picked[r0:r0 + rows], axis=0, keepdims=True))
            r0 += rows
        count += [picked[r0 + i:r0 + i + 1] for i in range(n_single)]
        l1 = jnp.zeros_like(rank1)
        for a in range(kk):
            l1 = jnp.where(rank1 == float(a), count[a], l1)
        sel1 = rank1 < float(kk)
        l1_ref[h] = l1
        c_ref[h] = jnp.where(sel1, jnp.exp(jnp.minimum(s1 - sv1[0:1], 0.0)) / z, 0.0)
        r2_ref[h] = rank2.astype(r2_ref.dtype)
        e2 = jnp.where(rank2 < float(kk), jnp.exp(jnp.minimum(s2 - sv2[0:1], 0.0)), 0.0)
        e2_ref[h] = e2.astype(e2_ref.dtype)


def peer_topk(q, keys, *, tt=256):
    n = q.shape[0]
    tt = min(tt, n)
    tab_spec = pl.BlockSpec((PEER_HEADS, PEER_NKEYS, tt), lambda i: (0, 0, i))
    return pl.pallas_call(
        _peer_topk_kernel,
        grid=(n // tt,),
        in_specs=[pl.BlockSpec((tt, q.shape[1]), lambda i: (i, 0)),
                  pl.BlockSpec(keys.shape, lambda i: (0, 0, 0, 0))],
        out_specs=[tab_spec] * 4,
        out_shape=[jax.ShapeDtypeStruct((PEER_HEADS, PEER_NKEYS, n), dt) for dt in (F32, F32, BF16, BF16)],
        compiler_params=_cparams(("parallel",)),
        name="peer_topk",
    )(q, keys)


def _gelu_exact(x):
    return 0.5 * x * (1.0 + lax.erf(x * (2.0 ** -0.5)))


def _peer_dense_kernel(hn_ref, down_ref, up_ref, l1_ref, c_ref, r2_ref, e2_ref, o_ref, wg_ref, *, groups):
    s = pl.program_id(1)
    row0 = (s * groups) % SUBLANES

    @pl.when(s == 0)
    def _():
        o_ref[...] = jnp.zeros_like(o_ref)

    tt = o_ref.shape[1]
    lane_chunk = 256
    halves = 2
    gh = groups // halves
    wgs = []
    for hf in range(halves):
        er = slice(hf * gh * PEER_NKEYS, (hf + 1) * gh * PEER_NKEYS)
        act = _nt_dot(down_ref[er, :], hn_ref[...])
        for gl in range(gh):
            gi = hf * gh + gl
            rows = slice(gi * PEER_NKEYS, (gi + 1) * PEER_NKEYS)
            lrows = slice(gl * PEER_NKEYS, (gl + 1) * PEER_NKEYS)
            for t0 in range(0, tt, lane_chunk):
                lanes = slice(t0, t0 + lane_chunk)
                g = None
                for h in range(PEER_HEADS):
                    l1 = l1_ref[h, pl.ds(row0 + gi, 1), lanes].astype(BF16)
                    cc = c_ref[h, pl.ds(row0 + gi, 1), lanes].astype(BF16)
                    e2 = e2_ref[h, :, lanes]
                    term = jnp.where(r2_ref[h, :, lanes] < l1, e2 * cc, jnp.zeros_like(e2))
                    g = term if g is None else g + term
                wg_ref[rows, lanes] = g * _gelu_exact(act[lrows, lanes]).astype(BF16)
        wgs.append(wg_ref[er, :])
    acc = o_ref[...]
    for hf in range(halves):
        er = slice(hf * gh * PEER_NKEYS, (hf + 1) * gh * PEER_NKEYS)
        acc = acc + _tn_dot(up_ref[er, :], wgs[hf])
    o_ref[...] = acc


def peer_dense(hn, down, up, tables, *, tt=512, te=512):
    n, d = hn.shape
    n_exp = down.shape[0]
    tt = min(tt, n)
    groups = te // PEER_NKEYS
    nblk = n_exp // te
    l1, c, r2, e2 = tables
    assert SUBLANES % groups == 0 and groups % 2 == 0
    row_spec = pl.BlockSpec((PEER_HEADS, SUBLANES, tt), lambda i, s: (0, s * groups // SUBLANES, i))
    tab_spec = pl.BlockSpec((PEER_HEADS, PEER_NKEYS, tt), lambda i, s: (0, 0, i))
    return pl.pallas_call(
        functools.partial(_peer_dense_kernel, groups=groups),
        grid=(n // tt, nblk),
        in_specs=[pl.BlockSpec((tt, d), lambda i, s: (i, 0)),
                  pl.BlockSpec((te, d), lambda i, s: (s, 0)),
                  pl.BlockSpec((te, d), lambda i, s: (s, 0)),
                  row_spec, row_spec, tab_spec, tab_spec],
        out_specs=pl.BlockSpec((d, tt), lambda i, s: (0, i)),
        out_shape=jax.ShapeDtypeStruct((d, n), F32),
        scratch_shapes=[pltpu.VMEM((te, tt), BF16)],
        compiler_params=_cparams(("parallel", "arbitrary")),
        name="peer_dense",
    )(hn, down, up, l1, c, r2, e2)


def _final_norm_kernel(h_ref, pt_ref, g_ref, o_ref):
    x = h_ref[...] + pt_ref[...].T
    ms = jnp.mean(x * x, axis=-1, keepdims=True)
    o_ref[...] = x * lax.rsqrt(ms + EPS) * g_ref[...]


def final_norm(h, pt, g, *, tm=256):
    m, d = h.shape
    tm = min(tm, m)
    row = pl.BlockSpec((tm, d), lambda i: (i, 0))
    return pl.pallas_call(
        _final_norm_kernel,
        grid=(m // tm,),
        in_specs=[row, pl.BlockSpec((d, tm), lambda i: (0, i)), pl.BlockSpec((1, d), lambda i: (0, 0))],
        out_specs=row,
        out_shape=jax.ShapeDtypeStruct((m, d), F32),
        compiler_params=_cparams(("parallel",)),
        name="final_norm",
    )(h, pt, g.reshape(1, d).astype(F32))


def kernel(x, mem, mix_norm_g, w_in, da_lambda_q1, da_lambda_k1, da_lambda_q2, da_lambda_k2, da_subln_g, hgrn_gamma, hgrn_norm_g, w_out, cross_norm_g, mem_norm_g, w_cq, w_ckv, w_co, ffn_norm_g, peer_wq, peer_subkeys, peer_down, peer_up, final_norm_g):
    batch, seq, d = x.shape
    mem_len = mem.shape[1]
    depth = w_in.shape[0]
    assert depth == 1 and hgrn_gamma.shape[0] == depth + 1
    l = 0
    da_w = DA_HEADS * HEAD_W
    x2 = x.reshape(batch * seq, d)

    proj = norm_matmul(x2, mix_norm_g[l], w_in[l].astype(BF16), tn=w_in.shape[-1] // 8)
    lam4 = jnp.stack([da_lambda_q1[l], da_lambda_k1[l], da_lambda_q2[l], da_lambda_k2[l]]).astype(F32)
    a_out = diff_attention(proj, lam4, da_subln_g[l], batch, seq, layer_idx=l)
    b_out = hgrn2(proj, hgrn_gamma, hgrn_norm_g[l], batch, seq, layer_idx=l)
    wo = w_out[l].astype(BF16)
    h1 = matmul_residual([a_out, b_out], [wo[:da_w], wo[da_w:]], x2)

    qc = norm_matmul(h1, cross_norm_g[l], w_cq[l].astype(BF16))
    kv = norm_matmul(mem.reshape(batch * mem_len, d), mem_norm_g[l], w_ckv[l].astype(BF16))
    oc = cross_attention(qc, kv, batch, seq, mem_len)
    h2 = matmul_residual([oc], [w_co[l].astype(BF16)], h1)

    qp, hn = norm_matmul(h2, ffn_norm_g[l], peer_wq[l].astype(BF16), emit_xn=True)
    tables = peer_topk(qp, peer_subkeys[l].astype(BF16))
    peer_t = peer_dense(hn, peer_down[l].astype(BF16), peer_up[l].astype(BF16), tables)
    out = final_norm(h2, peer_t, final_norm_g)
    return out.reshape(batch, seq, d)
```

```python
import functools
import math

import numpy as np
import jax
import jax.numpy as jnp
from jax import lax
from jax.experimental import pallas as pl
from jax.experimental.pallas import tpu as pltpu

F32 = jnp.float32
BF16 = jnp.bfloat16
EPS = 1e-6
NEG_BIG = -1e30

DA_HEADS = 16
DA_HALF = 64
HEAD_W = 128
HG_HEADS = 16
HG_BLOCK = 256
HG_SUB = 8
XA_HEADS = 4
PEER_HEADS = 8
PEER_NKEYS = 128
PEER_TOPK = 16
NOT_SELECTED = 99.0
SUBLANES = 8

VMEM_LIMIT = 58 * 1024 * 1024


def _cparams(sem):
    return pltpu.CompilerParams(dimension_semantics=sem, vmem_limit_bytes=VMEM_LIMIT)


def _nt_dot(a, b):
    return lax.dot_general(a, b, (((1,), (1,)), ((), ())), preferred_element_type=F32)


def _tn_dot(a, b):
    return lax.dot_general(a, b, (((0,), (0,)), ((), ())), preferred_element_type=F32)


def _sigmoid(x):
    return 0.5 * jnp.tanh(0.5 * x) + 0.5


def _norm_matmul_kernel(x_ref, g_ref, w_ref, o_ref, xn_ref, *, rows):
    @pl.when(pl.program_id(1) == 0)
    def _():
        g = g_ref[...]

        def chunk(ci, _):
            r0 = pl.multiple_of(ci * rows, rows)
            x = x_ref[pl.ds(r0, rows), :]
            ms = jnp.mean(x * x, axis=-1, keepdims=True)
            xn_ref[pl.ds(r0, rows), :] = (x * lax.rsqrt(ms + EPS) * g).astype(BF16)
            return 0

        lax.fori_loop(0, x_ref.shape[0] // rows, chunk, 0)

    o_ref[...] = jnp.dot(xn_ref[...], w_ref[...], preferred_element_type=F32).astype(o_ref.dtype)


def norm_matmul(x, g, w, *, tm=512, tn=1024, emit_xn=False):
    m, k = x.shape
    n = w.shape[1]
    tm, tn = min(tm, m), min(tn, n)
    assert m % tm == 0 and n % tn == 0
    rows = min(128, tm)
    out_shape = [jax.ShapeDtypeStruct((m, n), BF16)]
    out_specs = [pl.BlockSpec((tm, tn), lambda i, j: (i, j))]
    scratch = [pltpu.VMEM((tm, k), BF16)]
    if emit_xn:
        out_shape.append(jax.ShapeDtypeStruct((m, k), BF16))
        out_specs.append(pl.BlockSpec((tm, k), lambda i, j: (i, 0)))
        scratch = []
    res = pl.pallas_call(
        functools.partial(_norm_matmul_kernel, rows=rows),
        grid=(m // tm, n // tn),
        in_specs=[pl.BlockSpec((tm, k), lambda i, j: (i, 0)),
                  pl.BlockSpec((1, k), lambda i, j: (0, 0)),
                  pl.BlockSpec((k, tn), lambda i, j: (0, j))],
        out_specs=out_specs,
        out_shape=out_shape,
        scratch_shapes=scratch,
        compiler_params=_cparams(("parallel", "arbitrary")),
        name="norm_matmul",
    )(x, g.reshape(1, k).astype(F32), w)
    return res if emit_xn else res[0]


def _matmul_residual_kernel(*refs, n_lhs):
    lhs = refs[:n_lhs]
    ws = refs[n_lhs:2 * n_lhs]
    res_ref, o_ref = refs[2 * n_lhs], refs[2 * n_lhs + 1]
    acc = res_ref[...]
    for a_ref, w_ref in zip(lhs, ws):
        acc = acc + jnp.dot(a_ref[...], w_ref[...], preferred_element_type=F32)
    o_ref[...] = acc


def matmul_residual(lhs_list, w_list, res, *, tm=1024, tn=1024):
    m, n = res.shape
    tm, tn = min(tm, m), min(tn, n)
    assert m % tm == 0 and n % tn == 0
    n_lhs = len(lhs_list)
    in_specs = [pl.BlockSpec((tm, a.shape[1]), lambda i, j: (i, 0)) for a in lhs_list]
    in_specs += [pl.BlockSpec((w.shape[0], tn), lambda i, j: (0, j)) for w in w_list]
    in_specs += [pl.BlockSpec((tm, tn), lambda i, j: (i, j))]
    return pl.pallas_call(
        functools.partial(_matmul_residual_kernel, n_lhs=n_lhs),
        grid=(m // tm, n // tn),
        in_specs=in_specs,
        out_specs=pl.BlockSpec((tm, tn), lambda i, j: (i, j)),
        out_shape=jax.ShapeDtypeStruct((m, n), F32),
        compiler_params=_cparams(("parallel", "parallel")),
        name="matmul_residual",
    )(*lhs_list, *w_list, res)


def _diff_attn_kernel(lam_ref, subg_ref, slope_ref, q_ref, k_ref, vt_ref, o_ref, base_ref, acc_ref, *,
                      tq, tc, lam_init):
    qi = pl.program_id(2)
    ncol = 2 * tq
    per_q = tq // tc
    log2e = math.log2(math.e)
    q = (q_ref[...].astype(F32) * (DA_HALF ** -0.5 * log2e)).astype(BF16)
    lane = lax.broadcasted_iota(jnp.int32, q.shape, 1)
    zero = jnp.zeros_like(q)
    q2 = jnp.concatenate([jnp.where(lane < DA_HALF, q, zero), jnp.where(lane >= DA_HALF, q, zero)], axis=0)
    slope = slope_ref[0:1, 0:1] * log2e
    krow = lax.broadcasted_iota(jnp.int32, (tc, ncol), 0)

    @pl.when(qi == 0)
    def _():
        base_ref[...] = slope * krow.astype(F32)

    acc_ref[...] = jnp.zeros_like(acc_ref)

    def process(j, m, masked):
        rel = j * tc - qi * tq
        off = slope * rel.astype(F32)
        kj = k_ref[pl.ds(pl.multiple_of(j * tc, tc), tc), :]
        s = _nt_dot(kj, q2) + base_ref[...]
        if masked:
            qpos = lax.broadcasted_iota(jnp.int32, (tc, ncol), 1) % tq
            s = jnp.where(krow + rel <= qpos, s, NEG_BIG)
        m_new = jnp.maximum(m, jnp.max(s, axis=0, keepdims=True) + off)
        p = jnp.exp2((s - (m_new - off)).astype(BF16))
        alpha = jnp.exp2(m - m_new)
        acc_ref[...] = acc_ref[...] * alpha + jnp.dot(vt_ref[j], p, preferred_element_type=F32)
        return m_new

    m = jnp.full((1, ncol), NEG_BIG, F32)
    n_full = per_q * qi
    m = lax.fori_loop(0, n_full // 2, lambda it, mm: process(2 * it + 1, process(2 * it, mm, False), False), m)
    m = lax.cond(n_full % 2 == 1, lambda mm: process(n_full - 1, mm, False), lambda mm: mm, m)
    for u in range(per_q):
        m = process(per_q * qi + u, m, True)

    lv = lam_ref[...]
    lam = (jnp.exp(jnp.sum(lv[0:1] * lv[1:2], axis=-1, keepdims=True))
           - jnp.exp(jnp.sum(lv[2:3] * lv[3:4], axis=-1, keepdims=True)) + lam_init)
    on = acc_ref[0:HEAD_W, :] / acc_ref[HEAD_W:HEAD_W + 1, :]
    o = on[:, :tq] - lam * on[:, tq:]
    ms = jnp.mean(o * o, axis=0, keepdims=True)
    o = o * lax.rsqrt(ms + EPS) * subg_ref[...] * (1.0 - lam_init)
    o_ref[...] = o.T.astype(o_ref.dtype)


def diff_attention(proj, lam4, subln_g, batch, seq, *, tq=512, tc=512, layer_idx=0):
    n = proj.shape[0]
    tq = min(tq, seq)
    tc = min(tc, tq)
    nq, nc = seq // tq, seq // tc
    lam_init = 0.8 - 0.6 * math.exp(-0.3 * layer_idx)
    slopes = np.array([2.0 ** (-8.0 * (h + 1) / DA_HEADS) for h in range(DA_HEADS)], dtype=np.float32)
    slope_tab = jnp.asarray(np.broadcast_to(slopes[:, None, None], (DA_HEADS, SUBLANES, HEAD_W)).copy())
    da_w = DA_HEADS * HEAD_W
    vt5 = proj[:, 2 * da_w:3 * da_w].reshape(batch, nc, tc, DA_HEADS, HEAD_W).transpose(0, 3, 1, 4, 2)
    ones_tile = jnp.zeros((2 * SUBLANES, tc), BF16).at[0].set(1)
    vt5 = jnp.concatenate([vt5, jnp.broadcast_to(ones_tile, vt5.shape[:3] + ones_tile.shape)], axis=3)
    vrows = HEAD_W + 2 * SUBLANES
    return pl.pallas_call(
        functools.partial(_diff_attn_kernel, tq=tq, tc=tc, lam_init=lam_init),
        grid=(batch, DA_HEADS, nq),
        in_specs=[pl.BlockSpec((4, DA_HALF), lambda b, h, i: (0, 0)),
                  pl.BlockSpec((HEAD_W, 1), lambda b, h, i: (0, 0)),
                  pl.BlockSpec((None, SUBLANES, HEAD_W), lambda b, h, i: (h, 0, 0)),
                  pl.BlockSpec((tq, HEAD_W), lambda b, h, i: (b * nq + i, h)),
                  pl.BlockSpec((seq, HEAD_W), lambda b, h, i: (b, DA_HEADS + h)),
                  pl.BlockSpec((None, None, nc, vrows, tc), lambda b, h, i: (b, h, 0, 0, 0))],
        out_specs=pl.BlockSpec((tq, HEAD_W), lambda b, h, i: (b * nq + i, h)),
        out_shape=jax.ShapeDtypeStruct((n, da_w), BF16),
        scratch_shapes=[pltpu.VMEM((tc, 2 * tq), F32), pltpu.VMEM((vrows, 2 * tq), F32)],
        compiler_params=_cparams(("parallel", "parallel", "arbitrary")),
        name="diff_attention",
    )(lam4, subln_g.reshape(HEAD_W, 1).astype(F32), slope_tab, proj, proj, vt5)


def _hgrn_chunk(q, k, b, v):
    c = q.shape[0]
    row = lax.broadcasted_iota(jnp.int32, (c, 1), 0)
    arow = lax.broadcasted_iota(jnp.int32, (c, c), 0)
    acol = lax.broadcasted_iota(jnp.int32, (c, c), 1)

    def block_level(half):
        grp = 2 * half
        bg = b.reshape(c // grp, grp, HEAD_W)
        ref = jnp.broadcast_to(bg[:, half - 1:half, :], bg.shape).reshape(c, HEAD_W)
        upper = (row % grp) >= half
        qs = jnp.where(upper, q * jnp.exp2(jnp.minimum(b - ref, 0.0)), 0.0)
        ks = jnp.where(upper, 0.0, k * jnp.exp2(jnp.minimum(ref - b, 0.0)))
        a = _nt_dot(qs.astype(BF16), ks.astype(BF16))
        keep = ((arow // grp) == (acol // grp)) & ((arow % grp) >= half) & ((acol % grp) < half)
        return jnp.where(keep, a, 0.0)

    a_tot = None
    half = c // 2
    while half >= HG_SUB:
        lvl = block_level(half)
        a_tot = lvl if a_tot is None else a_tot + lvl
        half //= 2
    o = jnp.dot(a_tot.astype(BF16), v, preferred_element_type=F32)

    nsub = c // HG_SUB
    q3 = q.reshape(nsub, HG_SUB, HEAD_W)
    k3 = k.reshape(nsub, HG_SUB, HEAD_W)
    b3 = b.reshape(nsub, HG_SUB, HEAD_W)
    v3 = v.astype(F32).reshape(nsub, HG_SUB, HEAD_W)
    trow = lax.broadcasted_iota(jnp.int32, (nsub, HG_SUB, 1), 1)
    o3 = jnp.zeros((nsub, HG_SUB, HEAD_W), F32)
    for s in range(HG_SUB):
        e = jnp.exp2(jnp.minimum(b3 - b3[:, s:s + 1, :], 0.0))
        col = jnp.sum(q3 * (k3[:, s:s + 1, :] * e), axis=-1, keepdims=True)
        o3 = o3 + jnp.where(trow >= s, col, 0.0) * v3[:, s:s + 1, :]
    return o + o3.reshape(c, HEAD_W)


def _hgrn_kernel(gam_ref, ng_ref, q_ref, f_ref, i_ref, g_ref, o_ref, st_ref, *, n_chunks, layer_idx):
    @pl.when(pl.program_id(2) == 0)
    def _():
        st_ref[...] = jnp.zeros_like(st_ref)

    gam = gam_ref[...]
    ge = jnp.exp(gam - jnp.max(gam, axis=0, keepdims=True))
    lb = jnp.sum(ge[0:layer_idx + 1], axis=0, keepdims=True) / jnp.sum(ge, axis=0, keepdims=True)
    ng = ng_ref[...]
    c = q_ref.shape[0] // n_chunks
    tri = (lax.broadcasted_iota(jnp.int32, (c, c), 0) >= lax.broadcasted_iota(jnp.int32, (c, c), 1))
    tri = jnp.where(tri, 1.0, 0.0).astype(BF16)

    def chunk(ci, _):
        r0 = pl.multiple_of(ci * c, c)
        qr = q_ref[pl.ds(r0, c), :].astype(F32)
        fr = f_ref[pl.ds(r0, c), :].astype(F32)
        v = i_ref[pl.ds(r0, c), :]
        gr = g_ref[pl.ds(r0, c), :].astype(F32)
        q = qr * _sigmoid(qr)
        f = lb + (1.0 - lb) * _sigmoid(fr)
        logf = jnp.log2(f)
        k = 1.0 - f
        hi = logf.astype(BF16)
        r1 = logf - hi.astype(F32)
        mid = r1.astype(BF16)
        lo = (r1 - mid.astype(F32)).astype(BF16)
        b = (jnp.dot(tri, hi, preferred_element_type=F32) + jnp.dot(tri, mid, preferred_element_type=F32)
             + jnp.dot(tri, lo, preferred_element_type=F32))
        st = st_ref[...]
        o = _nt_dot((q * jnp.exp2(b)).astype(BF16), st.astype(BF16))
        o = o + _hgrn_chunk(q, k, b, v)
        bend = b[c - 1:c]
        kd = (k * jnp.exp2(bend - b)).astype(BF16)
        st_ref[...] = st * jnp.exp2(bend) + _tn_dot(v, kd)
        ms = jnp.mean(o * o, axis=-1, keepdims=True)
        on = o * lax.rsqrt(ms + EPS) * ng * (gr * _sigmoid(gr))
        o_ref[pl.ds(r0, c), :] = on.astype(o_ref.dtype)
        return 0

    per_trip = max(g for g in (4, 2, 1) if n_chunks % g == 0)

    def trip(i, _):
        for dj in range(per_trip):
            chunk(per_trip * i + dj, 0)
        return 0

    lax.fori_loop(0, n_chunks // per_trip, trip, 0)


def hgrn2(proj, gamma, norm_g, batch, seq, *, tc=2048, layer_idx=0, col0=3 * DA_HEADS):
    n = proj.shape[0]
    tc = min(tc, seq)
    ng = seq // tc
    depth1 = gamma.shape[0]

    def spec(off):
        return pl.BlockSpec((tc, HEAD_W), lambda b, h, g: (b * ng + g, col0 + off * HG_HEADS + h))

    return pl.pallas_call(
        functools.partial(_hgrn_kernel, n_chunks=tc // min(HG_BLOCK, tc), layer_idx=layer_idx),
        grid=(batch, HG_HEADS, ng),
        in_specs=[pl.BlockSpec((depth1, HEAD_W), lambda b, h, g: (0, h)),
                  pl.BlockSpec((1, HEAD_W), lambda b, h, g: (0, 0)),
                  spec(0), spec(1), spec(2), spec(3)],
        out_specs=pl.BlockSpec((tc, HEAD_W), lambda b, h, g: (b * ng + g, h)),
        out_shape=jax.ShapeDtypeStruct((n, HG_HEADS * HEAD_W), BF16),
        scratch_shapes=[pltpu.VMEM((HEAD_W, HEAD_W), F32)],
        compiler_params=_cparams(("parallel", "parallel", "arbitrary")),
        name="hgrn2",
    )(gamma.astype(F32), norm_g.reshape(1, HEAD_W).astype(F32), proj, proj, proj, proj)


def _cross_attn_kernel(q_ref, kv_ref, o_ref, *, scale):
    d = q_ref.shape[1]
    hd = d // XA_HEADS
    for h in range(XA_HEADS):
        cols = slice(h * hd, (h + 1) * hd)
        q = q_ref[:, cols] * jnp.asarray(scale, BF16)
        s = _nt_dot(q, kv_ref[:, cols])
        m = jnp.max(s, axis=-1, keepdims=True)
        p = jnp.exp(s - m)
        l = jnp.sum(p, axis=-1, keepdims=True)
        o = jnp.dot(p.astype(BF16), kv_ref[:, d + h * hd:d + (h + 1) * hd], preferred_element_type=F32) / l
        o_ref[:, cols] = o.astype(o_ref.dtype)


def cross_attention(q, kv, batch, seq, mem_len, *, tq=1024):
    n, d = q.shape
    tq = min(tq, seq)
    nq = seq // tq
    return pl.pallas_call(
        functools.partial(_cross_attn_kernel, scale=(d // XA_HEADS) ** -0.5),
        grid=(batch, nq),
        in_specs=[pl.BlockSpec((tq, d), lambda b, i: (b * nq + i, 0)),
                  pl.BlockSpec((mem_len, 2 * d), lambda b, i: (b, 0))],
        out_specs=pl.BlockSpec((tq, d), lambda b, i: (b * nq + i, 0)),
        out_shape=jax.ShapeDtypeStruct((n, d), BF16),
        compiler_params=_cparams(("parallel", "parallel")),
        name="cross_attention",
    )(q, kv)


def _topk_desc(s, k):
    rows = s.shape[0]

    def extract(first_only):
        iota = lax.broadcasted_iota(jnp.int32, s.shape, 0)
        rank = jnp.full(s.shape, NOT_SELECTED, F32)
        vals = []
        cur = s
        for r in range(k):
            m = jnp.max(cur, axis=0, keepdims=True)
            sel = cur == m
            if first_only:
                sel = iota == jnp.min(jnp.where(sel, iota, rows), axis=0, keepdims=True)
            rank = jnp.where(sel, float(r), rank)
            cur = jnp.where(sel, -jnp.inf, cur)
            vals.append(m)
        return jnp.concatenate(vals, axis=0), rank, cur

    sv, rank, cur = extract(False)
    removed = jnp.sum(jnp.where(cur == -jnp.inf, 1.0, 0.0), axis=0, keepdims=True)
    return lax.cond(jnp.max(removed) == float(k), lambda: (sv, rank), lambda: extract(True)[:2])


def _peer_topk_kernel(q_ref, keys_ref, l1_ref, c_ref, r2_ref, e2_ref):
    kk = PEER_TOPK
    for h in range(PEER_HEADS):
        q1 = q_ref[:, (2 * h) * PEER_NKEYS:(2 * h + 1) * PEER_NKEYS]
        q2 = q_ref[:, (2 * h + 1) * PEER_NKEYS:(2 * h + 2) * PEER_NKEYS]
        s1 = _nt_dot(keys_ref[h, 0], q1)
        s2 = _nt_dot(keys_ref[h, 1], q2)
        t = s1.shape[1]
        sv12, rank12 = _topk_desc(jnp.concatenate([s1, s2], axis=1), kk)
        sv1, sv2, rank1, rank2 = sv12[:, :t], sv12[:, t:], rank12[:, :t], rank12[:, t:]
        segs, seg_rows = [], []
        for a in range(kk):
            nb = kk // (a + 1)
            if nb > 1:
                rows = -(-nb // SUBLANES) * SUBLANES
                seg = sv1[a:a + 1] + sv2[0:rows]
                if rows != nb:
                    seg = jnp.where(lax.broadcasted_iota(jnp.int32, seg.shape, 0) < nb, seg, -jnp.inf)
                segs.append(seg)
                seg_rows.append(rows)
        single0 = len(segs)
        n_single = kk - single0
        assert n_single % SUBLANES == 0
        segs.append(sv1[single0:kk] + sv2[0:1])
        cand = jnp.concatenate(segs, axis=0)
        best0 = sv1[0:1] + sv2[0:1]

        def pick(first_only, cand=cand, best0=best0):
            iota = lax.broadcasted_iota(jnp.int32, cand.shape, 0)
            picked = jnp.zeros(cand.shape, F32)
            z = jnp.zeros((1, t), F32)
            cur = cand
            for _ in range(kk):
                m = jnp.max(cur, axis=0, keepdims=True)
                sel = cur == m
                if first_only:
                    sel = iota == jnp.min(jnp.where(sel, iota, cand.shape[0]), axis=0, keepdims=True)
                cur = jnp.where(sel, -jnp.inf, cur)
                picked = jnp.where(sel, 1.0, picked)
                z = z + jnp.exp(m - best0)
            return picked, z

        picked, z = pick(False)
        n_picked = jnp.sum(picked, axis=0, keepdims=True)
        picked, z = lax.cond(jnp.max(n_picked) == float(kk), lambda p=picked, zz=z: (p, zz),
                             lambda: pick(True))
        count, r0 = [], 0
        for rows in seg_rows:
            count.append(jnp.sum(picked[r0:r0 + rows], axis=0, keepdims=True))
            r0 += rows
        count += [picked[r0 + i:r0 + i + 1] for i in range(n_single)]
        l1 = jnp.zeros_like(rank1)
        for a in range(kk):
            l1 = jnp.where(rank1 == float(a), count[a], l1)
        sel1 = rank1 < float(kk)
        l1_ref[h] = l1
        c_ref[h] = jnp.where(sel1, jnp.exp(jnp.minimum(s1 - sv1[0:1], 0.0)) / z, 0.0)
        r2_ref[h] = rank2.astype(r2_ref.dtype)
        e2 = jnp.where(rank2 < float(kk), jnp.exp(jnp.minimum(s2 - sv2[0:1], 0.0)), 0.0)
        e2_ref[h] = e2.astype(e2_ref.dtype)


def peer_topk(q, keys, *, tt=256):
    n = q.shape[0]
    tt = min(tt, n)
    tab_spec = pl.BlockSpec((PEER_HEADS, PEER_NKEYS, tt), lambda i: (0, 0, i))
    return pl.pallas_call(
        _peer_topk_kernel,
        grid=(n // tt,),
        in_specs=[pl.BlockSpec((tt, q.shape[1]), lambda i: (i, 0)),
                  pl.BlockSpec(keys.shape, lambda i: (0, 0, 0, 0))],
        out_specs=[tab_spec] * 4,
        out_shape=[jax.ShapeDtypeStruct((PEER_HEADS, PEER_NKEYS, n), dt) for dt in (F32, F32, BF16, BF16)],
        compiler_params=_cparams(("parallel",)),
        name="peer_topk",
    )(q, keys)


def _gelu_exact(x):
    return 0.5 * x * (1.0 + lax.erf(x * (2.0 ** -0.5)))


def _peer_dense_kernel(hn_ref, down_ref, up_ref, l1_ref, c_ref, r2_ref, e2_ref, o_ref, wg_ref, *, groups):
    s = pl.program_id(1)
    row0 = (s * groups) % SUBLANES

    @pl.when(s == 0)
    def _():
        o_ref[...] = jnp.zeros_like(o_ref)

    tt = o_ref.shape[1]
    lane_chunk = 256
    halves = 2
    gh = groups // halves
    wgs = []
    for hf in range(halves):
        er = slice(hf * gh * PEER_NKEYS, (hf + 1) * gh * PEER_NKEYS)
        act = _nt_dot(down_ref[er, :], hn_ref[...])
        for gl in range(gh):
            gi = hf * gh + gl
            rows = slice(gi * PEER_NKEYS, (gi + 1) * PEER_NKEYS)
            lrows = slice(gl * PEER_NKEYS, (gl + 1) * PEER_NKEYS)
            for t0 in range(0, tt, lane_chunk):
                lanes = slice(t0, t0 + lane_chunk)
                g = None
                for h in range(PEER_HEADS):
                    l1 = l1_ref[h, pl.ds(row0 + gi, 1), lanes].astype(BF16)
                    cc = c_ref[h, pl.ds(row0 + gi, 1), lanes].astype(BF16)
                    e2 = e2_ref[h, :, lanes]
                    term = jnp.where(r2_ref[h, :, lanes] < l1, e2 * cc, jnp.zeros_like(e2))
                    g = term if g is None else g + term
                wg_ref[rows, lanes] = g * _gelu_exact(act[lrows, lanes]).astype(BF16)
        wgs.append(wg_ref[er, :])
    acc = o_ref[...]
    for hf in range(halves):
        er = slice(hf * gh * PEER_NKEYS, (hf + 1) * gh * PEER_NKEYS)
        acc = acc + _tn_dot(up_ref[er, :], wgs[hf])
    o_ref[...] = acc


def peer_dense(hn, down, up, tables, *, tt=512, te=512):
    n, d = hn.shape
    n_exp = down.shape[0]
    tt = min(tt, n)
    groups = te // PEER_NKEYS
    nblk = n_exp // te
    l1, c, r2, e2 = tables
    assert SUBLANES % groups == 0 and groups % 2 == 0
    row_spec = pl.BlockSpec((PEER_HEADS, SUBLANES, tt), lambda i, s: (0, s * groups // SUBLANES, i))
    tab_spec = pl.BlockSpec((PEER_HEADS, PEER_NKEYS, tt), lambda i, s: (0, 0, i))
    return pl.pallas_call(
        functools.partial(_peer_dense_kernel, groups=groups),
        grid=(n // tt, nblk),
        in_specs=[pl.BlockSpec((tt, d), lambda i, s: (i, 0)),
                  pl.BlockSpec((te, d), lambda i, s: (s, 0)),
                  pl.BlockSpec((te, d), lambda i, s: (s, 0)),
                  row_spec, row_spec, tab_spec, tab_spec],
        out_specs=pl.BlockSpec((d, tt), lambda i, s: (0, i)),
        out_shape=jax.ShapeDtypeStruct((d, n), F32),
        scratch_shapes=[pltpu.VMEM((te, tt), BF16)],
        compiler_params=_cparams(("parallel", "arbitrary")),
        name="peer_dense",
    )(hn, down, up, l1, c, r2, e2)


def _final_norm_kernel(h_ref, pt_ref, g_ref, o_ref):
    x = h_ref[...] + pt_ref[...].T
    ms = jnp.mean(x * x, axis=-1, keepdims=True)
    o_ref[...] = x * lax.rsqrt(ms + EPS) * g_ref[...]


def final_norm(h, pt, g, *, tm=256):
    m, d = h.shape
    tm = min(tm, m)
    row = pl.BlockSpec((tm, d), lambda i: (i, 0))
    return pl.pallas_call(
        _final_norm_kernel,
        grid=(m // tm,),
        in_specs=[row, pl.BlockSpec((d, tm), lambda i: (0, i)), pl.BlockSpec((1, d), lambda i: (0, 0))],
        out_specs=row,
        out_shape=jax.ShapeDtypeStruct((m, d), F32),
        compiler_params=_cparams(("parallel",)),
        name="final_norm",
    )(h, pt, g.reshape(1, d).astype(F32))


def kernel(x, mem, mix_norm_g, w_in, da_lambda_q1, da_lambda_k1, da_lambda_q2, da_lambda_k2, da_subln_g, hgrn_gamma, hgrn_norm_g, w_out, cross_norm_g, mem_norm_g, w_cq, w_ckv, w_co, ffn_norm_g, peer_wq, peer_subkeys, peer_down, peer_up, final_norm_g):
    batch, seq, d = x.shape
    mem_len = mem.shape[1]
    depth = w_in.shape[0]
    assert depth == 1 and hgrn_gamma.shape[0] == depth + 1
    l = 0
    da_w = DA_HEADS * HEAD_W
    x2 = x.reshape(batch * seq, d)

    proj = norm_matmul(x2, mix_norm_g[l], w_in[l].astype(BF16), tn=w_in.shape[-1] // 8)
    lam4 = jnp.stack([da_lambda_q1[l], da_lambda_k1[l], da_lambda_q2[l], da_lambda_k2[l]]).astype(F32)
    a_out = diff_attention(proj, lam4, da_subln_g[l], batch, seq, layer_idx=l)
    b_out = hgrn2(proj, hgrn_gamma, hgrn_norm_g[l], batch, seq, layer_idx=l)
    wo = w_out[l].astype(BF16)
    h1 = matmul_residual([a_out, b_out], [wo[:da_w], wo[da_w:]], x2)

    qc = norm_matmul(h1, cross_norm_g[l], w_cq[l].astype(BF16))
    kv = norm_matmul(mem.reshape(batch * mem_len, d), mem_norm_g[l], w_ckv[l].astype(BF16))
    oc = cross_attention(qc, kv, batch, seq, mem_len)
    h2 = matmul_residual([oc], [w_co[l].astype(BF16)], h1)

    qp, hn = norm_matmul(h2, ffn_norm_g[l], peer_wq[l].astype(BF16), emit_xn=True)
    tables = peer_topk(qp, peer_subkeys[l].astype(BF16))
    peer_t = peer_dense(hn, peer_down[l].astype(BF16), peer_up[l].astype(BF16), tables)
    out = final_norm(h2, peer_t, final_norm_g)
    return out.reshape(batch, seq, d)
```

```python
import functools
import math

import numpy as np
import jax
import jax.numpy as jnp
from jax import lax
from jax.experimental import pallas as pl
from jax.experimental.pallas import tpu as pltpu

F32 = jnp.float32
BF16 = jnp.bfloat16
EPS = 1e-6
NEG_BIG = -1e30

DA_HEADS = 16
DA_HALF = 64
HEAD_W = 128
HG_HEADS = 16
HG_BLOCK = 256
HG_SUB = 8
XA_HEADS = 4
PEER_HEADS = 8
PEER_NKEYS = 128
PEER_TOPK = 16
NOT_SELECTED = 99.0
SUBLANES = 8

VMEM_LIMIT = 58 * 1024 * 1024


def _cparams(sem):
    return pltpu.CompilerParams(dimension_semantics=sem, vmem_limit_bytes=VMEM_LIMIT)


def _nt_dot(a, b):
    return lax.dot_general(a, b, (((1,), (1,)), ((), ())), preferred_element_type=F32)


def _tn_dot(a, b):
    return lax.dot_general(a, b, (((0,), (0,)), ((), ())), preferred_element_type=F32)


def _sigmoid(x):
    return 0.5 * jnp.tanh(0.5 * x) + 0.5


def _norm_matmul_kernel(x_ref, g_ref, w_ref, o_ref, xn_ref, *, rows):
    @pl.when(pl.program_id(1) == 0)
    def _():
        g = g_ref[...]

        def chunk(ci, _):
            r0 = pl.multiple_of(ci * rows, rows)
            x = x_ref[pl.ds(r0, rows), :]
            ms = jnp.mean(x * x, axis=-1, keepdims=True)
            xn_ref[pl.ds(r0, rows), :] = (x * lax.rsqrt(ms + EPS) * g).astype(BF16)
            return 0

        lax.fori_loop(0, x_ref.shape[0] // rows, chunk, 0)

    o_ref[...] = jnp.dot(xn_ref[...], w_ref[...], preferred_element_type=F32).astype(o_ref.dtype)


def norm_matmul(x, g, w, *, tm=512, tn=1024, emit_xn=False):
    m, k = x.shape
    n = w.shape[1]
    tm, tn = min(tm, m), min(tn, n)
    assert m % tm == 0 and n % tn == 0
    rows = min(128, tm)
    out_shape = [jax.ShapeDtypeStruct((m, n), BF16)]
    out_specs = [pl.BlockSpec((tm, tn), lambda i, j: (i, j))]
    scratch = [pltpu.VMEM((tm, k), BF16)]
    if emit_xn:
        out_shape.append(jax.ShapeDtypeStruct((m, k), BF16))
        out_specs.append(pl.BlockSpec((tm, k), lambda i, j: (i, 0)))
        scratch = []
    res = pl.pallas_call(
        functools.partial(_norm_matmul_kernel, rows=rows),
        grid=(m // tm, n // tn),
        in_specs=[pl.BlockSpec((tm, k), lambda i, j: (i, 0)),
                  pl.BlockSpec((1, k), lambda i, j: (0, 0)),
                  pl.BlockSpec((k, tn), lambda i, j: (0, j))],
        out_specs=out_specs,
        out_shape=out_shape,
        scratch_shapes=scratch,
        compiler_params=_cparams(("parallel", "arbitrary")),
        name="norm_matmul",
    )(x, g.reshape(1, k).astype(F32), w)
    return res if emit_xn else res[0]


def _matmul_residual_kernel(*refs, n_lhs):
    lhs = refs[:n_lhs]
    ws = refs[n_lhs:2 * n_lhs]
    res_ref, o_ref = refs[2 * n_lhs], refs[2 * n_lhs + 1]
    acc = res_ref[...]
    for a_ref, w_ref in zip(lhs, ws):
        acc = acc + jnp.dot(a_ref[...], w_ref[...], preferred_element_type=F32)
    o_ref[...] = acc


def matmul_residual(lhs_list, w_list, res, *, tm=1024, tn=1024):
    m, n = res.shape
    tm, tn = min(tm, m), min(tn, n)
    assert m % tm == 0 and n % tn == 0
    n_lhs = len(lhs_list)
    in_specs = [pl.BlockSpec((tm, a.shape[1]), lambda i, j: (i, 0)) for a in lhs_list]
    in_specs += [pl.BlockSpec((w.shape[0], tn), lambda i, j: (0, j)) for w in w_list]
    in_specs += [pl.BlockSpec((tm, tn), lambda i, j: (i, j))]
    return pl.pallas_call(
        functools.partial(_matmul_residual_kernel, n_lhs=n_lhs),
        grid=(m // tm, n // tn),
        in_specs=in_specs,
        out_specs=pl.BlockSpec((tm, tn), lambda i, j: (i, j)),
        out_shape=jax.ShapeDtypeStruct((m, n), F32),
        compiler_params=_cparams(("parallel", "parallel")),
        name="matmul_residual",
    )(*lhs_list, *w_list, res)


def _diff_attn_kernel(lam_ref, subg_ref, slope_ref, q_ref, k_ref, vt_ref, o_ref, base_ref, acc_ref, s_ref, *,
                      tq, tc, lam_init):
    qi = pl.program_id(2)
    ncol = 2 * tq
    per_q = tq // tc
    log2e = math.log2(math.e)
    q = (q_ref[...].astype(F32) * (DA_HALF ** -0.5 * log2e)).astype(BF16)
    lane = lax.broadcasted_iota(jnp.int32, q.shape, 1)
    zero = jnp.zeros_like(q)
    q2 = jnp.concatenate([jnp.where(lane < DA_HALF, q, zero), jnp.where(lane >= DA_HALF, q, zero)], axis=0)
    slope = slope_ref[0:1, 0:1] * log2e
    krow = lax.broadcasted_iota(jnp.int32, (tc, ncol), 0)

    @pl.when(qi == 0)
    def _():
        base_ref[...] = slope * krow.astype(F32)

    acc_ref[...] = jnp.zeros_like(acc_ref)

    def offset(j):
        return slope * (j * tc - qi * tq).astype(F32)

    def scores(j, m, masked):
        kj = k_ref[pl.ds(pl.multiple_of(j * tc, tc), tc), :]
        s = _nt_dot(kj, q2) + base_ref[...]
        if masked:
            qpos = lax.broadcasted_iota(jnp.int32, (tc, ncol), 1) % tq
            s = jnp.where(krow + (j * tc - qi * tq) <= qpos, s, NEG_BIG)
        s_ref[j] = s
        return jnp.maximum(m, jnp.max(s, axis=0, keepdims=True) + offset(j))

    n_full = per_q * qi
    m = jnp.full((1, ncol), NEG_BIG, F32)
    m = lax.fori_loop(0, n_full // 2, lambda it, mm: scores(2 * it + 1, scores(2 * it, mm, False), False), m)
    m = lax.cond(n_full % 2 == 1, lambda mm: scores(n_full - 1, mm, False), lambda mm: mm, m)
    for u in range(per_q):
        m = scores(n_full + u, m, True)

    def weigh(j):
        p = jnp.exp2((s_ref[j] - (m - offset(j))).astype(BF16))
        acc_ref[...] += jnp.dot(vt_ref[j], p, preferred_element_type=F32)
        return 0

    def weigh_pair(it, _):
        weigh(2 * it)
        return weigh(2 * it + 1)

    n_all = n_full + per_q
    lax.fori_loop(0, n_all // 2, weigh_pair, 0)
    lax.cond(n_all % 2 == 1, lambda: weigh(n_all - 1), lambda: 0)

    lv = lam_ref[...]
    lam = (jnp.exp(jnp.sum(lv[0:1] * lv[1:2], axis=-1, keepdims=True))
           - jnp.exp(jnp.sum(lv[2:3] * lv[3:4], axis=-1, keepdims=True)) + lam_init)
    on = acc_ref[0:HEAD_W, :] / acc_ref[HEAD_W:HEAD_W + 1, :]
    o = on[:, :tq] - lam * on[:, tq:]
    ms = jnp.mean(o * o, axis=0, keepdims=True)
    o = o * lax.rsqrt(ms + EPS) * subg_ref[...] * (1.0 - lam_init)
    o_ref[...] = o.T.astype(o_ref.dtype)


def diff_attention(proj, lam4, subln_g, batch, seq, *, tq=512, tc=512, layer_idx=0):
    n = proj.shape[0]
    tq = min(tq, seq)
    tc = min(tc, tq)
    nq, nc = seq // tq, seq // tc
    lam_init = 0.8 - 0.6 * math.exp(-0.3 * layer_idx)
    slopes = np.array([2.0 ** (-8.0 * (h + 1) / DA_HEADS) for h in range(DA_HEADS)], dtype=np.float32)
    slope_tab = jnp.asarray(np.broadcast_to(slopes[:, None, None], (DA_HEADS, SUBLANES, HEAD_W)).copy())
    da_w = DA_HEADS * HEAD_W
    vt5 = proj[:, 2 * da_w:3 * da_w].reshape(batch, nc, tc, DA_HEADS, HEAD_W).transpose(0, 3, 1, 4, 2)
    ones_tile = jnp.zeros((2 * SUBLANES, tc), BF16).at[0].set(1)
    vt5 = jnp.concatenate([vt5, jnp.broadcast_to(ones_tile, vt5.shape[:3] + ones_tile.shape)], axis=3)
    vrows = HEAD_W + 2 * SUBLANES
    return pl.pallas_call(
        functools.partial(_diff_attn_kernel, tq=tq, tc=tc, lam_init=lam_init),
        grid=(batch, DA_HEADS, nq),
        in_specs=[pl.BlockSpec((4, DA_HALF), lambda b, h, i: (0, 0)),
                  pl.BlockSpec((HEAD_W, 1), lambda b, h, i: (0, 0)),
                  pl.BlockSpec((None, SUBLANES, HEAD_W), lambda b, h, i: (h, 0, 0)),
                  pl.BlockSpec((tq, HEAD_W), lambda b, h, i: (b * nq + i, h)),
                  pl.BlockSpec((seq, HEAD_W), lambda b, h, i: (b, DA_HEADS + h)),
                  pl.BlockSpec((None, None, nc, vrows, tc), lambda b, h, i: (b, h, 0, 0, 0))],
        out_specs=pl.BlockSpec((tq, HEAD_W), lambda b, h, i: (b * nq + i, h)),
        out_shape=jax.ShapeDtypeStruct((n, da_w), BF16),
        scratch_shapes=[pltpu.VMEM((tc, 2 * tq), F32), pltpu.VMEM((vrows, 2 * tq), F32),
                        pltpu.VMEM((nc, tc, 2 * tq), F32)],
        compiler_params=_cparams(("parallel", "parallel", "arbitrary")),
        name="diff_attention",
    )(lam4, subln_g.reshape(HEAD_W, 1).astype(F32), slope_tab, proj, proj, vt5)


def _hgrn_chunk(q, k, b, v):
    c = q.shape[0]
    row = lax.broadcasted_iota(jnp.int32, (c, 1), 0)
    arow = lax.broadcasted_iota(jnp.int32, (c, c), 0)
    acol = lax.broadcasted_iota(jnp.int32, (c, c), 1)

    def block_level(half):
        grp = 2 * half
        bg = b.reshape(c // grp, grp, HEAD_W)
        ref = jnp.broadcast_to(bg[:, half - 1:half, :], bg.shape).reshape(c, HEAD_W)
        upper = (row % grp) >= half
        qs = jnp.where(upper, q * jnp.exp2(jnp.minimum(b - ref, 0.0)), 0.0)
        ks = jnp.where(upper, 0.0, k * jnp.exp2(jnp.minimum(ref - b, 0.0)))
        a = _nt_dot(qs.astype(BF16), ks.astype(BF16))
        keep = ((arow // grp) == (acol // grp)) & ((arow % grp) >= half) & ((acol % grp) < half)
        return jnp.where(keep, a, 0.0)

    a_tot = None
    half = c // 2
    while half >= HG_SUB:
        lvl = block_level(half)
        a_tot = lvl if a_tot is None else a_tot + lvl
        half //= 2
    o = jnp.dot(a_tot.astype(BF16), v, preferred_element_type=F32)

    nsub = c // HG_SUB
    q3 = q.reshape(nsub, HG_SUB, HEAD_W)
    k3 = k.reshape(nsub, HG_SUB, HEAD_W)
    b3 = b.reshape(nsub, HG_SUB, HEAD_W)
    v3 = v.astype(F32).reshape(nsub, HG_SUB, HEAD_W)
    trow = lax.broadcasted_iota(jnp.int32, (nsub, HG_SUB, 1), 1)
    o3 = jnp.zeros((nsub, HG_SUB, HEAD_W), F32)
    for s in range(HG_SUB):
        e = jnp.exp2(jnp.minimum(b3 - b3[:, s:s + 1, :], 0.0))
        col = jnp.sum(q3 * (k3[:, s:s + 1, :] * e), axis=-1, keepdims=True)
        o3 = o3 + jnp.where(trow >= s, col, 0.0) * v3[:, s:s + 1, :]
    return o + o3.reshape(c, HEAD_W)


def _hgrn_kernel(gam_ref, ng_ref, q_ref, f_ref, i_ref, g_ref, o_ref, st_ref, *, n_chunks, layer_idx):
    @pl.when(pl.program_id(2) == 0)
    def _():
        st_ref[...] = jnp.zeros_like(st_ref)

    gam = gam_ref[...]
    ge = jnp.exp(gam - jnp.max(gam, axis=0, keepdims=True))
    lb = jnp.sum(ge[0:layer_idx + 1], axis=0, keepdims=True) / jnp.sum(ge, axis=0, keepdims=True)
    ng = ng_ref[...]
    c = q_ref.shape[0] // n_chunks
    tri = (lax.broadcasted_iota(jnp.int32, (c, c), 0) >= lax.broadcasted_iota(jnp.int32, (c, c), 1))
    tri = jnp.where(tri, 1.0, 0.0).astype(BF16)

    def chunk(ci, _):
        r0 = pl.multiple_of(ci * c, c)
        qr = q_ref[pl.ds(r0, c), :].astype(F32)
        fr = f_ref[pl.ds(r0, c), :].astype(F32)
        v = i_ref[pl.ds(r0, c), :]
        gr = g_ref[pl.ds(r0, c), :].astype(F32)
        q = qr * _sigmoid(qr)
        f = lb + (1.0 - lb) * _sigmoid(fr)
        logf = jnp.log2(f)
        k = 1.0 - f
        hi = logf.astype(BF16)
        r1 = logf - hi.astype(F32)
        mid = r1.astype(BF16)
        lo = (r1 - mid.astype(F32)).astype(BF16)
        b = (jnp.dot(tri, hi, preferred_element_type=F32) + jnp.dot(tri, mid, preferred_element_type=F32)
             + jnp.dot(tri, lo, preferred_element_type=F32))
        st = st_ref[...]
        o = _nt_dot((q * jnp.exp2(b)).astype(BF16), st.astype(BF16))
        o = o + _hgrn_chunk(q, k, b, v)
        bend = b[c - 1:c]
        kd = (k * jnp.exp2(bend - b)).astype(BF16)
        st_ref[...] = st * jnp.exp2(bend) + _tn_dot(v, kd)
        ms = jnp.mean(o * o, axis=-1, keepdims=True)
        on = o * lax.rsqrt(ms + EPS) * ng * (gr * _sigmoid(gr))
        o_ref[pl.ds(r0, c), :] = on.astype(o_ref.dtype)
        return 0

    per_trip = max(g for g in (4, 2, 1) if n_chunks % g == 0)

    def trip(i, _):
        for dj in range(per_trip):
            chunk(per_trip * i + dj, 0)
        return 0

    lax.fori_loop(0, n_chunks // per_trip, trip, 0)


def hgrn2(proj, gamma, norm_g, batch, seq, *, tc=2048, layer_idx=0, col0=3 * DA_HEADS):
    n = proj.shape[0]
    tc = min(tc, seq)
    ng = seq // tc
    depth1 = gamma.shape[0]

    def spec(off):
        return pl.BlockSpec((tc, HEAD_W), lambda b, h, g: (b * ng + g, col0 + off * HG_HEADS + h))

    return pl.pallas_call(
        functools.partial(_hgrn_kernel, n_chunks=tc // min(HG_BLOCK, tc), layer_idx=layer_idx),
        grid=(batch, HG_HEADS, ng),
        in_specs=[pl.BlockSpec((depth1, HEAD_W), lambda b, h, g: (0, h)),
                  pl.BlockSpec((1, HEAD_W), lambda b, h, g: (0, 0)),
                  spec(0), spec(1), spec(2), spec(3)],
        out_specs=pl.BlockSpec((tc, HEAD_W), lambda b, h, g: (b * ng + g, h)),
        out_shape=jax.ShapeDtypeStruct((n, HG_HEADS * HEAD_W), BF16),
        scratch_shapes=[pltpu.VMEM((HEAD_W, HEAD_W), F32)],
        compiler_params=_cparams(("parallel", "parallel", "arbitrary")),
        name="hgrn2",
    )(gamma.astype(F32), norm_g.reshape(1, HEAD_W).astype(F32), proj, proj, proj, proj)


def _cross_attn_kernel(q_ref, kv_ref, o_ref, *, scale):
    d = q_ref.shape[1]
    hd = d // XA_HEADS
    for h in range(XA_HEADS):
        cols = slice(h * hd, (h + 1) * hd)
        q = q_ref[:, cols] * jnp.asarray(scale, BF16)
        s = _nt_dot(q, kv_ref[:, cols])
        m = jnp.max(s, axis=-1, keepdims=True)
        p = jnp.exp(s - m)
        l = jnp.sum(p, axis=-1, keepdims=True)
        o = jnp.dot(p.astype(BF16), kv_ref[:, d + h * hd:d + (h + 1) * hd], preferred_element_type=F32) / l
        o_ref[:, cols] = o.astype(o_ref.dtype)


def cross_attention(q, kv, batch, seq, mem_len, *, tq=1024):
    n, d = q.shape
    tq = min(tq, seq)
    nq = seq // tq
    return pl.pallas_call(
        functools.partial(_cross_attn_kernel, scale=(d // XA_HEADS) ** -0.5),
        grid=(batch, nq),
        in_specs=[pl.BlockSpec((tq, d), lambda b, i: (b * nq + i, 0)),
                  pl.BlockSpec((mem_len, 2 * d), lambda b, i: (b, 0))],
        out_specs=pl.BlockSpec((tq, d), lambda b, i: (b * nq + i, 0)),
        out_shape=jax.ShapeDtypeStruct((n, d), BF16),
        compiler_params=_cparams(("parallel", "parallel")),
        name="cross_attention",
    )(q, kv)


def _topk_desc(s, k):
    rows = s.shape[0]

    def extract(first_only):
        iota = lax.broadcasted_iota(jnp.int32, s.shape, 0)
        rank = jnp.full(s.shape, NOT_SELECTED, F32)
        vals = []
        cur = s
        for r in range(k):
            m = jnp.max(cur, axis=0, keepdims=True)
            sel = cur == m
            if first_only:
                sel = iota == jnp.min(jnp.where(sel, iota, rows), axis=0, keepdims=True)
            rank = jnp.where(sel, float(r), rank)
            cur = jnp.where(sel, -jnp.inf, cur)
            vals.append(m)
        return jnp.concatenate(vals, axis=0), rank, cur

    sv, rank, cur = extract(False)
    removed = jnp.sum(jnp.where(cur == -jnp.inf, 1.0, 0.0), axis=0, keepdims=True)
    return lax.cond(jnp.max(removed) == float(k), lambda: (sv, rank), lambda: extract(True)[:2])


def _peer_topk_kernel(q_ref, keys_ref, l1_ref, c_ref, r2_ref, e2_ref):
    kk = PEER_TOPK
    for h in range(PEER_HEADS):
        q1 = q_ref[:, (2 * h) * PEER_NKEYS:(2 * h + 1) * PEER_NKEYS]
        q2 = q_ref[:, (2 * h + 1) * PEER_NKEYS:(2 * h + 2) * PEER_NKEYS]
        s1 = _nt_dot(keys_ref[h, 0], q1)
        s2 = _nt_dot(keys_ref[h, 1], q2)
        t = s1.shape[1]
        sv12, rank12 = _topk_desc(jnp.concatenate([s1, s2], axis=1), kk)
        sv1, sv2, rank1, rank2 = sv12[:, :t], sv12[:, t:], rank12[:, :t], rank12[:, t:]
        segs, seg_rows = [], []
        for a in range(kk):
            nb = kk // (a + 1)
            if nb > 1:
                rows = -(-nb // SUBLANES) * SUBLANES
                seg = sv1[a:a + 1] + sv2[0:rows]
                if rows != nb:
                    seg = jnp.where(lax.broadcasted_iota(jnp.int32, seg.shape, 0) < nb, seg, -jnp.inf)
                segs.append(seg)
                seg_rows.append(rows)
        single0 = len(segs)
        n_single = kk - single0
        assert n_single % SUBLANES == 0
        segs.append(sv1[single0:kk] + sv2[0:1])
        cand = jnp.concatenate(segs, axis=0)
        best0 = sv1[0:1] + sv2[0:1]

        def pick(first_only, cand=cand, best0=best0):
            iota = lax.broadcasted_iota(jnp.int32, cand.shape, 0)
            picked = jnp.zeros(cand.shape, F32)
            z = jnp.zeros((1, t), F32)
            cur = cand
            for _ in range(kk):
                m = jnp.max(cur, axis=0, keepdims=True)
                sel = cur == m
                if first_only:
                    sel = iota == jnp.min(jnp.where(sel, iota, cand.shape[0]), axis=0, keepdims=True)
                cur = jnp.where(sel, -jnp.inf, cur)
                picked = jnp.where(sel, 1.0, picked)
                z = z + jnp.exp(m - best0)
            return picked, z

        picked, z = pick(False)
        n_picked = jnp.sum(picked, axis=0, keepdims=True)
        picked, z = lax.cond(jnp.max(n_picked) == float(kk), lambda p=picked, zz=z: (p, zz),
                             lambda: pick(True))
        count, r0 = [], 0
        for rows in seg_rows:
            count.append(jnp.sum(picked[r0:r0 + rows], axis=0, keepdims=True))
            r0 += rows
        count += [picked[r0 + i:r0 + i + 1] for i in range(n_single)]
        l1 = jnp.zeros_like(rank1)
        for a in range(kk):
            l1 = jnp.where(rank1 == float(a), count[a], l1)
        sel1 = rank1 < float(kk)
        l1_ref[h] = l1
        c_ref[h] = jnp.where(sel1, jnp.exp(jnp.minimum(s1 - sv1[0:1], 0.0)) / z, 0.0)
        r2_ref[h] = rank2.astype(r2_ref.dtype)
        e2 = jnp.where(rank2 < float(kk), jnp.exp(jnp.minimum(s2 - sv2[0:1], 0.0)), 0.0)
        e2_ref[h] = e2.astype(e2_ref.dtype)


def peer_topk(q, keys, *, tt=256):
    n = q.shape[0]
    tt = min(tt, n)
    tab_spec = pl.BlockSpec((PEER_HEADS, PEER_NKEYS, tt), lambda i: (0, 0, i))
    return pl.pallas_call(
        _peer_topk_kernel,
        grid=(n // tt,),
        in_specs=[pl.BlockSpec((tt, q.shape[1]), lambda i: (i, 0)),
                  pl.BlockSpec(keys.shape, lambda i: (0, 0, 0, 0))],
        out_specs=[tab_spec] * 4,
        out_shape=[jax.ShapeDtypeStruct((PEER_HEADS, PEER_NKEYS, n), dt) for dt in (F32, F32, BF16, BF16)],
        compiler_params=_cparams(("parallel",)),
        name="peer_topk",
    )(q, keys)


def _gelu_exact(x):
    return 0.5 * x * (1.0 + lax.erf(x * (2.0 ** -0.5)))


def _peer_dense_kernel(hn_ref, down_ref, up_ref, l1_ref, c_ref, r2_ref, e2_ref, o_ref, wg_ref, *, groups):
    s = pl.program_id(1)
    row0 = (s * groups) % SUBLANES

    @pl.when(s == 0)
    def _():
        o_ref[...] = jnp.zeros_like(o_ref)

    tt = o_ref.shape[1]
    lane_chunk = 256
    halves = 2
    gh = groups // halves
    wgs = []
    for hf in range(halves):
        er = slice(hf * gh * PEER_NKEYS, (hf + 1) * gh * PEER_NKEYS)
        act = _nt_dot(down_ref[er, :], hn_ref[...])
        for gl in range(gh):
            gi = hf * gh + gl
            rows = slice(gi * PEER_NKEYS, (gi + 1) * PEER_NKEYS)
            lrows = slice(gl * PEER_NKEYS, (gl + 1) * PEER_NKEYS)
            for t0 in range(0, tt, lane_chunk):
                lanes = slice(t0, t0 + lane_chunk)
                g = None
                for h in range(PEER_HEADS):
                    l1 = l1_ref[h, pl.ds(row0 + gi, 1), lanes].astype(BF16)
                    cc = c_ref[h, pl.ds(row0 + gi, 1), lanes].astype(BF16)
                    e2 = e2_ref[h, :, lanes]
                    term = jnp.where(r2_ref[h, :, lanes] < l1, e2 * cc, jnp.zeros_like(e2))
                    g = term if g is None else g + term
                wg_ref[rows, lanes] = g * _gelu_exact(act[lrows, lanes]).astype(BF16)
        wgs.append(wg_ref[er, :])
    acc = o_ref[...]
    for hf in range(halves):
        er = slice(hf * gh * PEER_NKEYS, (hf + 1) * gh * PEER_NKEYS)
        acc = acc + _tn_dot(up_ref[er, :], wgs[hf])
    o_ref[...] = acc


def peer_dense(hn, down, up, tables, *, tt=512, te=512):
    n, d = hn.shape
    n_exp = down.shape[0]
    tt = min(tt, n)
    groups = te // PEER_NKEYS
    nblk = n_exp // te
    l1, c, r2, e2 = tables
    assert SUBLANES % groups == 0 and groups % 2 == 0
    row_spec = pl.BlockSpec((PEER_HEADS, SUBLANES, tt), lambda i, s: (0, s * groups // SUBLANES, i))
    tab_spec = pl.BlockSpec((PEER_HEADS, PEER_NKEYS, tt), lambda i, s: (0, 0, i))
    return pl.pallas_call(
        functools.partial(_peer_dense_kernel, groups=groups),
        grid=(n // tt, nblk),
        in_specs=[pl.BlockSpec((tt, d), lambda i, s: (i, 0)),
                  pl.BlockSpec((te, d), lambda i, s: (s, 0)),
                  pl.BlockSpec((te, d), lambda i, s: (s, 0)),
                  row_spec, row_spec, tab_spec, tab_spec],
        out_specs=pl.BlockSpec((d, tt), lambda i, s: (0, i)),
        out_shape=jax.ShapeDtypeStruct((d, n), F32),
        scratch_shapes=[pltpu.VMEM((te, tt), BF16)],
        compiler_params=_cparams(("parallel", "arbitrary")),
        name="peer_dense",
    )(hn, down, up, l1, c, r2, e2)


def _final_norm_kernel(h_ref, pt_ref, g_ref, o_ref):
    x = h_ref[...] + pt_ref[...].T
    ms = jnp.mean(x * x, axis=-1, keepdims=True)
    o_ref[...] = x * lax.rsqrt(ms + EPS) * g_ref[...]


def final_norm(h, pt, g, *, tm=256):
    m, d = h.shape
    tm = min(tm, m)
    row = pl.BlockSpec((tm, d), lambda i: (i, 0))
    return pl.pallas_call(
        _final_norm_kernel,
        grid=(m // tm,),
        in_specs=[row, pl.BlockSpec((d, tm), lambda i: (0, i)), pl.BlockSpec((1, d), lambda i: (0, 0))],
        out_specs=row,
        out_shape=jax.ShapeDtypeStruct((m, d), F32),
        compiler_params=_cparams(("parallel",)),
        name="final_norm",
    )(h, pt, g.reshape(1, d).astype(F32))


def kernel(x, mem, mix_norm_g, w_in, da_lambda_q1, da_lambda_k1, da_lambda_q2, da_lambda_k2, da_subln_g, hgrn_gamma, hgrn_norm_g, w_out, cross_norm_g, mem_norm_g, w_cq, w_ckv, w_co, ffn_norm_g, peer_wq, peer_subkeys, peer_down, peer_up, final_norm_g):
    batch, seq, d = x.shape
    mem_len = mem.shape[1]
    depth = w_in.shape[0]
    assert depth == 1 and hgrn_gamma.shape[0] == depth + 1
    l = 0
    da_w = DA_HEADS * HEAD_W
    x2 = x.reshape(batch * seq, d)

    proj = norm_matmul(x2, mix_norm_g[l], w_in[l].astype(BF16), tn=w_in.shape[-1] // 8)
    lam4 = jnp.stack([da_lambda_q1[l], da_lambda_k1[l], da_lambda_q2[l], da_lambda_k2[l]]).astype(F32)
    a_out = diff_attention(proj, lam4, da_subln_g[l], batch, seq, layer_idx=l)
    b_out = hgrn2(proj, hgrn_gamma, hgrn_norm_g[l], batch, seq, layer_idx=l)
    wo = w_out[l].astype(BF16)
    h1 = matmul_residual([a_out, b_out], [wo[:da_w], wo[da_w:]], x2)

    qc = norm_matmul(h1, cross_norm_g[l], w_cq[l].astype(BF16))
    kv = norm_matmul(mem.reshape(batch * mem_len, d), mem_norm_g[l], w_ckv[l].astype(BF16))
    oc = cross_attention(qc, kv, batch, seq, mem_len)
    h2 = matmul_residual([oc], [w_co[l].astype(BF16)], h1)

    qp, hn = norm_matmul(h2, ffn_norm_g[l], peer_wq[l].astype(BF16), emit_xn=True)
    tables = peer_topk(qp, peer_subkeys[l].astype(BF16))
    peer_t = peer_dense(hn, peer_down[l].astype(BF16), peer_up[l].astype(BF16), tables)
    out = final_norm(h2, peer_t, final_norm_g)
    return out.reshape(batch, seq, d)
```

```python
import functools
import math

import numpy as np
import jax
import jax.numpy as jnp
from jax import lax
from jax.experimental import pallas as pl
from jax.experimental.pallas import tpu as pltpu

F32 = jnp.float32
BF16 = jnp.bfloat16
EPS = 1e-6
NEG_BIG = -1e30

DA_HEADS = 16
DA_HALF = 64
HEAD_W = 128
HG_HEADS = 16
HG_BLOCK = 256
HG_SUB = 8
XA_HEADS = 4
PEER_HEADS = 8
PEER_NKEYS = 128
PEER_TOPK = 16
NOT_SELECTED = 99.0
SUBLANES = 8

VMEM_LIMIT = 58 * 1024 * 1024


def _cparams(sem):
    return pltpu.CompilerParams(dimension_semantics=sem, vmem_limit_bytes=VMEM_LIMIT)


def _nt_dot(a, b):
    return lax.dot_general(a, b, (((1,), (1,)), ((), ())), preferred_element_type=F32)


def _tn_dot(a, b):
    return lax.dot_general(a, b, (((0,), (0,)), ((), ())), preferred_element_type=F32)


def _sigmoid(x):
    return 0.5 * jnp.tanh(0.5 * x) + 0.5


def _norm_matmul_kernel(x_ref, g_ref, w_ref, o_ref, xn_ref, *, rows):
    @pl.when(pl.program_id(1) == 0)
    def _():
        g = g_ref[...]

        def chunk(ci, _):
            r0 = pl.multiple_of(ci * rows, rows)
            x = x_ref[pl.ds(r0, rows), :]
            ms = jnp.mean(x * x, axis=-1, keepdims=True)
            xn_ref[pl.ds(r0, rows), :] = (x * lax.rsqrt(ms + EPS) * g).astype(BF16)
            return 0

        lax.fori_loop(0, x_ref.shape[0] // rows, chunk, 0)

    o_ref[...] = jnp.dot(xn_ref[...], w_ref[...], preferred_element_type=F32).astype(o_ref.dtype)


def norm_matmul(x, g, w, *, tm=512, tn=1024, emit_xn=False):
    m, k = x.shape
    n = w.shape[1]
    tm, tn = min(tm, m), min(tn, n)
    assert m % tm == 0 and n % tn == 0
    rows = min(128, tm)
    out_shape = [jax.ShapeDtypeStruct((m, n), BF16)]
    out_specs = [pl.BlockSpec((tm, tn), lambda i, j: (i, j))]
    scratch = [pltpu.VMEM((tm, k), BF16)]
    if emit_xn:
        out_shape.append(jax.ShapeDtypeStruct((m, k), BF16))
        out_specs.append(pl.BlockSpec((tm, k), lambda i, j: (i, 0)))
        scratch = []
    res = pl.pallas_call(
        functools.partial(_norm_matmul_kernel, rows=rows),
        grid=(m // tm, n // tn),
        in_specs=[pl.BlockSpec((tm, k), lambda i, j: (i, 0)),
                  pl.BlockSpec((1, k), lambda i, j: (0, 0)),
                  pl.BlockSpec((k, tn), lambda i, j: (0, j))],
        out_specs=out_specs,
        out_shape=out_shape,
        scratch_shapes=scratch,
        compiler_params=_cparams(("parallel", "arbitrary")),
        name="norm_matmul",
    )(x, g.reshape(1, k).astype(F32), w)
    return res if emit_xn else res[0]


def _matmul_residual_kernel(*refs, n_lhs):
    lhs = refs[:n_lhs]
    ws = refs[n_lhs:2 * n_lhs]
    res_ref, o_ref = refs[2 * n_lhs], refs[2 * n_lhs + 1]
    acc = res_ref[...]
    for a_ref, w_ref in zip(lhs, ws):
        acc = acc + jnp.dot(a_ref[...], w_ref[...], preferred_element_type=F32)
    o_ref[...] = acc


def matmul_residual(lhs_list, w_list, res, *, tm=1024, tn=1024):
    m, n = res.shape
    tm, tn = min(tm, m), min(tn, n)
    assert m % tm == 0 and n % tn == 0
    n_lhs = len(lhs_list)
    in_specs = [pl.BlockSpec((tm, a.shape[1]), lambda i, j: (i, 0)) for a in lhs_list]
    in_specs += [pl.BlockSpec((w.shape[0], tn), lambda i, j: (0, j)) for w in w_list]
    in_specs += [pl.BlockSpec((tm, tn), lambda i, j: (i, j))]
    return pl.pallas_call(
        functools.partial(_matmul_residual_kernel, n_lhs=n_lhs),
        grid=(m // tm, n // tn),
        in_specs=in_specs,
        out_specs=pl.BlockSpec((tm, tn), lambda i, j: (i, j)),
        out_shape=jax.ShapeDtypeStruct((m, n), F32),
        compiler_params=_cparams(("parallel", "parallel")),
        name="matmul_residual",
    )(*lhs_list, *w_list, res)


def _diff_attn_kernel(lam_ref, subg_ref, slope_ref, q_ref, k_ref, vt_ref, o_ref, base_ref, acc_ref, *,
                      tq, tc, lam_init):
    qi = pl.program_id(2)
    ncol = 2 * tq
    per_q = tq // tc
    log2e = math.log2(math.e)
    q = (q_ref[...].astype(F32) * (DA_HALF ** -0.5 * log2e)).astype(BF16)
    lane = lax.broadcasted_iota(jnp.int32, q.shape, 1)
    zero = jnp.zeros_like(q)
    q2 = jnp.concatenate([jnp.where(lane < DA_HALF, q, zero), jnp.where(lane >= DA_HALF, q, zero)], axis=0)
    slope = slope_ref[0:1, 0:1] * log2e
    krow = lax.broadcasted_iota(jnp.int32, (tc, ncol), 0)

    @pl.when(qi == 0)
    def _():
        base_ref[...] = slope * krow.astype(F32)

    acc_ref[...] = jnp.zeros_like(acc_ref)

    def process(j, m, masked):
        rel = j * tc - qi * tq
        off = slope * rel.astype(F32)
        kj = k_ref[pl.ds(pl.multiple_of(j * tc, tc), tc), :]
        s = _nt_dot(kj, q2) + base_ref[...]
        if masked:
            qpos = lax.broadcasted_iota(jnp.int32, (tc, ncol), 1) % tq
            s = jnp.where(krow + rel <= qpos, s, NEG_BIG)
        m_new = jnp.maximum(m, jnp.max(s, axis=0, keepdims=True) + off)
        p = jnp.exp2((s - (m_new - off)).astype(BF16))
        alpha = jnp.exp2(m - m_new)
        acc_ref[...] = acc_ref[...] * alpha + jnp.dot(vt_ref[j], p, preferred_element_type=F32)
        return m_new

    m = jnp.full((1, ncol), NEG_BIG, F32)
    n_full = per_q * qi
    m = lax.fori_loop(0, n_full // 2, lambda it, mm: process(2 * it + 1, process(2 * it, mm, False), False), m)
    m = lax.cond(n_full % 2 == 1, lambda mm: process(n_full - 1, mm, False), lambda mm: mm, m)
    for u in range(per_q):
        m = process(per_q * qi + u, m, True)

    lv = lam_ref[...]
    lam = (jnp.exp(jnp.sum(lv[0:1] * lv[1:2], axis=-1, keepdims=True))
           - jnp.exp(jnp.sum(lv[2:3] * lv[3:4], axis=-1, keepdims=True)) + lam_init)
    on = acc_ref[0:HEAD_W, :] / acc_ref[HEAD_W:HEAD_W + 1, :]
    o = on[:, :tq] - lam * on[:, tq:]
    ms = jnp.mean(o * o, axis=0, keepdims=True)
    o = o * lax.rsqrt(ms + EPS) * subg_ref[...] * (1.0 - lam_init)
    o_ref[...] = o.T.astype(o_ref.dtype)


def diff_attention(proj, lam4, subln_g, batch, seq, *, tq=512, tc=512, layer_idx=0):
    n = proj.shape[0]
    tq = min(tq, seq)
    tc = min(tc, tq)
    nq, nc = seq // tq, seq // tc
    lam_init = 0.8 - 0.6 * math.exp(-0.3 * layer_idx)
    slopes = np.array([2.0 ** (-8.0 * (h + 1) / DA_HEADS) for h in range(DA_HEADS)], dtype=np.float32)
    slope_tab = jnp.asarray(np.broadcast_to(slopes[:, None, None], (DA_HEADS, SUBLANES, HEAD_W)).copy())
    da_w = DA_HEADS * HEAD_W
    vt5 = proj[:, 2 * da_w:3 * da_w].reshape(batch, nc, tc, DA_HEADS, HEAD_W).transpose(0, 3, 1, 4, 2)
    ones_tile = jnp.zeros((2 * SUBLANES, tc), BF16).at[0].set(1)
    vt5 = jnp.concatenate([vt5, jnp.broadcast_to(ones_tile, vt5.shape[:3] + ones_tile.shape)], axis=3)
    vrows = HEAD_W + 2 * SUBLANES
    return pl.pallas_call(
        functools.partial(_diff_attn_kernel, tq=tq, tc=tc, lam_init=lam_init),
        grid=(batch, DA_HEADS, nq),
        in_specs=[pl.BlockSpec((4, DA_HALF), lambda b, h, i: (0, 0)),
                  pl.BlockSpec((HEAD_W, 1), lambda b, h, i: (0, 0)),
                  pl.BlockSpec((None, SUBLANES, HEAD_W), lambda b, h, i: (h, 0, 0)),
                  pl.BlockSpec((tq, HEAD_W), lambda b, h, i: (b * nq + i, h)),
                  pl.BlockSpec((seq, HEAD_W), lambda b, h, i: (b, DA_HEADS + h)),
                  pl.BlockSpec((None, None, nc, vrows, tc), lambda b, h, i: (b, h, 0, 0, 0))],
        out_specs=pl.BlockSpec((tq, HEAD_W), lambda b, h, i: (b * nq + i, h)),
        out_shape=jax.ShapeDtypeStruct((n, da_w), BF16),
        scratch_shapes=[pltpu.VMEM((tc, 2 * tq), F32), pltpu.VMEM((vrows, 2 * tq), F32)],
        compiler_params=_cparams(("parallel", "parallel", "arbitrary")),
        name="diff_attention",
    )(lam4, subln_g.reshape(HEAD_W, 1).astype(F32), slope_tab, proj, proj, vt5)


def _hgrn_chunk(q, k, b, v):
    c = q.shape[0]
    row = lax.broadcasted_iota(jnp.int32, (c, 1), 0)
    arow = lax.broadcasted_iota(jnp.int32, (c, c), 0)
    acol = lax.broadcasted_iota(jnp.int32, (c, c), 1)

    def block_level(half):
        grp = 2 * half
        bg = b.reshape(c // grp, grp, HEAD_W)
        ref = jnp.broadcast_to(bg[:, half - 1:half, :], bg.shape).reshape(c, HEAD_W)
        upper = (row % grp) >= half
        qs = jnp.where(upper, q * jnp.exp2(jnp.minimum(b - ref, 0.0)), 0.0)
        ks = jnp.where(upper, 0.0, k * jnp.exp2(jnp.minimum(ref - b, 0.0)))
        a = _nt_dot(qs.astype(BF16), ks.astype(BF16))
        keep = ((arow // grp) == (acol // grp)) & ((arow % grp) >= half) & ((acol % grp) < half)
        return jnp.where(keep, a, 0.0)

    a_tot = None
    half = c // 2
    while half >= HG_SUB:
        lvl = block_level(half)
        a_tot = lvl if a_tot is None else a_tot + lvl
        half //= 2
    o = jnp.dot(a_tot.astype(BF16), v, preferred_element_type=F32)

    nsub = c // HG_SUB
    q3 = q.reshape(nsub, HG_SUB, HEAD_W)
    k3 = k.reshape(nsub, HG_SUB, HEAD_W)
    b3 = b.reshape(nsub, HG_SUB, HEAD_W)
    v3 = v.astype(F32).reshape(nsub, HG_SUB, HEAD_W)
    trow = lax.broadcasted_iota(jnp.int32, (nsub, HG_SUB, 1), 1)
    o3 = jnp.zeros((nsub, HG_SUB, HEAD_W), F32)
    for s in range(HG_SUB):
        e = jnp.exp2(jnp.minimum(b3 - b3[:, s:s + 1, :], 0.0))
        col = jnp.sum(q3 * (k3[:, s:s + 1, :] * e), axis=-1, keepdims=True)
        o3 = o3 + jnp.where(trow >= s, col, 0.0) * v3[:, s:s + 1, :]
    return o + o3.reshape(c, HEAD_W)


def _hgrn_kernel(gam_ref, ng_ref, q_ref, f_ref, i_ref, g_ref, o_ref, st_ref, *, n_chunks, layer_idx):
    @pl.when(pl.program_id(2) == 0)
    def _():
        st_ref[...] = jnp.zeros_like(st_ref)

    gam = gam_ref[...]
    ge = jnp.exp(gam - jnp.max(gam, axis=0, keepdims=True))
    lb = jnp.sum(ge[0:layer_idx + 1], axis=0, keepdims=True) / jnp.sum(ge, axis=0, keepdims=True)
    ng = ng_ref[...]
    c = q_ref.shape[0] // n_chunks
    tri = (lax.broadcasted_iota(jnp.int32, (c, c), 0) >= lax.broadcasted_iota(jnp.int32, (c, c), 1))
    tri = jnp.where(tri, 1.0, 0.0).astype(BF16)

    def chunk(ci, _):
        r0 = pl.multiple_of(ci * c, c)
        qr = q_ref[pl.ds(r0, c), :].astype(F32)
        fr = f_ref[pl.ds(r0, c), :].astype(F32)
        v = i_ref[pl.ds(r0, c), :]
        gr = g_ref[pl.ds(r0, c), :].astype(F32)
        q = qr * _sigmoid(qr)
        f = lb + (1.0 - lb) * _sigmoid(fr)
        logf = jnp.log2(f)
        k = 1.0 - f
        hi = logf.astype(BF16)
        r1 = logf - hi.astype(F32)
        mid = r1.astype(BF16)
        lo = (r1 - mid.astype(F32)).astype(BF16)
        b = (jnp.dot(tri, hi, preferred_element_type=F32) + jnp.dot(tri, mid, preferred_element_type=F32)
             + jnp.dot(tri, lo, preferred_element_type=F32))
        st = st_ref[...]
        o = _nt_dot((q * jnp.exp2(b)).astype(BF16), st.astype(BF16))
        o = o + _hgrn_chunk(q, k, b, v)
        bend = b[c - 1:c]
        kd = (k * jnp.exp2(bend - b)).astype(BF16)
        st_ref[...] = st * jnp.exp2(bend) + _tn_dot(v, kd)
        ms = jnp.mean(o * o, axis=-1, keepdims=True)
        on = o * lax.rsqrt(ms + EPS) * ng * (gr * _sigmoid(gr))
        o_ref[pl.ds(r0, c), :] = on.astype(o_ref.dtype)
        return 0

    per_trip = max(g for g in (8, 4, 2, 1) if n_chunks % g == 0)

    def trip(i, _):
        for dj in range(per_trip):
            chunk(per_trip * i + dj, 0)
        return 0

    lax.fori_loop(0, n_chunks // per_trip, trip, 0)


def hgrn2(proj, gamma, norm_g, batch, seq, *, tc=2048, layer_idx=0, col0=3 * DA_HEADS):
    n = proj.shape[0]
    tc = min(tc, seq)
    ng = seq // tc
    depth1 = gamma.shape[0]

    def spec(off):
        return pl.BlockSpec((tc, HEAD_W), lambda b, h, g: (b * ng + g, col0 + off * HG_HEADS + h))

    return pl.pallas_call(
        functools.partial(_hgrn_kernel, n_chunks=tc // min(HG_BLOCK, tc), layer_idx=layer_idx),
        grid=(batch, HG_HEADS, ng),
        in_specs=[pl.BlockSpec((depth1, HEAD_W), lambda b, h, g: (0, h)),
                  pl.BlockSpec((1, HEAD_W), lambda b, h, g: (0, 0)),
                  spec(0), spec(1), spec(2), spec(3)],
        out_specs=pl.BlockSpec((tc, HEAD_W), lambda b, h, g: (b * ng + g, h)),
        out_shape=jax.ShapeDtypeStruct((n, HG_HEADS * HEAD_W), BF16),
        scratch_shapes=[pltpu.VMEM((HEAD_W, HEAD_W), F32)],
        compiler_params=_cparams(("parallel", "parallel", "arbitrary")),
        name="hgrn2",
    )(gamma.astype(F32), norm_g.reshape(1, HEAD_W).astype(F32), proj, proj, proj, proj)


def _cross_attn_kernel(q_ref, kv_ref, o_ref, *, scale):
    d = q_ref.shape[1]
    hd = d // XA_HEADS
    for h in range(XA_HEADS):
        cols = slice(h * hd, (h + 1) * hd)
        q = q_ref[:, cols] * jnp.asarray(scale, BF16)
        s = _nt_dot(q, kv_ref[:, cols])
        m = jnp.max(s, axis=-1, keepdims=True)
        p = jnp.exp(s - m)
        l = jnp.sum(p, axis=-1, keepdims=True)
        o = jnp.dot(p.astype(BF16), kv_ref[:, d + h * hd:d + (h + 1) * hd], preferred_element_type=F32) / l
        o_ref[:, cols] = o.astype(o_ref.dtype)


def cross_attention(q, kv, batch, seq, mem_len, *, tq=1024):
    n, d = q.shape
    tq = min(tq, seq)
    nq = seq // tq
    return pl.pallas_call(
        functools.partial(_cross_attn_kernel, scale=(d // XA_HEADS) ** -0.5),
        grid=(batch, nq),
        in_specs=[pl.BlockSpec((tq, d), lambda b, i: (b * nq + i, 0)),
                  pl.BlockSpec((mem_len, 2 * d), lambda b, i: (b, 0))],
        out_specs=pl.BlockSpec((tq, d), lambda b, i: (b * nq + i, 0)),
        out_shape=jax.ShapeDtypeStruct((n, d), BF16),
        compiler_params=_cparams(("parallel", "parallel")),
        name="cross_attention",
    )(q, kv)


def _topk_desc(s, k):
    rows = s.shape[0]

    def extract(first_only):
        iota = lax.broadcasted_iota(jnp.int32, s.shape, 0)
        rank = jnp.full(s.shape, NOT_SELECTED, F32)
        vals = []
        cur = s
        for r in range(k):
            m = jnp.max(cur, axis=0, keepdims=True)
            sel = cur == m
            if first_only:
                sel = iota == jnp.min(jnp.where(sel, iota, rows), axis=0, keepdims=True)
            rank = jnp.where(sel, float(r), rank)
            cur = jnp.where(sel, -jnp.inf, cur)
            vals.append(m)
        return jnp.concatenate(vals, axis=0), rank, cur

    sv, rank, cur = extract(False)
    removed = jnp.sum(jnp.where(cur == -jnp.inf, 1.0, 0.0), axis=0, keepdims=True)
    return lax.cond(jnp.max(removed) == float(k), lambda: (sv, rank), lambda: extract(True)[:2])


def _peer_topk_kernel(q_ref, keys_ref, l1_ref, c_ref, r2_ref, e2_ref):
    kk = PEER_TOPK
    for h in range(PEER_HEADS):
        q1 = q_ref[:, (2 * h) * PEER_NKEYS:(2 * h + 1) * PEER_NKEYS]
        q2 = q_ref[:, (2 * h + 1) * PEER_NKEYS:(2 * h + 2) * PEER_NKEYS]
        s1 = _nt_dot(keys_ref[h, 0], q1)
        s2 = _nt_dot(keys_ref[h, 1], q2)
        t = s1.shape[1]
        sv12, rank12 = _topk_desc(jnp.concatenate([s1, s2], axis=1), kk)
        sv1, sv2, rank1, rank2 = sv12[:, :t], sv12[:, t:], rank12[:, :t], rank12[:, t:]
        segs, seg_rows = [], []
        for a in range(kk):
            nb = kk // (a + 1)
            if nb > 1:
                rows = -(-nb // SUBLANES) * SUBLANES
                seg = sv1[a:a + 1] + sv2[0:rows]
                if rows != nb:
                    seg = jnp.where(lax.broadcasted_iota(jnp.int32, seg.shape, 0) < nb, seg, -jnp.inf)
                segs.append(seg)
                seg_rows.append(rows)
        single0 = len(segs)
        n_single = kk - single0
        assert n_single % SUBLANES == 0
        segs.append(sv1[single0:kk] + sv2[0:1])
        cand = jnp.concatenate(segs, axis=0)
        best0 = sv1[0:1] + sv2[0:1]

        def pick(first_only, cand=cand, best0=best0):
            iota = lax.broadcasted_iota(jnp.int32, cand.shape, 0)
            picked = jnp.zeros(cand.shape, F32)
            z = jnp.zeros((1, t), F32)
            cur = cand
            for _ in range(kk):
                m = jnp.max(cur, axis=0, keepdims=True)
                sel = cur == m
                if first_only:
                    sel = iota == jnp.min(jnp.where(sel, iota, cand.shape[0]), axis=0, keepdims=True)
                cur = jnp.where(sel, -jnp.inf, cur)
                picked = jnp.where(sel, 1.0, picked)
                z = z + jnp.exp(m - best0)
            return picked, z

        picked, z = pick(False)
        n_picked = jnp.sum(picked, axis=0, keepdims=True)
        picked, z = lax.cond(jnp.max(n_picked) == float(kk), lambda p=picked, zz=z: (p, zz),
                             lambda: pick(True))
        count, r0 = [], 0
        for rows in seg_rows:
            count.append(jnp.sum(picked[r0:r0 + rows], axis=0, keepdims=True))
            r0 += rows
        count += [picked[r0 + i:r0 + i + 1] for i in range(n_single)]
        l1 = jnp.zeros_like(rank1)
        for a in range(kk):
            l1 = jnp.where(rank1 == float(a), count[a], l1)
        sel1 = rank1 < float(kk)
        l1_ref[h] = l1
        c_ref[h] = jnp.where(sel1, jnp.exp(jnp.minimum(s1 - sv1[0:1], 0.0)) / z, 0.0)
        r2_ref[h] = rank2.astype(r2_ref.dtype)
        e2 = jnp.where(rank2 < float(kk), jnp.exp(jnp.minimum(s2 - sv2[0:1], 0.0)), 0.0)
        e2_ref[h] = e2.astype(e2_ref.dtype)


def peer_topk(q, keys, *, tt=256):
    n = q.shape[0]
    tt = min(tt, n)
    tab_spec = pl.BlockSpec((PEER_HEADS, PEER_NKEYS, tt), lambda i: (0, 0, i))
    return pl.pallas_call(
        _peer_topk_kernel,
        grid=(n // tt,),
        in_specs=[pl.BlockSpec((tt, q.shape[1]), lambda i: (i, 0)),
                  pl.BlockSpec(keys.shape, lambda i: (0, 0, 0, 0))],
        out_specs=[tab_spec] * 4,
        out_shape=[jax.ShapeDtypeStruct((PEER_HEADS, PEER_NKEYS, n), dt) for dt in (F32, F32, BF16, BF16)],
        compiler_params=_cparams(("parallel",)),
        name="peer_topk",
    )(q, keys)


def _gelu_exact(x):
    return 0.5 * x * (1.0 + lax.erf(x * (2.0 ** -0.5)))


def _peer_dense_kernel(hn_ref, down_ref, up_ref, l1_ref, c_ref, r2_ref, e2_ref, o_ref, wg_ref, *, groups):
    s = pl.program_id(1)
    row0 = (s * groups) % SUBLANES

    @pl.when(s == 0)
    def _():
        o_ref[...] = jnp.zeros_like(o_ref)

    tt = o_ref.shape[1]
    lane_chunk = 256
    halves = 2
    gh = groups // halves
    wgs = []
    for hf in range(halves):
        er = slice(hf * gh * PEER_NKEYS, (hf + 1) * gh * PEER_NKEYS)
        act = _nt_dot(down_ref[er, :], hn_ref[...])
        for gl in range(gh):
            gi = hf * gh + gl
            rows = slice(gi * PEER_NKEYS, (gi + 1) * PEER_NKEYS)
            lrows = slice(gl * PEER_NKEYS, (gl + 1) * PEER_NKEYS)
            for t0 in range(0, tt, lane_chunk):
                lanes = slice(t0, t0 + lane_chunk)
                g = None
                for h in range(PEER_HEADS):
                    l1 = l1_ref[h, pl.ds(row0 + gi, 1), lanes].astype(BF16)
                    cc = c_ref[h, pl.ds(row0 + gi, 1), lanes].astype(BF16)
                    e2 = e2_ref[h, :, lanes]
                    term = jnp.where(r2_ref[h, :, lanes] < l1, e2 * cc, jnp.zeros_like(e2))
                    g = term if g is None else g + term
                wg_ref[rows, lanes] = g * _gelu_exact(act[lrows, lanes]).astype(BF16)
        wgs.append(wg_ref[er, :])
    acc = o_ref[...]
    for hf in range(halves):
        er = slice(hf * gh * PEER_NKEYS, (hf + 1) * gh * PEER_NKEYS)
        acc = acc + _tn_dot(up_ref[er, :], wgs[hf])
    o_ref[...] = acc


def peer_dense(hn, down, up, tables, *, tt=512, te=512):
    n, d = hn.shape
    n_exp = down.shape[0]
    tt = min(tt, n)
    groups = te // PEER_NKEYS
    nblk = n_exp // te
    l1, c, r2, e2 = tables
    assert SUBLANES % groups == 0 and groups % 2 == 0
    row_spec = pl.BlockSpec((PEER_HEADS, SUBLANES, tt), lambda i, s: (0, s * groups // SUBLANES, i))
    tab_spec = pl.BlockSpec((PEER_HEADS, PEER_NKEYS, tt), lambda i, s: (0, 0, i))
    return pl.pallas_call(
        functools.partial(_peer_dense_kernel, groups=groups),
        grid=(n // tt, nblk),
        in_specs=[pl.BlockSpec((tt, d), lambda i, s: (i, 0)),
                  pl.BlockSpec((te, d), lambda i, s: (s, 0)),
                  pl.BlockSpec((te, d), lambda i, s: (s, 0)),
                  row_spec, row_spec, tab_spec, tab_spec],
        out_specs=pl.BlockSpec((d, tt), lambda i, s: (0, i)),
        out_shape=jax.ShapeDtypeStruct((d, n), F32),
        scratch_shapes=[pltpu.VMEM((te, tt), BF16)],
        compiler_params=_cparams(("parallel", "arbitrary")),
        name="peer_dense",
    )(hn, down, up, l1, c, r2, e2)


def _final_norm_kernel(h_ref, pt_ref, g_ref, o_ref):
    x = h_ref[...] + pt_ref[...].T
    ms = jnp.mean(x * x, axis=-1, keepdims=True)
    o_ref[...] = x * lax.rsqrt(ms + EPS) * g_ref[...]


def final_norm(h, pt, g, *, tm=256):
    m, d = h.shape
    tm = min(tm, m)
    row = pl.BlockSpec((tm, d), lambda i: (i, 0))
    return pl.pallas_call(
        _final_norm_kernel,
        grid=(m // tm,),
        in_specs=[row, pl.BlockSpec((d, tm), lambda i: (0, i)), pl.BlockSpec((1, d), lambda i: (0, 0))],
        out_specs=row,
        out_shape=jax.ShapeDtypeStruct((m, d), F32),
        compiler_params=_cparams(("parallel",)),
        name="final_norm",
    )(h, pt, g.reshape(1, d).astype(F32))


def kernel(x, mem, mix_norm_g, w_in, da_lambda_q1, da_lambda_k1, da_lambda_q2, da_lambda_k2, da_subln_g, hgrn_gamma, hgrn_norm_g, w_out, cross_norm_g, mem_norm_g, w_cq, w_ckv, w_co, ffn_norm_g, peer_wq, peer_subkeys, peer_down, peer_up, final_norm_g):
    batch, seq, d = x.shape
    mem_len = mem.shape[1]
    depth = w_in.shape[0]
    assert depth == 1 and hgrn_gamma.shape[0] == depth + 1
    l = 0
    da_w = DA_HEADS * HEAD_W
    x2 = x.reshape(batch * seq, d)

    proj = norm_matmul(x2, mix_norm_g[l], w_in[l].astype(BF16), tn=w_in.shape[-1] // 8)
    lam4 = jnp.stack([da_lambda_q1[l], da_lambda_k1[l], da_lambda_q2[l], da_lambda_k2[l]]).astype(F32)
    a_out = diff_attention(proj, lam4, da_subln_g[l], batch, seq, layer_idx=l)
    b_out = hgrn2(proj, hgrn_gamma, hgrn_norm_g[l], batch, seq, layer_idx=l)
    wo = w_out[l].astype(BF16)
    h1 = matmul_residual([a_out, b_out], [wo[:da_w], wo[da_w:]], x2)

    qc = norm_matmul(h1, cross_norm_g[l], w_cq[l].astype(BF16))
    kv = norm_matmul(mem.reshape(batch * mem_len, d), mem_norm_g[l], w_ckv[l].astype(BF16))
    oc = cross_attention(qc, kv, batch, seq, mem_len)
    h2 = matmul_residual([oc], [w_co[l].astype(BF16)], h1)

    qp, hn = norm_matmul(h2, ffn_norm_g[l], peer_wq[l].astype(BF16), emit_xn=True)
    tables = peer_topk(qp, peer_subkeys[l].astype(BF16))
    peer_t = peer_dense(hn, peer_down[l].astype(BF16), peer_up[l].astype(BF16), tables)
    out = final_norm(h2, peer_t, final_norm_g)
    return out.reshape(batch, seq, d)
```
